```python
import math
import jax, jax.numpy as jnp
from jax import lax
import numpy as np

D_MODEL = 2048
BATCH = 8
SEQ = 8192
DEPTH = 4

N_MEM = 256
D_MIX = 2 * D_MODEL
D_SSD = D_MIX // 2
D_SC = D_MIX - D_SSD
SSD_HEADDIM = 64
SSD_HEADS = D_SSD // SSD_HEADDIM
SSD_GROUPS = 4
SSD_STATE = 128
SSD_CONV = 4
SSD_CHUNK = 256
D_XBC = D_SSD + 2 * SSD_GROUPS * SSD_STATE
DT_MIN = 1e-3
DT_MAX = 1e-1
SC_CONV = 3
SC_GROUPS = 16
XA_HEADS = 4
XA_HEADDIM = 128
D_XA = XA_HEADS * XA_HEADDIM
D_FF = ((8 * D_MODEL + 3 * 256 - 1) // (3 * 256)) * 256
NORM_EPS = 1e-5
D_IN_PROJ = D_SSD + D_XBC + SSD_HEADS + 3 * D_SC
IN_SPLITS = [D_SSD,
             D_SSD + D_XBC,
             D_SSD + D_XBC + SSD_HEADS,
             D_SSD + D_XBC + SSD_HEADS + D_SC,
             D_SSD + D_XBC + SSD_HEADS + 2 * D_SC]

kernel_name = "hybrid_ssd_shortconv_memxattn_trunk"


def rmsnorm(x, g):
    xf = x.astype(jnp.float32)
    xf = xf * lax.rsqrt(jnp.mean(xf * xf, axis=-1, keepdims=True) + NORM_EPS)
    return (xf * g.astype(jnp.float32)).astype(x.dtype)


def grouped_rmsnorm(x, g, n_groups):
    shp = x.shape
    xf = x.astype(jnp.float32).reshape(shp[:-1] + (n_groups, shp[-1] // n_groups))
    xf = xf * lax.rsqrt(jnp.mean(xf * xf, axis=-1, keepdims=True) + NORM_EPS)
    return (xf.reshape(shp) * g.astype(jnp.float32)).astype(x.dtype)


def causal_dwconv(x, w):
    k, c = w.shape
    return lax.conv_general_dilated(
        x, w[:, None, :].astype(x.dtype), window_strides=(1,), padding=[(k - 1, 0)],
        dimension_numbers=("NWC", "WIO", "NWC"), feature_group_count=c)


def ssd_chunked(xh, dt, a, bm, cm):
    f32 = jnp.float32
    bsz, seqlen, nh, hd = xh.shape
    g, n = bm.shape[2], bm.shape[3]
    r = nh // g
    pad = (-seqlen) % SSD_CHUNK
    xh, dt, bm, cm = (t.astype(f32) for t in (xh, dt, bm, cm))
    if pad:
        xh = jnp.pad(xh, ((0, 0), (0, pad), (0, 0), (0, 0)))
        dt = jnp.pad(dt, ((0, 0), (0, pad), (0, 0)))
        bm = jnp.pad(bm, ((0, 0), (0, pad), (0, 0), (0, 0)))
        cm = jnp.pad(cm, ((0, 0), (0, pad), (0, 0), (0, 0)))
    nc = (seqlen + pad) // SSD_CHUNK
    L = SSD_CHUNK
    x = (xh * dt[..., None]).reshape(bsz, nc, L, g, r, hd)
    a_dt = jnp.moveaxis((dt * a.astype(f32)).reshape(bsz, nc, L, g, r), 2, -1)
    a_cs = jnp.cumsum(a_dt, axis=-1)
    bc = bm.reshape(bsz, nc, L, g, n)
    cc = cm.reshape(bsz, nc, L, g, n)
    causal = jnp.tril(jnp.ones((L, L), dtype=bool))
    decay = jnp.exp(jnp.where(causal, a_cs[..., :, None] - a_cs[..., None, :], -jnp.inf))
    cb = jnp.einsum("bclgn,bcsgn->bcgls", cc, bc)
    scores = cb[:, :, :, None] * decay
    y_diag = jnp.einsum("bcgrls,bcsgrp->bclgrp", scores, x)
    decay_states = jnp.exp(a_cs[..., -1:] - a_cs)
    states = jnp.einsum("bclgn,bcgrl,bclgrp->bcgrpn", bc, decay_states, x)
    chunk_decay = jnp.exp(a_cs[..., -1])

    def step(h, inp):
        s_c, d_c = inp
        return h * d_c[..., None, None] + s_c, h

    h0 = jnp.zeros_like(states[:, 0])
    _, prev = lax.scan(step, h0, (jnp.moveaxis(states, 1, 0), jnp.moveaxis(chunk_decay, 1, 0)))
    prev = jnp.moveaxis(prev, 0, 1)
    y_off = jnp.einsum("bclgn,bcgrpn,bcgrl->bclgrp", cc, prev, jnp.exp(a_cs))
    y = (y_diag + y_off).reshape(bsz, nc * L, nh, hd)
    return y[:, :seqlen]


def hybrid_mixer(h, w_in, ssd_conv_w, ssd_conv_b, dt_bias, a_log, d_skip, ssd_norm,
                 sc_conv_w, sc_norm, w_out):
    bsz, seqlen, _ = h.shape
    proj = h @ w_in
    z, xbc, dt_raw, sc_u, sc_b, sc_c = jnp.split(proj, IN_SPLITS, axis=-1)
    xbc = jax.nn.silu(causal_dwconv(xbc, ssd_conv_w) + ssd_conv_b)
    xs, bm, cm = jnp.split(xbc, [D_SSD, D_SSD + SSD_GROUPS * SSD_STATE], axis=-1)
    xh = xs.reshape(bsz, seqlen, SSD_HEADS, SSD_HEADDIM)
    dt = jax.nn.softplus(dt_raw.astype(jnp.float32) + dt_bias.astype(jnp.float32))
    a = -jnp.exp(a_log.astype(jnp.float32))
    y = ssd_chunked(xh, dt, a,
                    bm.reshape(bsz, seqlen, SSD_GROUPS, SSD_STATE),
                    cm.reshape(bsz, seqlen, SSD_GROUPS, SSD_STATE))
    y = y + xh.astype(jnp.float32) * d_skip.astype(jnp.float32)[:, None]
    y = y.reshape(bsz, seqlen, D_SSD).astype(h.dtype)
    y_ssd = grouped_rmsnorm(y * jax.nn.silu(z), ssd_norm, SSD_GROUPS)
    v = sc_b * causal_dwconv(sc_c * sc_u, sc_conv_w)
    y_sc = grouped_rmsnorm(v, sc_norm, SC_GROUPS)
    return jnp.concatenate([y_ssd, y_sc], axis=-1) @ w_out


def memory_cross_attention(h, mem_n, w_q, w_k, w_v, w_o):
    bsz, seqlen, _ = h.shape
    n_mem = mem_n.shape[1]
    q = (h @ w_q).reshape(bsz, seqlen, XA_HEADS, XA_HEADDIM)
    k = (mem_n @ w_k).reshape(bsz, n_mem, XA_HEADS, XA_HEADDIM)
    v = (mem_n @ w_v).reshape(bsz, n_mem, XA_HEADS, XA_HEADDIM)
    scores = jnp.einsum("bshd,bmhd->bhsm", q, k).astype(jnp.float32) * (XA_HEADDIM ** -0.5)
    p = jax.nn.softmax(scores, axis=-1).astype(v.dtype)
    o = jnp.einsum("bhsm,bmhd->bshd", p, v).reshape(bsz, seqlen, D_XA)
    return o @ w_o


def swiglu(h, w_gate, w_up, w_down):
    return (jax.nn.silu(h @ w_gate) * (h @ w_up)) @ w_down


def _fwd_setup_inputs(seed: int = 0) -> dict:
    key = jax.random.key(seed)
    ks = jax.random.split(key, 24)
    f32 = jnp.float32

    def nrm(k, shape, scale):
        return jax.random.normal(k, shape, f32) * scale

    def gain(k, shape):
        return 1.0 + 0.02 * jax.random.normal(k, shape, f32)

    x = nrm(ks[0], (BATCH, SEQ, D_MODEL), 1.0)
    mem = nrm(ks[1], (BATCH, N_MEM, D_MODEL), 1.0)
    norm_mix = gain(ks[2], (DEPTH, D_MODEL))
    w_in = nrm(ks[3], (DEPTH, D_MODEL, D_IN_PROJ), D_MODEL ** -0.5)
    ssd_conv_w = nrm(ks[4], (DEPTH, SSD_CONV, D_XBC), SSD_CONV ** -0.5)
    ssd_conv_b = nrm(ks[5], (DEPTH, D_XBC), 0.02)
    dt0 = jnp.exp(jax.random.uniform(ks[6], (DEPTH, SSD_HEADS), f32, math.log(DT_MIN), math.log(DT_MAX)))
    dt_bias = dt0 + jnp.log(-jnp.expm1(-dt0))
    a_log = jnp.log(jax.random.uniform(ks[7], (DEPTH, SSD_HEADS), f32, 1.0, 16.0))
    d_skip = gain(ks[8], (DEPTH, SSD_HEADS))
    ssd_norm = gain(ks[9], (DEPTH, D_SSD))
    sc_conv_w = nrm(ks[10], (DEPTH, SC_CONV, D_SC), SC_CONV ** -0.5)
    sc_norm = gain(ks[11], (DEPTH, D_SC))
    w_out = nrm(ks[12], (DEPTH, D_MIX, D_MODEL), D_MIX ** -0.5)
    mem_norm = gain(ks[13], (D_MODEL,))
    norm_xa = gain(ks[14], (DEPTH, D_MODEL))
    w_q = nrm(ks[15], (DEPTH, D_MODEL, D_XA), D_MODEL ** -0.5)
    w_k = nrm(ks[16], (DEPTH, D_MODEL, D_XA), D_MODEL ** -0.5)
    w_v = nrm(ks[17], (DEPTH, D_MODEL, D_XA), D_MODEL ** -0.5)
    w_o = nrm(ks[18], (DEPTH, D_XA, D_MODEL), D_XA ** -0.5)
    norm_ffn = gain(ks[19], (DEPTH, D_MODEL))
    w_gate = nrm(ks[20], (DEPTH, D_MODEL, D_FF), D_MODEL ** -0.5)
    w_up = nrm(ks[21], (DEPTH, D_MODEL, D_FF), D_MODEL ** -0.5)
    w_down = nrm(ks[22], (DEPTH, D_FF, D_MODEL), D_FF ** -0.5)
    norm_final = gain(ks[23], (D_MODEL,))
    return {"x": x, "mem": mem, "norm_mix": norm_mix, "w_in": w_in,
            "ssd_conv_w": ssd_conv_w, "ssd_conv_b": ssd_conv_b, "dt_bias": dt_bias,
            "a_log": a_log, "d_skip": d_skip, "ssd_norm": ssd_norm,
            "sc_conv_w": sc_conv_w, "sc_norm": sc_norm, "w_out": w_out,
            "mem_norm": mem_norm, "norm_xa": norm_xa, "w_q": w_q, "w_k": w_k,
            "w_v": w_v, "w_o": w_o, "norm_ffn": norm_ffn, "w_gate": w_gate,
            "w_up": w_up, "w_down": w_down, "norm_final": norm_final}


def _fwd_reference(x, mem, norm_mix, w_in, ssd_conv_w, ssd_conv_b, dt_bias, a_log, d_skip,
              ssd_norm, sc_conv_w, sc_norm, w_out, mem_norm, norm_xa, w_q, w_k, w_v, w_o,
              norm_ffn, w_gate, w_up, w_down, norm_final):
    mem_n = rmsnorm(mem, mem_norm)
    h = x
    for i in range(DEPTH):
        h = h + hybrid_mixer(rmsnorm(h, norm_mix[i]), w_in[i], ssd_conv_w[i], ssd_conv_b[i],
                             dt_bias[i], a_log[i], d_skip[i], ssd_norm[i],
                             sc_conv_w[i], sc_norm[i], w_out[i])
        h = h + memory_cross_attention(rmsnorm(h, norm_xa[i]), mem_n, w_q[i], w_k[i], w_v[i], w_o[i])
        h = h + swiglu(rmsnorm(h, norm_ffn[i]), w_gate[i], w_up[i], w_down[i])
    return rmsnorm(h, norm_final)


import jax as _jax
import jax.numpy as _jnp

TWIN_FORMAT = 'train_step'
FWD_PARAMS = ['x', 'mem', 'norm_mix', 'w_in', 'ssd_conv_w', 'ssd_conv_b', 'dt_bias', 'a_log', 'd_skip', 'ssd_norm', 'sc_conv_w', 'sc_norm', 'w_out', 'mem_norm', 'norm_xa', 'w_q', 'w_k', 'w_v', 'w_o', 'norm_ffn', 'w_gate', 'w_up', 'w_down', 'norm_final']
TWIN_WEIGHTS = ['norm_mix', 'w_in', 'ssd_conv_w', 'ssd_conv_b', 'dt_bias', 'a_log', 'd_skip', 'ssd_norm', 'sc_conv_w', 'sc_norm', 'w_out', 'mem_norm', 'norm_xa', 'w_q', 'w_k', 'w_v', 'w_o', 'norm_ffn', 'w_gate', 'w_up', 'w_down', 'norm_final']
TWIN_DIFF_INPUT = 'x'
TWIN_INPUTS = ['x', 'mem', 'norm_mix', 'w_in', 'ssd_conv_w', 'ssd_conv_b', 'dt_bias', 'a_log', 'd_skip', 'ssd_norm', 'sc_conv_w', 'sc_norm', 'w_out', 'mem_norm', 'norm_xa', 'w_q', 'w_k', 'w_v', 'w_o', 'norm_ffn', 'w_gate', 'w_up', 'w_down', 'norm_final', 'loss_target', 'm_norm_mix', 'm_w_in', 'm_ssd_conv_w', 'm_ssd_conv_b', 'm_dt_bias', 'm_a_log', 'm_d_skip', 'm_ssd_norm', 'm_sc_conv_w', 'm_sc_norm', 'm_w_out', 'm_mem_norm', 'm_norm_xa', 'm_w_q', 'm_w_k', 'm_w_v', 'm_w_o', 'm_norm_ffn', 'm_w_gate', 'm_w_up', 'm_w_down', 'm_norm_final', 'v_norm_mix', 'v_w_in', 'v_ssd_conv_w', 'v_ssd_conv_b', 'v_dt_bias', 'v_a_log', 'v_d_skip', 'v_ssd_norm', 'v_sc_conv_w', 'v_sc_norm', 'v_w_out', 'v_mem_norm', 'v_norm_xa', 'v_w_q', 'v_w_k', 'v_w_v', 'v_w_o', 'v_norm_ffn', 'v_w_gate', 'v_w_up', 'v_w_down', 'v_norm_final']
TWIN_OUTPUTS = ['loss', 'grad_x', 'grad_norm_mix', 'grad_w_in', 'grad_ssd_conv_w', 'grad_ssd_conv_b', 'grad_dt_bias', 'grad_a_log', 'grad_d_skip', 'grad_ssd_norm', 'grad_sc_conv_w', 'grad_sc_norm', 'grad_w_out', 'grad_mem_norm', 'grad_norm_xa', 'grad_w_q', 'grad_w_k', 'grad_w_v', 'grad_w_o', 'grad_norm_ffn', 'grad_w_gate', 'grad_w_up', 'grad_w_down', 'grad_norm_final', 'delta_norm_mix', 'delta_w_in', 'delta_ssd_conv_w', 'delta_ssd_conv_b', 'delta_dt_bias', 'delta_a_log', 'delta_d_skip', 'delta_ssd_norm', 'delta_sc_conv_w', 'delta_sc_norm', 'delta_w_out', 'delta_mem_norm', 'delta_norm_xa', 'delta_w_q', 'delta_w_k', 'delta_w_v', 'delta_w_o', 'delta_norm_ffn', 'delta_w_gate', 'delta_w_up', 'delta_w_down', 'delta_norm_final', 'new_m_norm_mix', 'new_m_w_in', 'new_m_ssd_conv_w', 'new_m_ssd_conv_b', 'new_m_dt_bias', 'new_m_a_log', 'new_m_d_skip', 'new_m_ssd_norm', 'new_m_sc_conv_w', 'new_m_sc_norm', 'new_m_w_out', 'new_m_mem_norm', 'new_m_norm_xa', 'new_m_w_q', 'new_m_w_k', 'new_m_w_v', 'new_m_w_o', 'new_m_norm_ffn', 'new_m_w_gate', 'new_m_w_up', 'new_m_w_down', 'new_m_norm_final', 'new_v_norm_mix', 'new_v_w_in', 'new_v_ssd_conv_w', 'new_v_ssd_conv_b', 'new_v_dt_bias', 'new_v_a_log', 'new_v_d_skip', 'new_v_ssd_norm', 'new_v_sc_conv_w', 'new_v_sc_norm', 'new_v_w_out', 'new_v_mem_norm', 'new_v_norm_xa', 'new_v_w_q', 'new_v_w_k', 'new_v_w_v', 'new_v_w_o', 'new_v_norm_ffn', 'new_v_w_gate', 'new_v_w_up', 'new_v_w_down', 'new_v_norm_final']
TWIN_LEAF_KINDS = {'loss': 'loss', 'grad_x': 'grad_x', 'grad_norm_mix': 'grad_w', 'grad_w_in': 'grad_w', 'grad_ssd_conv_w': 'grad_w', 'grad_ssd_conv_b': 'grad_w', 'grad_dt_bias': 'grad_w', 'grad_a_log': 'grad_w', 'grad_d_skip': 'grad_w', 'grad_ssd_norm': 'grad_w', 'grad_sc_conv_w': 'grad_w', 'grad_sc_norm': 'grad_w', 'grad_w_out': 'grad_w', 'grad_mem_norm': 'grad_w', 'grad_norm_xa': 'grad_w', 'grad_w_q': 'grad_w', 'grad_w_k': 'grad_w', 'grad_w_v': 'grad_w', 'grad_w_o': 'grad_w', 'grad_norm_ffn': 'grad_w', 'grad_w_gate': 'grad_w', 'grad_w_up': 'grad_w', 'grad_w_down': 'grad_w', 'grad_norm_final': 'grad_w', 'delta_norm_mix': 'delta_w', 'delta_w_in': 'delta_w', 'delta_ssd_conv_w': 'delta_w', 'delta_ssd_conv_b': 'delta_w', 'delta_dt_bias': 'delta_w', 'delta_a_log': 'delta_w', 'delta_d_skip': 'delta_w', 'delta_ssd_norm': 'delta_w', 'delta_sc_conv_w': 'delta_w', 'delta_sc_norm': 'delta_w', 'delta_w_out': 'delta_w', 'delta_mem_norm': 'delta_w', 'delta_norm_xa': 'delta_w', 'delta_w_q': 'delta_w', 'delta_w_k': 'delta_w', 'delta_w_v': 'delta_w', 'delta_w_o': 'delta_w', 'delta_norm_ffn': 'delta_w', 'delta_w_gate': 'delta_w', 'delta_w_up': 'delta_w', 'delta_w_down': 'delta_w', 'delta_norm_final': 'delta_w', 'new_m_norm_mix': 'new_m', 'new_m_w_in': 'new_m', 'new_m_ssd_conv_w': 'new_m', 'new_m_ssd_conv_b': 'new_m', 'new_m_dt_bias': 'new_m', 'new_m_a_log': 'new_m', 'new_m_d_skip': 'new_m', 'new_m_ssd_norm': 'new_m', 'new_m_sc_conv_w': 'new_m', 'new_m_sc_norm': 'new_m', 'new_m_w_out': 'new_m', 'new_m_mem_norm': 'new_m', 'new_m_norm_xa': 'new_m', 'new_m_w_q': 'new_m', 'new_m_w_k': 'new_m', 'new_m_w_v': 'new_m', 'new_m_w_o': 'new_m', 'new_m_norm_ffn': 'new_m', 'new_m_w_gate': 'new_m', 'new_m_w_up': 'new_m', 'new_m_w_down': 'new_m', 'new_m_norm_final': 'new_m', 'new_v_norm_mix': 'new_v', 'new_v_w_in': 'new_v', 'new_v_ssd_conv_w': 'new_v', 'new_v_ssd_conv_b': 'new_v', 'new_v_dt_bias': 'new_v', 'new_v_a_log': 'new_v', 'new_v_d_skip': 'new_v', 'new_v_ssd_norm': 'new_v', 'new_v_sc_conv_w': 'new_v', 'new_v_sc_norm': 'new_v', 'new_v_w_out': 'new_v', 'new_v_mem_norm': 'new_v', 'new_v_norm_xa': 'new_v', 'new_v_w_q': 'new_v', 'new_v_w_k': 'new_v', 'new_v_w_v': 'new_v', 'new_v_w_o': 'new_v', 'new_v_norm_ffn': 'new_v', 'new_v_w_gate': 'new_v', 'new_v_w_up': 'new_v', 'new_v_w_down': 'new_v', 'new_v_norm_final': 'new_v'}


def _forward(args):
    return _fwd_reference(*[args[k] for k in FWD_PARAMS])


def _output_shape():
    def fwd():
        inp = _fwd_setup_inputs(0)
        return _fwd_reference(*[inp[k] for k in FWD_PARAMS])
    out = _jax.eval_shape(fwd)
    return out.shape, out.dtype

N_MICROBATCH = 1
ADAM_LR = 0.001
ADAM_B1 = 0.9
ADAM_B2 = 0.999
ADAM_EPS = 1e-08
ADAM_WD = 0.01
ADAM_STEP = 10
PER_EXAMPLE_BATCH_AXIS = {'x': 0, 'mem': 0, 'loss_target': 0}
SHARED_INPUTS = []
_WEIGHT_DTYPES = {'norm_mix': _jnp.float32, 'w_in': _jnp.float32, 'ssd_conv_w': _jnp.float32, 'ssd_conv_b': _jnp.float32, 'dt_bias': _jnp.float32, 'a_log': _jnp.float32, 'd_skip': _jnp.float32, 'ssd_norm': _jnp.float32, 'sc_conv_w': _jnp.float32, 'sc_norm': _jnp.float32, 'w_out': _jnp.float32, 'mem_norm': _jnp.float32, 'norm_xa': _jnp.float32, 'w_q': _jnp.float32, 'w_k': _jnp.float32, 'w_v': _jnp.float32, 'w_o': _jnp.float32, 'norm_ffn': _jnp.float32, 'w_gate': _jnp.float32, 'w_up': _jnp.float32, 'w_down': _jnp.float32, 'norm_final': _jnp.float32}
MOMENT_SCALE = {'norm_mix': 1.832386e-01, 'w_in': 7.705915e-02, 'ssd_conv_w': 6.673210e-02, 'ssd_conv_b': 9.504238e-02, 'dt_bias': 2.140834e-01, 'a_log': 2.101822e-01, 'd_skip': 4.267651e-01, 'ssd_norm': 7.825214e-02, 'sc_conv_w': 8.004111e-02, 'sc_norm': 7.812756e-02, 'w_out': 1.095127e-01, 'mem_norm': 3.397733e-02, 'norm_xa': 1.142642e-02, 'w_q': 2.235741e-02, 'w_k': 2.245710e-02, 'w_v': 2.334775e-02, 'w_o': 1.176833e-02, 'norm_ffn': 8.260968e-02, 'w_gate': 3.567882e-02, 'w_up': 3.455922e-02, 'w_down': 5.730404e-02, 'norm_final': 3.195367e+01}


def _to_microbatches(a, axis):
    t = _jnp.moveaxis(a, axis, 0)
    t = t.reshape((N_MICROBATCH, t.shape[0] // N_MICROBATCH) + t.shape[1:])
    return _jnp.moveaxis(t, 1, axis + 1)


def setup_inputs(seed: int = 0) -> dict:
    inp = _fwd_setup_inputs(seed)
    key = _jax.random.fold_in(_jax.random.key(seed), 7919)
    shape, _ = _output_shape()
    out = dict(inp)
    out["loss_target"] = _jax.random.normal(_jax.random.fold_in(key, 0), shape, _jnp.float32)
    for i, name in enumerate(TWIN_WEIGHTS):
        w = inp[name].astype(_jnp.float32)
        if MOMENT_SCALE is None:
            s = _jnp.sqrt(_jnp.mean(_jnp.square(w)) + 1e-30)
        else:
            s = MOMENT_SCALE[name]
        km, kv = _jax.random.split(_jax.random.fold_in(key, i + 1))
        out[name] = w
        out["m_" + name] = s * _jax.random.normal(km, w.shape, _jnp.float32)
        out["v_" + name] = (s * s) * _jax.random.uniform(kv, w.shape, _jnp.float32, 0.5, 1.5)
    if N_MICROBATCH > 1:
        for name, axis in PER_EXAMPLE_BATCH_AXIS.items():
            out[name] = _to_microbatches(out[name], axis)
    return {'x': out['x'], 'mem': out['mem'], 'norm_mix': out['norm_mix'], 'w_in': out['w_in'], 'ssd_conv_w': out['ssd_conv_w'], 'ssd_conv_b': out['ssd_conv_b'], 'dt_bias': out['dt_bias'], 'a_log': out['a_log'], 'd_skip': out['d_skip'], 'ssd_norm': out['ssd_norm'], 'sc_conv_w': out['sc_conv_w'], 'sc_norm': out['sc_norm'], 'w_out': out['w_out'], 'mem_norm': out['mem_norm'], 'norm_xa': out['norm_xa'], 'w_q': out['w_q'], 'w_k': out['w_k'], 'w_v': out['w_v'], 'w_o': out['w_o'], 'norm_ffn': out['norm_ffn'], 'w_gate': out['w_gate'], 'w_up': out['w_up'], 'w_down': out['w_down'], 'norm_final': out['norm_final'], 'loss_target': out['loss_target'], 'm_norm_mix': out['m_norm_mix'], 'm_w_in': out['m_w_in'], 'm_ssd_conv_w': out['m_ssd_conv_w'], 'm_ssd_conv_b': out['m_ssd_conv_b'], 'm_dt_bias': out['m_dt_bias'], 'm_a_log': out['m_a_log'], 'm_d_skip': out['m_d_skip'], 'm_ssd_norm': out['m_ssd_norm'], 'm_sc_conv_w': out['m_sc_conv_w'], 'm_sc_norm': out['m_sc_norm'], 'm_w_out': out['m_w_out'], 'm_mem_norm': out['m_mem_norm'], 'm_norm_xa': out['m_norm_xa'], 'm_w_q': out['m_w_q'], 'm_w_k': out['m_w_k'], 'm_w_v': out['m_w_v'], 'm_w_o': out['m_w_o'], 'm_norm_ffn': out['m_norm_ffn'], 'm_w_gate': out['m_w_gate'], 'm_w_up': out['m_w_up'], 'm_w_down': out['m_w_down'], 'm_norm_final': out['m_norm_final'], 'v_norm_mix': out['v_norm_mix'], 'v_w_in': out['v_w_in'], 'v_ssd_conv_w': out['v_ssd_conv_w'], 'v_ssd_conv_b': out['v_ssd_conv_b'], 'v_dt_bias': out['v_dt_bias'], 'v_a_log': out['v_a_log'], 'v_d_skip': out['v_d_skip'], 'v_ssd_norm': out['v_ssd_norm'], 'v_sc_conv_w': out['v_sc_conv_w'], 'v_sc_norm': out['v_sc_norm'], 'v_w_out': out['v_w_out'], 'v_mem_norm': out['v_mem_norm'], 'v_norm_xa': out['v_norm_xa'], 'v_w_q': out['v_w_q'], 'v_w_k': out['v_w_k'], 'v_w_v': out['v_w_v'], 'v_w_o': out['v_w_o'], 'v_norm_ffn': out['v_norm_ffn'], 'v_w_gate': out['v_w_gate'], 'v_w_up': out['v_w_up'], 'v_w_down': out['v_w_down'], 'v_norm_final': out['v_norm_final']}


def _loss(weights, diff, rest, loss_target):
    with _jax.named_scope("forward"):
        args = {**rest, TWIN_DIFF_INPUT: diff, **{k: w.astype(_WEIGHT_DTYPES[k]) for k, w in weights.items()}}
        y = _forward(args)
    with _jax.named_scope("loss_head"):
        err = _jnp.square(y.astype(_jnp.float32) - loss_target)
        return 0.5 * _jnp.sum(_jnp.mean(err, axis=-1)) if err.ndim else 0.5 * err


def _adamw(w, g, m, v):
    m = ADAM_B1 * m + (1.0 - ADAM_B1) * g
    v = ADAM_B2 * v + (1.0 - ADAM_B2) * _jnp.square(g)
    m_hat = m / (1.0 - ADAM_B1 ** ADAM_STEP)
    v_hat = v / (1.0 - ADAM_B2 ** ADAM_STEP)
    delta = -ADAM_LR * (m_hat / (_jnp.sqrt(v_hat) + ADAM_EPS) + ADAM_WD * w)
    return delta, m, v


def reference(x, mem, norm_mix, w_in, ssd_conv_w, ssd_conv_b, dt_bias, a_log, d_skip, ssd_norm, sc_conv_w, sc_norm, w_out, mem_norm, norm_xa, w_q, w_k, w_v, w_o, norm_ffn, w_gate, w_up, w_down, norm_final, loss_target, m_norm_mix, m_w_in, m_ssd_conv_w, m_ssd_conv_b, m_dt_bias, m_a_log, m_d_skip, m_ssd_norm, m_sc_conv_w, m_sc_norm, m_w_out, m_mem_norm, m_norm_xa, m_w_q, m_w_k, m_w_v, m_w_o, m_norm_ffn, m_w_gate, m_w_up, m_w_down, m_norm_final, v_norm_mix, v_w_in, v_ssd_conv_w, v_ssd_conv_b, v_dt_bias, v_a_log, v_d_skip, v_ssd_norm, v_sc_conv_w, v_sc_norm, v_w_out, v_mem_norm, v_norm_xa, v_w_q, v_w_k, v_w_v, v_w_o, v_norm_ffn, v_w_gate, v_w_up, v_w_down, v_norm_final):
    given = dict(x=x, mem=mem, norm_mix=norm_mix, w_in=w_in, ssd_conv_w=ssd_conv_w, ssd_conv_b=ssd_conv_b, dt_bias=dt_bias, a_log=a_log, d_skip=d_skip, ssd_norm=ssd_norm, sc_conv_w=sc_conv_w, sc_norm=sc_norm, w_out=w_out, mem_norm=mem_norm, norm_xa=norm_xa, w_q=w_q, w_k=w_k, w_v=w_v, w_o=w_o, norm_ffn=norm_ffn, w_gate=w_gate, w_up=w_up, w_down=w_down, norm_final=norm_final, loss_target=loss_target, m_norm_mix=m_norm_mix, m_w_in=m_w_in, m_ssd_conv_w=m_ssd_conv_w, m_ssd_conv_b=m_ssd_conv_b, m_dt_bias=m_dt_bias, m_a_log=m_a_log, m_d_skip=m_d_skip, m_ssd_norm=m_ssd_norm, m_sc_conv_w=m_sc_conv_w, m_sc_norm=m_sc_norm, m_w_out=m_w_out, m_mem_norm=m_mem_norm, m_norm_xa=m_norm_xa, m_w_q=m_w_q, m_w_k=m_w_k, m_w_v=m_w_v, m_w_o=m_w_o, m_norm_ffn=m_norm_ffn, m_w_gate=m_w_gate, m_w_up=m_w_up, m_w_down=m_w_down, m_norm_final=m_norm_final, v_norm_mix=v_norm_mix, v_w_in=v_w_in, v_ssd_conv_w=v_ssd_conv_w, v_ssd_conv_b=v_ssd_conv_b, v_dt_bias=v_dt_bias, v_a_log=v_a_log, v_d_skip=v_d_skip, v_ssd_norm=v_ssd_norm, v_sc_conv_w=v_sc_conv_w, v_sc_norm=v_sc_norm, v_w_out=v_w_out, v_mem_norm=v_mem_norm, v_norm_xa=v_norm_xa, v_w_q=v_w_q, v_w_k=v_w_k, v_w_v=v_w_v, v_w_o=v_w_o, v_norm_ffn=v_norm_ffn, v_w_gate=v_w_gate, v_w_up=v_w_up, v_w_down=v_w_down, v_norm_final=v_norm_final)
    weights = {n: given[n] for n in TWIN_WEIGHTS}
    shared = {n: given[n] for n in SHARED_INPUTS}
    per_example = {n: given[n] for n in ['x', 'mem']}
    grad_fn = _jax.value_and_grad(_loss, argnums=(0, 1))

    def one_microbatch(ex, loss_target):
        ex = dict(ex)
        diff = ex.pop(TWIN_DIFF_INPUT)
        return grad_fn(weights, diff, {**shared, **ex}, loss_target)

    if N_MICROBATCH == 1:
        loss, (grad_w, grad_x) = one_microbatch(per_example, given["loss_target"])
    else:
        def body(carry, xs):
            loss_sum, grad_sum = carry
            l_k, (gw_k, gx_k) = one_microbatch(xs[0], xs[1])
            with _jax.named_scope("update"):
                return (loss_sum + l_k, _jax.tree.map(_jnp.add, grad_sum, gw_k)), gx_k

        init = (_jnp.zeros((), _jnp.float32), _jax.tree.map(_jnp.zeros_like, weights))
        (loss, grad_w), grad_x = _jax.lax.scan(body, init, (per_example, given["loss_target"]))
    with _jax.named_scope("update"):
        delta_w, new_m, new_v = {}, {}, {}
        for n in TWIN_WEIGHTS:
            delta_w[n], new_m[n], new_v[n] = _adamw(weights[n], grad_w[n], given["m_" + n], given["v_" + n])
    return (loss, grad_x, *[grad_w[n] for n in TWIN_WEIGHTS], *[delta_w[n] for n in TWIN_WEIGHTS],
            *[new_m[n] for n in TWIN_WEIGHTS], *[new_v[n] for n in TWIN_WEIGHTS])
```

```python
import functools

import jax
import jax.numpy as jnp
from jax import lax
from jax.experimental import pallas as pl
from jax.experimental.pallas import tpu as pltpu

F32 = jnp.float32
BF16 = jnp.bfloat16

D = 2048
HEAD_DIM = 64
N_HEADS = 32
N_GROUPS = 4
N_STATE = 128
CHUNK = 256
D_XBC = D + 2 * N_GROUPS * N_STATE
SSD_GROUP_W = D // N_GROUPS
SC_GROUP_W = 128
XA_HEADS = 4
XA_HD = 128
D_XA = XA_HEADS * XA_HD
D_FF = 5632
NORM_EPS = 1e-5
PROJ_W = 11264
DT_W = 128
N_DEV = 8

ADAM_LR = 0.001
ADAM_B1 = 0.9
ADAM_B2 = 0.999
ADAM_EPS = 1e-08
ADAM_WD = 0.01
ADAM_STEP = 10

TB = CHUNK
HALO = 8
LANES = 128
VMEM_LIMIT = 56 * 1024 * 1024

NT = (((1,), (1,)), ((), ()))
TN = (((0,), (0,)), ((), ()))
MESH = pl.DeviceIdType.MESH


def _cp(*sem):
    return pltpu.CompilerParams(dimension_semantics=sem, vmem_limit_bytes=VMEM_LIMIT)


def _sds(shape, dtype):
    return jax.ShapeDtypeStruct(shape, dtype)


def _sigmoid(x):
    return 1.0 / (1.0 + jnp.exp(-x))


def _split3_dot(t_bf16, x):
    hi = x.astype(BF16)
    r1 = x - hi.astype(F32)
    mid = r1.astype(BF16)
    lo = (r1 - mid.astype(F32)).astype(BF16)
    out = jnp.dot(t_bf16, hi, preferred_element_type=F32)
    out = out + jnp.dot(t_bf16, mid, preferred_element_type=F32)
    return out + jnp.dot(t_bf16, lo, preferred_element_type=F32)


def _group_bcast(stat_fn, v, gw):
    pieces = []
    for g in range(v.shape[1] // gw):
        vs = v[:, g * gw:(g + 1) * gw]
        pieces.append(jnp.broadcast_to(stat_fn(vs), vs.shape))
    return jnp.concatenate(pieces, axis=1) if len(pieces) > 1 else pieces[0]


def _group_rstd(v, gw):
    return _group_bcast(lambda s: lax.rsqrt(jnp.mean(s * s, axis=1, keepdims=True) + NORM_EPS), v, gw)


def _group_mean(v, gw):
    return _group_bcast(lambda s: jnp.mean(s, axis=1, keepdims=True), v, gw)


def _pair_sum(d, first):
    s0 = jnp.sum(jnp.where(first, d, 0.0), axis=1, keepdims=True)
    s1 = jnp.sum(jnp.where(first, 0.0, d), axis=1, keepdims=True)
    return jnp.where(first, s0, s1)


def _mm(a, b, *, name, ta=False, tb=False, res=None, out_dtype=F32, bm=1024, bn=1024, bk=None):
    if ta:
        k_dim, m_dim = a.shape
    else:
        m_dim, k_dim = a.shape
    if tb:
        n_dim, kb = b.shape
    else:
        kb, n_dim = b.shape
    assert k_dim == kb, (a.shape, b.shape)
    bm, bn = min(bm, m_dim), min(bn, n_dim)
    bk = k_dim if bk is None else min(bk, k_dim)
    assert m_dim % bm == 0 and n_dim % bn == 0 and k_dim % bk == 0, (name, a.shape, b.shape, bm, bn, bk)
    nk = k_dim // bk
    dn = (((0,) if ta else (1,), (1,) if tb else (0,)), ((), ()))
    has_res = res is not None

    def body(*refs):
        a_ref, b_ref = refs[0], refs[1]
        r_ref = refs[2] if has_res else None
        o_ref = refs[2 + has_res]
        p = lax.dot_general(a_ref[...].astype(BF16), b_ref[...].astype(BF16), dn, preferred_element_type=F32)
        if nk == 1:
            if has_res:
                p = p + r_ref[...]
            o_ref[...] = p.astype(o_ref.dtype)
            return
        acc_ref = refs[3 + has_res]
        k = pl.program_id(2)

        @pl.when(k == 0)
        def _():
            acc_ref[...] = p

        @pl.when(k > 0)
        def _():
            acc_ref[...] += p

        @pl.when(k == nk - 1)
        def _():
            r = acc_ref[...]
            if has_res:
                r = r + r_ref[...]
            o_ref[...] = r.astype(o_ref.dtype)

    a_spec = pl.BlockSpec((bk, bm), lambda i, j, k: (k, i)) if ta else pl.BlockSpec((bm, bk), lambda i, j, k: (i, k))
    b_spec = pl.BlockSpec((bn, bk), lambda i, j, k: (j, k)) if tb else pl.BlockSpec((bk, bn), lambda i, j, k: (k, j))
    o_spec = pl.BlockSpec((bm, bn), lambda i, j, k: (i, j))
    in_specs = [a_spec, b_spec] + ([o_spec] if has_res else [])
    args = (a, b) + ((res,) if has_res else ())
    return pl.pallas_call(
        body, name=name, grid=(m_dim // bm, n_dim // bn, nk), in_specs=in_specs, out_specs=o_spec,
        out_shape=_sds((m_dim, n_dim), out_dtype),
        scratch_shapes=[pltpu.VMEM((bm, bn), F32)] if nk > 1 else [],
        compiler_params=_cp("parallel", "parallel", "arbitrary"),
    )(*args)


def _norm_fwd(h, g, *, name, out_dtype=BF16):
    s, d = h.shape
    tb = min(TB, s)

    def body(h_ref, g_ref, o_ref):
        x = h_ref[...]
        r = lax.rsqrt(jnp.mean(x * x, axis=-1, keepdims=True) + NORM_EPS)
        o_ref[...] = (x * r * g_ref[...]).astype(o_ref.dtype)

    row = pl.BlockSpec((tb, d), lambda i: (i, 0))
    return pl.pallas_call(
        body, name=name, grid=(s // tb,), in_specs=[row, pl.BlockSpec((1, d), lambda i: (0, 0))], out_specs=row,
        out_shape=_sds((s, d), out_dtype), compiler_params=_cp("parallel"),
    )(h, g)


def _norm_bwd(h, g, dhn, dres, *, name):
    s, d = h.shape
    tb = min(TB, s)
    has_res = dres is not None

    def body(*refs):
        h_ref, g_ref, dhn_ref = refs[:3]
        r_ref = refs[3] if has_res else None
        dh_ref, dhb_ref, dg_ref = refs[3 + has_res:]
        x = h_ref[...]
        r = lax.rsqrt(jnp.mean(x * x, axis=-1, keepdims=True) + NORM_EPS)
        xhat = x * r
        dy = dhn_ref[...]
        gy = dy * g_ref[...]
        dx = r * (gy - xhat * jnp.mean(gy * xhat, axis=-1, keepdims=True))
        if has_res:
            dx = dx + r_ref[...]
        dh_ref[...] = dx
        dhb_ref[...] = dx.astype(BF16)
        part = jnp.sum(dy * xhat, axis=0, keepdims=True)

        @pl.when(pl.program_id(0) == 0)
        def _():
            dg_ref[...] = part

        @pl.when(pl.program_id(0) > 0)
        def _():
            dg_ref[...] += part

    row = pl.BlockSpec((tb, d), lambda i: (i, 0))
    vec = pl.BlockSpec((1, d), lambda i: (0, 0))
    return pl.pallas_call(
        body, name=name, grid=(s // tb,), in_specs=[row, vec, row] + ([row] if has_res else []),
        out_specs=(row, row, vec), out_shape=(_sds((s, d), F32), _sds((s, d), BF16), _sds((1, d), F32)),
        compiler_params=_cp("arbitrary"),
    )(*((h, g, dhn) + ((dres,) if has_res else ())))


def _loss_head(h, g, tgt, *, name):
    s, d = h.shape
    tb = min(TB, s)

    def body(h_ref, g_ref, t_ref, loss_ref, dh_ref, dhb_ref, dg_ref):
        x = h_ref[...]
        r = lax.rsqrt(jnp.mean(x * x, axis=-1, keepdims=True) + NORM_EPS)
        xhat = x * r
        gain = g_ref[...]
        err = xhat * gain - t_ref[...]
        part_loss = 0.5 * jnp.sum(jnp.mean(err * err, axis=-1, keepdims=True), axis=0, keepdims=True)
        dy = err * (1.0 / d)
        gy = dy * gain
        dx = r * (gy - xhat * jnp.mean(gy * xhat, axis=-1, keepdims=True))
        dh_ref[...] = dx
        dhb_ref[...] = dx.astype(BF16)
        part = jnp.sum(dy * xhat, axis=0, keepdims=True)
        lossv = jnp.broadcast_to(part_loss, (1, LANES))

        @pl.when(pl.program_id(0) == 0)
        def _():
            dg_ref[...] = part
            loss_ref[...] = lossv

        @pl.when(pl.program_id(0) > 0)
        def _():
            dg_ref[...] += part
            loss_ref[...] += lossv

    row = pl.BlockSpec((tb, d), lambda i: (i, 0))
    vec = pl.BlockSpec((1, d), lambda i: (0, 0))
    return pl.pallas_call(
        body, name=name, grid=(s // tb,), in_specs=[row, vec, row],
        out_specs=(pl.BlockSpec((1, LANES), lambda i: (0, 0)), row, row, vec),
        out_shape=(_sds((1, LANES), F32), _sds((s, d), F32), _sds((s, d), BF16), _sds((1, d), F32)),
        compiler_params=_cp("arbitrary"),
    )(h, g, tgt)


def _swiglu_fwd(gu, *, name):
    s = gu.shape[0]
    tb = min(TB, s)

    def body(g_ref, u_ref, o_ref):
        g = g_ref[...]
        o_ref[...] = (g * _sigmoid(g) * u_ref[...]).astype(o_ref.dtype)

    return pl.pallas_call(
        body, name=name, grid=(s // tb,),
        in_specs=[pl.BlockSpec((tb, D_FF), lambda i: (i, 0)), pl.BlockSpec((tb, D_FF), lambda i: (i, 1))],
        out_specs=pl.BlockSpec((tb, D_FF), lambda i: (i, 0)), out_shape=_sds((s, D_FF), BF16),
        compiler_params=_cp("parallel"),
    )(gu, gu)


def _swiglu_bwd(gu, dact, *, name):
    s = gu.shape[0]
    tb = min(TB, s)

    def body(g_ref, u_ref, d_ref, o_ref):
        g = g_ref[...]
        sg = _sigmoid(g)
        da = d_ref[...]
        o_ref[:, :D_FF] = (da * u_ref[...] * sg * (1.0 + g * (1.0 - sg))).astype(o_ref.dtype)
        o_ref[:, D_FF:] = (da * g * sg).astype(o_ref.dtype)

    return pl.pallas_call(
        body, name=name, grid=(s // tb,),
        in_specs=[pl.BlockSpec((tb, D_FF), lambda i: (i, 0)), pl.BlockSpec((tb, D_FF), lambda i: (i, 1)),
                  pl.BlockSpec((tb, D_FF), lambda i: (i, 0))],
        out_specs=pl.BlockSpec((tb, 2 * D_FF), lambda i: (i, 0)), out_shape=_sds((s, 2 * D_FF), BF16),
        compiler_params=_cp("parallel"),
    )(gu, gu, dact)


def _softmax_rows(qh, kh):
    sc = lax.dot_general(qh, kh, NT, preferred_element_type=F32) * (XA_HD ** -0.5)
    sc = sc - jnp.max(sc, axis=-1, keepdims=True)
    e = jnp.exp(sc)
    return e / jnp.sum(e, axis=-1, keepdims=True)


def _attn_fwd(q, k, v, *, name):
    s = q.shape[0]
    n_mem = k.shape[0]
    tq = min(512, s)

    def body(q_ref, k_ref, v_ref, o_ref):
        outs = []
        for h in range(XA_HEADS):
            sl = slice(h * XA_HD, (h + 1) * XA_HD)
            p = _softmax_rows(q_ref[:, sl], k_ref[:, sl])
            outs.append(jnp.dot(p.astype(BF16), v_ref[:, sl], preferred_element_type=F32))
        o_ref[...] = jnp.concatenate(outs, axis=1).astype(o_ref.dtype)

    row = pl.BlockSpec((tq, D_XA), lambda i: (i, 0))
    kv = pl.BlockSpec((n_mem, D_XA), lambda i: (0, 0))
    return pl.pallas_call(
        body, name=name, grid=(s // tq,), in_specs=[row, kv, kv], out_specs=row, out_shape=_sds((s, D_XA), BF16),
        compiler_params=_cp("parallel"),
    )(q, k, v)


def _attn_bwd(q, k, v, do, *, name):
    s = q.shape[0]
    n_mem = k.shape[0]
    tq = min(512, s)

    def body(q_ref, k_ref, v_ref, do_ref, dq_ref, dk_ref, dv_ref):
        dqs, dks, dvs = [], [], []
        for h in range(XA_HEADS):
            sl = slice(h * XA_HD, (h + 1) * XA_HD)
            qh, kh, vh, doh = q_ref[:, sl], k_ref[:, sl], v_ref[:, sl], do_ref[:, sl]
            p = _softmax_rows(qh, kh)
            dvs.append(lax.dot_general(p.astype(BF16), doh, TN, preferred_element_type=F32))
            dp = lax.dot_general(doh, vh, NT, preferred_element_type=F32)
            ds = (p * (dp - jnp.sum(dp * p, axis=-1, keepdims=True)) * (XA_HD ** -0.5)).astype(BF16)
            dqs.append(jnp.dot(ds, kh, preferred_element_type=F32))
            dks.append(lax.dot_general(ds, qh, TN, preferred_element_type=F32))
        dq_ref[...] = jnp.concatenate(dqs, axis=1).astype(dq_ref.dtype)
        dk = jnp.concatenate(dks, axis=1)
        dv = jnp.concatenate(dvs, axis=1)

        @pl.when(pl.program_id(0) == 0)
        def _():
            dk_ref[...] = dk
            dv_ref[...] = dv

        @pl.when(pl.program_id(0) > 0)
        def _():
            dk_ref[...] += dk
            dv_ref[...] += dv

    row = pl.BlockSpec((tq, D_XA), lambda i: (i, 0))
    kv = pl.BlockSpec((n_mem, D_XA), lambda i: (0, 0))
    return pl.pallas_call(
        body, name=name, grid=(s // tq,), in_specs=[row, kv, kv, row], out_specs=(row, kv, kv),
        out_shape=(_sds((s, D_XA), BF16), _sds((n_mem, D_XA), F32), _sds((n_mem, D_XA), F32)),
        compiler_params=_cp("arbitrary"),
    )(q, k, v, do)


CONV_CB = 1024
XBC_CB0 = D // CONV_CB


def _prev_rows(i, tb):
    return jnp.maximum(i * (tb // HALO) - 1, 0)


def _next_rows(i, tb, s):
    return jnp.minimum((i + 1) * (tb // HALO), s // HALO - 1)


def _conv4(xcat, w_ref, lo):
    acc = w_ref[3:4, :] * xcat[lo:]
    for k in range(3):
        acc = acc + w_ref[k:k + 1, :] * pltpu.roll(xcat, 3 - k, 0)[lo:]
    return acc


def _conv_silu_fwd(proj, w, b, *, name):
    s = proj.shape[0]
    tb = min(TB, s)

    def body(x_ref, xp_ref, w_ref, b_ref, o_ref):
        i = pl.program_id(0)
        prev = jnp.where(i > 0, xp_ref[...], 0.0)
        pre = _conv4(jnp.concatenate([prev, x_ref[...]], axis=0), w_ref, HALO) + b_ref[...]
        o_ref[...] = pre * _sigmoid(pre)

    return pl.pallas_call(
        body, name=name, grid=(s // tb, D_XBC // CONV_CB),
        in_specs=[pl.BlockSpec((tb, CONV_CB), lambda i, j: (i, XBC_CB0 + j)),
                  pl.BlockSpec((HALO, CONV_CB), lambda i, j: (_prev_rows(i, tb), XBC_CB0 + j)),
                  pl.BlockSpec((4, CONV_CB), lambda i, j: (0, j)),
                  pl.BlockSpec((1, CONV_CB), lambda i, j: (0, j))],
        out_specs=pl.BlockSpec((tb, CONV_CB), lambda i, j: (i, j)), out_shape=_sds((s, D_XBC), F32),
        compiler_params=_cp("parallel", "parallel"),
    )(proj, proj, w, b)


def _conv_silu_bwd(proj, dy, w, b, *, name):
    s = proj.shape[0]
    tb = min(TB, s)
    n_i = s // tb

    def body(x_ref, xp_ref, xn_ref, dy_ref, dyn_ref, w_ref, b_ref, dx_ref, dw_ref, db_ref):
        i = pl.program_id(1)
        prev = jnp.where(i > 0, xp_ref[...], 0.0)
        xcat = jnp.concatenate([prev, x_ref[...], xn_ref[...]], axis=0)
        pre = _conv4(xcat, w_ref, HALO) + b_ref[...]
        sg = _sigmoid(pre)
        dy_ext = jnp.concatenate([dy_ref[...], jnp.where(i < n_i - 1, dyn_ref[...], 0.0)], axis=0)
        dpre = dy_ext * sg * (1.0 + pre * (1.0 - sg))
        n2 = tb + HALO
        dx = w_ref[3:4, :] * dpre[:tb]
        for k in range(3):
            dx = dx + w_ref[k:k + 1, :] * pltpu.roll(dpre, n2 - (3 - k), 0)[:tb]
        dx_ref[...] = dx.astype(dx_ref.dtype)
        dpc = dpre[:tb]
        parts = [jnp.sum(dpc * pltpu.roll(xcat, 3 - k, 0)[HALO:HALO + tb], axis=0, keepdims=True) for k in range(3)]
        parts.append(jnp.sum(dpc * x_ref[...], axis=0, keepdims=True))
        dbp = jnp.sum(dpc, axis=0, keepdims=True)

        @pl.when(i == 0)
        def _():
            for k in range(4):
                dw_ref[k] = parts[k]
            db_ref[...] = dbp

        @pl.when(i > 0)
        def _():
            for k in range(4):
                dw_ref[k] += parts[k]
            db_ref[...] += dbp

    return pl.pallas_call(
        body, name=name, grid=(D_XBC // CONV_CB, n_i),
        in_specs=[pl.BlockSpec((tb, CONV_CB), lambda j, i: (i, XBC_CB0 + j)),
                  pl.BlockSpec((HALO, CONV_CB), lambda j, i: (_prev_rows(i, tb), XBC_CB0 + j)),
                  pl.BlockSpec((HALO, CONV_CB), lambda j, i: (_next_rows(i, tb, s), XBC_CB0 + j)),
                  pl.BlockSpec((tb, CONV_CB), lambda j, i: (i, j)),
                  pl.BlockSpec((HALO, CONV_CB), lambda j, i: (_next_rows(i, tb, s), j)),
                  pl.BlockSpec((4, CONV_CB), lambda j, i: (0, j)),
                  pl.BlockSpec((1, CONV_CB), lambda j, i: (0, j))],
        out_specs=(pl.BlockSpec((tb, CONV_CB), lambda j, i: (i, j)),
                   pl.BlockSpec((4, 1, CONV_CB), lambda j, i: (0, 0, j)),
                   pl.BlockSpec((1, CONV_CB), lambda j, i: (0, j))),
        out_shape=(_sds((s, D_XBC), BF16), _sds((4, 1, D_XBC), F32), _sds((1, D_XBC), F32)),
        compiler_params=_cp("parallel", "arbitrary"),
    )(proj, proj, proj, dy, dy, w, b)


SC_U0 = (D + D_XBC) // CONV_CB
SC_B0 = SC_U0 + D // CONV_CB
SC_C0 = SC_B0 + D // CONV_CB


def _conv3(cat, w_ref, lo):
    return (w_ref[2:3, :] * cat[lo:] + w_ref[1:2, :] * pltpu.roll(cat, 1, 0)[lo:]
            + w_ref[0:1, :] * pltpu.roll(cat, 2, 0)[lo:])


def _sc_fwd(proj, w, gain, *, name):
    s = proj.shape[0]
    tb = min(TB, s)

    def body(u_ref, b_ref, c_ref, up_ref, cp_ref, w_ref, g_ref, o_ref):
        i = pl.program_id(0)
        cup = jnp.where(i > 0, cp_ref[...] * up_ref[...], 0.0)
        cat = jnp.concatenate([cup, c_ref[...] * u_ref[...]], axis=0)
        v = b_ref[...] * _conv3(cat, w_ref, HALO)
        o_ref[...] = (v * _group_rstd(v, SC_GROUP_W) * g_ref[...]).astype(o_ref.dtype)

    def cur(c0):
        return pl.BlockSpec((tb, CONV_CB), lambda i, j: (i, c0 + j))

    def prev(c0):
        return pl.BlockSpec((HALO, CONV_CB), lambda i, j: (_prev_rows(i, tb), c0 + j))

    return pl.pallas_call(
        body, name=name, grid=(s // tb, D // CONV_CB),
        in_specs=[cur(SC_U0), cur(SC_B0), cur(SC_C0), prev(SC_U0), prev(SC_C0),
                  pl.BlockSpec((3, CONV_CB), lambda i, j: (0, j)), pl.BlockSpec((1, CONV_CB), lambda i, j: (0, j))],
        out_specs=pl.BlockSpec((tb, CONV_CB), lambda i, j: (i, j)), out_shape=_sds((s, D), BF16),
        compiler_params=_cp("parallel", "parallel"),
    )(proj, proj, proj, proj, proj, w, gain)


def _sc_bwd(proj, dcat, w, gain, *, name):
    s = proj.shape[0]
    tb = min(TB, s)
    n_i = s // tb
    dy0 = D // CONV_CB

    def body(u_ref, b_ref, c_ref, up_ref, cp_ref, un_ref, bn_ref, cn_ref, dy_ref, dyn_ref, w_ref, g_ref,
             du_ref, db_ref, dc_ref, dw_ref, dg_ref):
        i = pl.program_id(1)
        u, c = u_ref[...], c_ref[...]
        cup = jnp.where(i > 0, cp_ref[...] * up_ref[...], 0.0)
        cu_ext = jnp.concatenate([cup, c * u, cn_ref[...] * un_ref[...]], axis=0)
        conv = _conv3(cu_ext, w_ref, HALO)
        b_ext = jnp.concatenate([b_ref[...], bn_ref[...]], axis=0)
        dy_ext = jnp.concatenate([dy_ref[...], jnp.where(i < n_i - 1, dyn_ref[...], 0.0)], axis=0)
        v = b_ext * conv
        r = _group_rstd(v, SC_GROUP_W)
        vhat = v * r
        dvhat = dy_ext * g_ref[...]
        dv = r * (dvhat - vhat * _group_mean(dvhat * vhat, SC_GROUP_W))
        dconv = dv * b_ext
        n2 = tb + HALO
        dcu = (w_ref[2:3, :] * dconv[:tb] + w_ref[1:2, :] * pltpu.roll(dconv, n2 - 1, 0)[:tb]
               + w_ref[0:1, :] * pltpu.roll(dconv, n2 - 2, 0)[:tb])
        du_ref[...] = (dcu * c).astype(du_ref.dtype)
        dc_ref[...] = (dcu * u).astype(dc_ref.dtype)
        db_ref[...] = (dv * conv)[:tb].astype(db_ref.dtype)
        dcc = dconv[:tb]
        parts = [jnp.sum(dcc * pltpu.roll(cu_ext, 2 - k, 0)[HALO:HALO + tb], axis=0, keepdims=True) for k in range(2)]
        parts.append(jnp.sum(dcc * c * u, axis=0, keepdims=True))
        dgp = jnp.sum((dy_ext * vhat)[:tb], axis=0, keepdims=True)

        @pl.when(i == 0)
        def _():
            for k in range(3):
                dw_ref[k] = parts[k]
            dg_ref[...] = dgp

        @pl.when(i > 0)
        def _():
            for k in range(3):
                dw_ref[k] += parts[k]
            dg_ref[...] += dgp

    def cur(c0):
        return pl.BlockSpec((tb, CONV_CB), lambda j, i: (i, c0 + j))

    def prev(c0):
        return pl.BlockSpec((HALO, CONV_CB), lambda j, i: (_prev_rows(i, tb), c0 + j))

    def nxt(c0):
        return pl.BlockSpec((HALO, CONV_CB), lambda j, i: (_next_rows(i, tb, s), c0 + j))

    vec = pl.BlockSpec((1, CONV_CB), lambda j, i: (0, j))
    out_row = pl.BlockSpec((tb, CONV_CB), lambda j, i: (i, j))
    return pl.pallas_call(
        body, name=name, grid=(D // CONV_CB, n_i),
        in_specs=[cur(SC_U0), cur(SC_B0), cur(SC_C0), prev(SC_U0), prev(SC_C0), nxt(SC_U0), nxt(SC_B0), nxt(SC_C0),
                  cur(dy0), nxt(dy0), pl.BlockSpec((3, CONV_CB), lambda j, i: (0, j)), vec],
        out_specs=(out_row, out_row, out_row, pl.BlockSpec((3, 1, CONV_CB), lambda j, i: (0, 0, j)), vec),
        out_shape=(_sds((s, D), BF16), _sds((s, D), BF16), _sds((s, D), BF16), _sds((3, 1, D), F32), _sds((1, D), F32)),
        compiler_params=_cp("parallel", "arbitrary"),
    )(proj, proj, proj, proj, proj, proj, proj, proj, dcat, dcat, w, gain)


def _gated_norm_fwd(y, proj, gain, *, name):
    s = y.shape[0]
    tb = min(TB, s)

    def body(y_ref, z_ref, g_ref, o_ref):
        z = z_ref[...]
        t = y_ref[...] * z * _sigmoid(z)
        o_ref[...] = (t * _group_rstd(t, SSD_GROUP_W) * g_ref[...]).astype(o_ref.dtype)

    row = pl.BlockSpec((tb, D), lambda i: (i, 0))
    return pl.pallas_call(
        body, name=name, grid=(s // tb,), in_specs=[row, row, pl.BlockSpec((1, D), lambda i: (0, 0))], out_specs=row,
        out_shape=_sds((s, D), BF16), compiler_params=_cp("parallel"),
    )(y, proj, gain)


def _gated_norm_bwd(y, proj, dcat, gain, *, name):
    s = y.shape[0]
    tb = min(TB, s)

    def body(y_ref, z_ref, d_ref, g_ref, dy_ref, dz_ref, dg_ref):
        z, yv, dout = z_ref[...], y_ref[...], d_ref[...]
        sg = _sigmoid(z)
        sz = z * sg
        t = yv * sz
        r = _group_rstd(t, SSD_GROUP_W)
        that = t * r
        dthat = dout * g_ref[...]
        dt = r * (dthat - that * _group_mean(dthat * that, SSD_GROUP_W))
        dy_ref[...] = dt * sz
        dz_ref[...] = (dt * yv * sg * (1.0 + z * (1.0 - sg))).astype(dz_ref.dtype)
        part = jnp.sum(dout * that, axis=0, keepdims=True)

        @pl.when(pl.program_id(0) == 0)
        def _():
            dg_ref[...] = part

        @pl.when(pl.program_id(0) > 0)
        def _():
            dg_ref[...] += part

    row = pl.BlockSpec((tb, D), lambda i: (i, 0))
    vec = pl.BlockSpec((1, D), lambda i: (0, 0))
    return pl.pallas_call(
        body, name=name, grid=(s // tb,), in_specs=[row, row, row, vec], out_specs=(row, row, vec),
        out_shape=(_sds((s, D), F32), _sds((s, D), BF16), _sds((1, D), F32)), compiler_params=_cp("arbitrary"),
    )(y, proj, dcat, gain)


def _tri(lower):
    row = lax.broadcasted_iota(jnp.int32, (CHUNK, CHUNK), 0)
    col = lax.broadcasted_iota(jnp.int32, (CHUNK, CHUNK), 1)
    return jnp.where(row >= col if lower else col >= row, 1.0, 0.0).astype(BF16)


def _dt_fwd(dt_raw, bias, a_log, *, name):
    s = dt_raw.shape[0]
    nc = s // CHUNK

    def body(raw_ref, bias_ref, alog_ref, dt_ref, cum_ref, cumt_ref):
        x = raw_ref[...] + bias_ref[...]
        dt = jnp.maximum(x, 0.0) + jnp.log1p(jnp.exp(-jnp.abs(x)))
        cum = _split3_dot(_tri(True), dt * (-jnp.exp(alog_ref[...])))
        dt_ref[...] = dt
        cum_ref[...] = cum
        cumt_ref[0] = cum.T

    row = pl.BlockSpec((CHUNK, DT_W), lambda i: (i, 0))
    vec = pl.BlockSpec((1, DT_W), lambda i: (0, 0))
    return pl.pallas_call(
        body, name=name, grid=(nc,), in_specs=[row, vec, vec],
        out_specs=(row, row, pl.BlockSpec((1, DT_W, CHUNK), lambda i: (i, 0, 0))),
        out_shape=(_sds((s, DT_W), F32), _sds((s, DT_W), F32), _sds((nc, DT_W, CHUNK), F32)),
        compiler_params=_cp("parallel"),
    )(dt_raw, bias, a_log)


def _dt_bwd(dt_raw, bias, ddt, *, name):
    s = dt_raw.shape[0]
    tb = min(TB, s)

    def body(raw_ref, bias_ref, d_ref, o_ref, db_ref):
        g = d_ref[...] * _sigmoid(raw_ref[...] + bias_ref[...])
        o_ref[...] = g.astype(o_ref.dtype)
        part = jnp.sum(g, axis=0, keepdims=True)

        @pl.when(pl.program_id(0) == 0)
        def _():
            db_ref[...] = part

        @pl.when(pl.program_id(0) > 0)
        def _():
            db_ref[...] += part

    row = pl.BlockSpec((tb, DT_W), lambda i: (i, 0))
    vec = pl.BlockSpec((1, DT_W), lambda i: (0, 0))
    return pl.pallas_call(
        body, name=name, grid=(s // tb,), in_specs=[row, vec, row], out_specs=(row, vec),
        out_shape=(_sds((s, DT_W), BF16), _sds((1, DT_W), F32)), compiler_params=_cp("arbitrary"),
    )(dt_raw, bias, ddt)


N_PAIRS = N_HEADS // 2
PAIRS_PER_GROUP = N_PAIRS // N_GROUPS
B_CB0 = D // LANES
C_CB0 = B_CB0 + N_GROUPS


def _decay(cum_col, cum_row, causal):
    return jnp.where(causal, jnp.exp(jnp.minimum(cum_col - cum_row, 0.0)), 0.0)


def _causal_mask():
    row = lax.broadcasted_iota(jnp.int32, (CHUNK, CHUNK), 0)
    col = lax.broadcasted_iota(jnp.int32, (CHUNK, CHUNK), 1)
    return row >= col


def _state_row_scale(cumt_ref):
    last0 = cumt_ref[0, 0][:, CHUNK - 1:CHUNK]
    last1 = cumt_ref[0, 1][:, CHUNK - 1:CHUNK]
    rown = lax.broadcasted_iota(jnp.int32, (LANES, 1), 0)
    return jnp.exp(jnp.where(rown < HEAD_DIM, last0, last1))


def _ssd_fwd(xbc_c, dt_b, cum_b, cumt4, dskip_b, *, name):
    s = xbc_c.shape[0]
    nc = s // CHUNK

    def body(xs_ref, b_ref, c_ref, dt_ref, cum_ref, cumt_ref, dsk_ref, y_ref, hp_ref, state, cb):
        c, j = pl.program_id(0), pl.program_id(1)

        @pl.when(c == 0)
        def _():
            state[j] = jnp.zeros((LANES, N_STATE), F32)

        bb, cbm = b_ref[...].astype(BF16), c_ref[...].astype(BF16)

        @pl.when(j % PAIRS_PER_GROUP == 0)
        def _():
            cb[...] = lax.dot_general(cbm, bb, NT, preferred_element_type=F32)

        xs, cum = xs_ref[...], cum_ref[...]
        xt = xs * dt_ref[...]
        xtb = xt.astype(BF16)
        hp = state[j]
        hp_ref[0, 0] = hp
        first = lax.broadcasted_iota(jnp.int32, (CHUNK, LANES), 1) < HEAD_DIM
        causal = _causal_mask()
        cbv = cb[...]
        y = jnp.exp(cum) * lax.dot_general(cbm, hp.astype(BF16), NT, preferred_element_type=F32)
        for hh in range(2):
            lm = _decay(cum[:, hh * HEAD_DIM:hh * HEAD_DIM + 1], cumt_ref[0, hh], causal)
            xm = jnp.where(first if hh == 0 else jnp.logical_not(first), xtb, jnp.zeros_like(xtb))
            y = y + jnp.dot((cbv * lm).astype(BF16), xm, preferred_element_type=F32)
        decs = jnp.exp(cum[CHUNK - 1:CHUNK, :] - cum)
        st = lax.dot_general((xt * decs).astype(BF16), bb, TN, preferred_element_type=F32)
        state[j] = _state_row_scale(cumt_ref) * hp + st
        y_ref[...] = y + xs * dsk_ref[...]

    def pair(c0=0):
        return pl.BlockSpec((CHUNK, LANES), lambda c, j: (c, c0 + j))

    def group(c0):
        return pl.BlockSpec((CHUNK, LANES), lambda c, j: (c, c0 + j // PAIRS_PER_GROUP))

    return pl.pallas_call(
        body, name=name, grid=(nc, N_PAIRS),
        in_specs=[pair(), group(B_CB0), group(C_CB0), pair(), pair(),
                  pl.BlockSpec((1, 2, 1, CHUNK), lambda c, j: (c, j, 0, 0)),
                  pl.BlockSpec((1, LANES), lambda c, j: (0, j))],
        out_specs=(pair(), pl.BlockSpec((1, 1, LANES, N_STATE), lambda c, j: (c, j, 0, 0))),
        out_shape=(_sds((s, D), F32), _sds((nc, N_PAIRS, LANES, N_STATE), F32)),
        scratch_shapes=[pltpu.VMEM((N_PAIRS, LANES, N_STATE), F32), pltpu.VMEM((CHUNK, CHUNK), F32)],
        compiler_params=_cp("arbitrary", "arbitrary"),
    )(xbc_c, xbc_c, xbc_c, dt_b, cum_b, cumt4, dskip_b)


def _ssd_bwd(xbc_c, dt_b, cum_b, cumt4, hprev, dy, alog_b, dskip_b, *, name):
    s = xbc_c.shape[0]
    nc = s // CHUNK

    def body(xs_ref, b_ref, c_ref, dt_ref, cum_ref, cumt_ref, hp_ref, dy_ref, alog_ref, dsk_ref,
             dxs_ref, db_ref, dc_ref, ddt_ref, dalog_ref, ddsk_ref, dstate, cb):
        c, j = pl.program_id(0), pl.program_id(1)

        @pl.when(c == 0)
        def _():
            dstate[j] = jnp.zeros((LANES, N_STATE), F32)
            dalog_ref[j] = jnp.zeros((1, LANES), F32)
            ddsk_ref[j] = jnp.zeros((1, LANES), F32)

        bb, cbm = b_ref[...].astype(BF16), c_ref[...].astype(BF16)

        @pl.when(j % PAIRS_PER_GROUP == 0)
        def _():
            cb[...] = lax.dot_general(cbm, bb, NT, preferred_element_type=F32)

        xs, dtb, cum, dy = xs_ref[...], dt_ref[...], cum_ref[...], dy_ref[...]
        a_b = -jnp.exp(alog_ref[...])
        xt = xs * dtb
        xtb = xt.astype(BF16)
        dyb = dy.astype(BF16)
        hp = hp_ref[0, 0]
        hpb = hp.astype(BF16)
        dh = dstate[j]
        dhb = dh.astype(BF16)
        exp_cum = jnp.exp(cum)
        decs = jnp.exp(cum[CHUNK - 1:CHUNK, :] - cum)
        row_scale = _state_row_scale(cumt_ref)
        xd = (xt * decs).astype(BF16)
        dye = (dy * exp_cum).astype(BF16)
        first = lax.broadcasted_iota(jnp.int32, (CHUNK, LANES), 1) < HEAD_DIM
        causal = _causal_mask()
        cbv = cb[...]
        zero_b = jnp.zeros_like(xtb)

        h_next = row_scale * hp + lax.dot_general(xd, bb, TN, preferred_element_type=F32)
        y = exp_cum * lax.dot_general(cbm, hpb, NT, preferred_element_type=F32)
        dxt = decs * lax.dot_general(bb, dhb, NT, preferred_element_type=F32)
        dcb = jnp.zeros((CHUNK, CHUNK), F32)
        for hh in range(2):
            mask = first if hh == 0 else jnp.logical_not(first)
            lm = _decay(cum[:, hh * HEAD_DIM:hh * HEAD_DIM + 1], cumt_ref[0, hh], causal)
            m = (cbv * lm).astype(BF16)
            y = y + jnp.dot(m, jnp.where(mask, xtb, zero_b), preferred_element_type=F32)
            dxt = dxt + jnp.where(mask, lax.dot_general(m, dyb, TN, preferred_element_type=F32), 0.0)
            dm = lax.dot_general(jnp.where(mask, dyb, zero_b), xtb, NT, preferred_element_type=F32)
            dcb = dcb + dm * lm
        dcbb = dcb.astype(BF16)
        d_c = jnp.dot(dcbb, bb, preferred_element_type=F32) + jnp.dot(dye, hpb, preferred_element_type=F32)
        d_b = (lax.dot_general(dcbb, cbm, TN, preferred_element_type=F32)
               + jnp.dot(xd, dhb, preferred_element_type=F32))
        dstate[j] = row_scale * dh + lax.dot_general(dye, cbm, TN, preferred_element_type=F32)

        d_cum = _pair_sum(dyb.astype(F32) * y - dxt * xtb.astype(F32), first)
        e = jnp.sum(dh * h_next, axis=1, keepdims=True)
        rown = lax.broadcasted_iota(jnp.int32, (LANES, 1), 0)
        t0 = jnp.sum(jnp.where(rown < HEAD_DIM, e, 0.0), axis=0, keepdims=True)
        t1 = jnp.sum(jnp.where(rown < HEAD_DIM, 0.0, e), axis=0, keepdims=True)
        d_da = _split3_dot(_tri(False), d_cum) + jnp.where(first[0:1, :], t0, t1)
        ddt_ref[...] = a_b * d_da + _pair_sum(dxt * xs, first)
        dxs_ref[...] = dxt * dtb + dy * dsk_ref[...]
        dalog_ref[j] += jnp.sum(d_da * dtb * a_b, axis=0, keepdims=True)
        ddsk_ref[j] += jnp.sum(_pair_sum(dy * xs, first), axis=0, keepdims=True)

        @pl.when(j % PAIRS_PER_GROUP == 0)
        def _():
            db_ref[...] = d_b
            dc_ref[...] = d_c

        @pl.when(j % PAIRS_PER_GROUP != 0)
        def _():
            db_ref[...] += d_b
            dc_ref[...] += d_c

    def rev(c):
        return nc - 1 - c

    def pair(c0=0):
        return pl.BlockSpec((CHUNK, LANES), lambda c, j: (rev(c), c0 + j))

    def group(c0):
        return pl.BlockSpec((CHUNK, LANES), lambda c, j: (rev(c), c0 + j // PAIRS_PER_GROUP))

    vec = pl.BlockSpec((1, LANES), lambda c, j: (0, j))
    acc = pl.BlockSpec((N_PAIRS, 1, LANES), lambda c, j: (0, 0, 0))
    return pl.pallas_call(
        body, name=name, grid=(nc, N_PAIRS),
        in_specs=[pair(), group(B_CB0), group(C_CB0), pair(), pair(),
                  pl.BlockSpec((1, 2, 1, CHUNK), lambda c, j: (rev(c), j, 0, 0)),
                  pl.BlockSpec((1, 1, LANES, N_STATE), lambda c, j: (rev(c), j, 0, 0)),
                  pair(), vec, vec],
        out_specs=(pair(), group(0), group(0), pair(), acc, acc),
        out_shape=(_sds((s, D), F32), _sds((s, N_GROUPS * N_STATE), F32), _sds((s, N_GROUPS * N_STATE), F32),
                   _sds((s, D), F32), _sds((N_PAIRS, 1, LANES), F32), _sds((N_PAIRS, 1, LANES), F32)),
        scratch_shapes=[pltpu.VMEM((N_PAIRS, LANES, N_STATE), F32), pltpu.VMEM((CHUNK, CHUNK), F32)],
        compiler_params=_cp("arbitrary", "arbitrary"),
    )(xbc_c, xbc_c, xbc_c, dt_b, cum_b, cumt4, hprev, dy, alog_b, dskip_b)


def _lane_bcast(v):
    return jnp.repeat(v, HEAD_DIM, axis=1)


def _prep_layer(w_in, conv_w, conv_b, dt_bias, a_log, d_skip, w_out, w_q, w_k, w_v, w_o, w_gate, w_up, w_down):
    dt0 = D + D_XBC
    pad = DT_W - N_HEADS
    return dict(
        w_main=jnp.concatenate([w_in[:, :dt0], w_in[:, dt0 + N_HEADS:]], axis=1),
        w_dt=jnp.pad(w_in[:, dt0:dt0 + N_HEADS], ((0, 0), (0, pad))),
        conv_w=conv_w, conv_b=conv_b.reshape(1, D_XBC),
        dt_bias=jnp.pad(dt_bias.reshape(1, N_HEADS), ((0, 0), (0, pad))),
        a_log=jnp.pad(a_log.reshape(1, N_HEADS), ((0, 0), (0, pad))),
        alog_b=_lane_bcast(a_log.reshape(1, N_HEADS)), dskip_b=_lane_bcast(d_skip.reshape(1, N_HEADS)),
        w_out=w_out, w_q=w_q, w_k=w_k, w_v=w_v, w_o=w_o,
        w_gu=jnp.concatenate([w_gate, w_up], axis=1), w_down=w_down)


def _layer_fwd(h0, memn, p, g, li):
    s = h0.shape[0]
    nc = s // CHUNK
    n = f"l{li}_"
    hn = _norm_fwd(h0, g["norm_mix"], name=n + "norm_mix")
    proj = _mm(hn, p["w_main"], name=n + "proj")
    dt_raw = _mm(hn, p["w_dt"], name=n + "proj_dt")
    xbc_c = _conv_silu_fwd(proj, p["conv_w"], p["conv_b"], name=n + "conv")
    dt, cum, cumt = _dt_fwd(dt_raw, p["dt_bias"], p["a_log"], name=n + "dt")
    dt_b = _lane_bcast(dt[:, :N_HEADS])
    cum_b = _lane_bcast(cum[:, :N_HEADS])
    cumt4 = cumt[:, :N_HEADS, :].reshape(nc, N_HEADS, 1, CHUNK)
    y, hprev = _ssd_fwd(xbc_c, dt_b, cum_b, cumt4, p["dskip_b"], name=n + "ssd")
    y_ssd = _gated_norm_fwd(y, proj, g["ssd_norm"], name=n + "gnorm")
    y_sc = _sc_fwd(proj, p["sc_conv_w"], g["sc_norm"], name=n + "sc")
    cat = jnp.concatenate([y_ssd, y_sc], axis=1)
    h1 = _mm(cat, p["w_out"], res=h0, bk=2048, name=n + "out")
    hx = _norm_fwd(h1, g["norm_xa"], name=n + "norm_xa")
    q = _mm(hx, p["w_q"], out_dtype=BF16, name=n + "q")
    k = _mm(memn, p["w_k"], out_dtype=BF16, name=n + "k")
    v = _mm(memn, p["w_v"], out_dtype=BF16, name=n + "v")
    o = _attn_fwd(q, k, v, name=n + "attn")
    h2 = _mm(o, p["w_o"], res=h1, name=n + "o")
    hf = _norm_fwd(h2, g["norm_ffn"], name=n + "norm_ffn")
    gu = _mm(hf, p["w_gu"], name=n + "gu")
    act = _swiglu_fwd(gu, name=n + "swiglu")
    h3 = _mm(act, p["w_down"], res=h2, bk=1408, name=n + "down")
    saved = dict(h0=h0, hn=hn, proj=proj, dt_raw=dt_raw, xbc_c=xbc_c, dt_b=dt_b, cum_b=cum_b, cumt4=cumt4, hprev=hprev,
                 y=y, cat=cat, h1=h1, hx=hx, q=q, k=k, v=v, o=o, h2=h2, hf=hf, gu=gu, act=act)
    return h3, saved


def _layer_bwd(dh, dhb, dmemn, memn, p, g, sv, li):
    n = f"l{li}b_"
    gr = {}
    dact = _mm(dhb, p["w_down"], tb=True, bn=512, name=n + "dact")
    gr["w_down"] = _mm(sv["act"], dhb, ta=True, bm=512, bk=1024, out_dtype=BF16, name=n + "dw_down")
    dgu = _swiglu_bwd(sv["gu"], dact, name=n + "swiglu")
    gr["w_gu"] = _mm(sv["hf"], dgu, ta=True, bk=1024, out_dtype=BF16, name=n + "dw_gu")
    dhf = _mm(dgu, p["w_gu"], tb=True, bk=1024, name=n + "dhf")
    dh, dhb, gr["norm_ffn"] = _norm_bwd(sv["h2"], g["norm_ffn"], dhf, dh, name=n + "norm_ffn")
    do = _mm(dhb, p["w_o"], tb=True, out_dtype=BF16, name=n + "do")
    gr["w_o"] = _mm(sv["o"], dhb, ta=True, bk=1024, out_dtype=BF16, name=n + "dw_o")
    dq, dk, dv = _attn_bwd(sv["q"], sv["k"], sv["v"], do, name=n + "attn")
    gr["w_q"] = _mm(sv["hx"], dq, ta=True, bk=1024, out_dtype=BF16, name=n + "dw_q")
    gr["w_k"] = _mm(memn, dk, ta=True, out_dtype=BF16, name=n + "dw_k")
    gr["w_v"] = _mm(memn, dv, ta=True, out_dtype=BF16, name=n + "dw_v")
    dmemn = _mm(dk, p["w_k"], tb=True, res=dmemn, name=n + "dmem_k")
    dmemn = _mm(dv, p["w_v"], tb=True, res=dmemn, name=n + "dmem_v")
    dhx = _mm(dq, p["w_q"], tb=True, name=n + "dhx")
    dh, dhb, gr["norm_xa"] = _norm_bwd(sv["h1"], g["norm_xa"], dhx, dh, name=n + "norm_xa")
    dcat = _mm(dhb, p["w_out"], tb=True, name=n + "dcat")
    gr["w_out"] = _mm(sv["cat"], dhb, ta=True, bk=1024, out_dtype=BF16, name=n + "dw_out")
    du, dgb, dgc, dsc_w, gr["sc_norm"] = _sc_bwd(sv["proj"], dcat, p["sc_conv_w"], g["sc_norm"], name=n + "sc")
    gr["sc_conv_w"] = dsc_w.reshape(3, D)
    dy, dz, gr["ssd_norm"] = _gated_norm_bwd(sv["y"], sv["proj"], dcat, g["ssd_norm"], name=n + "gnorm")
    dxs, d_b, d_c, ddt_b, dalog, ddsk = _ssd_bwd(sv["xbc_c"], sv["dt_b"], sv["cum_b"], sv["cumt4"], sv["hprev"], dy,
                                                 p["alog_b"], p["dskip_b"], name=n + "ssd")
    gr["a_log"] = dalog.reshape(N_HEADS, HEAD_DIM)[:, 0]
    gr["d_skip"] = ddsk.reshape(N_HEADS, HEAD_DIM)[:, 0]
    dxbc_c = jnp.concatenate([dxs, d_b, d_c], axis=1)
    dxbc, dconv_w, dconv_b = _conv_silu_bwd(sv["proj"], dxbc_c, p["conv_w"], p["conv_b"], name=n + "conv")
    gr["ssd_conv_w"] = dconv_w.reshape(4, D_XBC)
    gr["ssd_conv_b"] = dconv_b.reshape(D_XBC)
    ddt = jnp.pad(ddt_b[:, ::HEAD_DIM], ((0, 0), (0, DT_W - N_HEADS)))
    ddt_raw, ddt_bias = _dt_bwd(sv["dt_raw"], p["dt_bias"], ddt, name=n + "dt")
    gr["dt_bias"] = ddt_bias[0, :N_HEADS]
    dproj = jnp.concatenate([dz, dxbc, du, dgb, dgc], axis=1)
    gr["w_main"] = _mm(sv["hn"], dproj, ta=True, bk=1024, out_dtype=BF16, name=n + "dw_main")
    gr["w_dt"] = _mm(sv["hn"], ddt_raw, ta=True, bk=1024, out_dtype=BF16, name=n + "dw_dt")
    dhn = _mm(dproj, p["w_main"], tb=True, bk=1024, name=n + "dhn")
    dhn = _mm(ddt_raw, p["w_dt"], tb=True, res=dhn, name=n + "dhn_dt")
    dh, dhb, gr["norm_mix"] = _norm_bwd(sv["h0"], g["norm_mix"], dhn, dh, name=n + "norm_mix")
    return dh, dhb, dmemn, gr


def _local_step(x, mem, tgt, layers, gains, mem_norm, norm_final):
    depth = len(layers)
    memn_f = _norm_fwd(mem, mem_norm, out_dtype=F32, name="mem_norm")
    memn = memn_f.astype(BF16)
    h = x
    saved = []
    for li in range(depth):
        h, sv = _layer_fwd(h, memn, layers[li], gains[li], li)
        saved.append(sv)
    loss, dh, dhb, d_final = _loss_head(h, norm_final, tgt, name="loss_head")
    dmemn = jnp.zeros(mem.shape, F32)
    grads = [None] * depth
    for li in reversed(range(depth)):
        dh, dhb, dmemn, grads[li] = _layer_bwd(dh, dhb, dmemn, memn, layers[li], gains[li], saved[li], li)
    _, _, d_mem_norm = _norm_bwd(mem, mem_norm, dmemn, None, name="mem_norm_b")
    return loss, dh, grads, d_mem_norm, d_final


def _flip(k, x, y, c):
    return (1 - x if k & 4 else x, 1 - y if k & 2 else y, 1 - c if k & 1 else c)


def _all_gather(shard, *, name):
    l, r, c_dim = shard.shape

    def body(x_ref, out_ref, send_sems, recv_sems, local_sem):
        x, y, c = lax.axis_index("x"), lax.axis_index("y"), lax.axis_index("c")
        me, sibling = (x, y, c), (x, y, 1 - c)
        chips = [(1 - x, y), (x, 1 - y), (1 - x, 1 - y)]

        def rows(px, py, pc):
            return out_ref.at[:, 4 * px + 2 * py + pc]

        def copy(k, block, to, src=None):
            return pltpu.make_async_remote_copy(
                src_ref=rows(*block) if src is None else src, dst_ref=rows(*block),
                send_sem=send_sems.at[k], recv_sem=recv_sems.at[k], device_id=to, device_id_type=MESH)

        mine = pltpu.make_async_copy(x_ref, rows(*me), local_sem)
        mine.start()
        first = [copy(0, me, sibling, src=x_ref)]
        first += [copy(1 + j, me, (*chip, c), src=x_ref) for j, chip in enumerate(chips)]
        for cp in first:
            cp.start()
        passed = [copy(4 + j, (*chip, c), sibling) for j, chip in enumerate(chips)]
        for j, chip in enumerate(chips):
            copy(1 + j, (*chip, c), me).wait_recv()
            passed[j].start()
        copy(0, sibling, me).wait_recv()
        for j, chip in enumerate(chips):
            copy(4 + j, (*chip, 1 - c), me).wait_recv()
        for cp in first + passed:
            cp.wait_send()
        mine.wait()

    return pl.pallas_call(
        body, name=name, out_shape=_sds((l, N_DEV, r, c_dim), shard.dtype),
        in_specs=[pl.BlockSpec(memory_space=pl.ANY)], out_specs=pl.BlockSpec(memory_space=pl.ANY),
        scratch_shapes=[pltpu.SemaphoreType.DMA((7,)), pltpu.SemaphoreType.DMA((7,)), pltpu.SemaphoreType.DMA(())],
    )(shard)


def _exchange(pieces, *, name):
    def body(g_ref, out_ref, send_sems, recv_sems, local_sem):
        x, y, c = lax.axis_index("x"), lax.axis_index("y"), lax.axis_index("c")
        me = 4 * x + 2 * y + c
        mine = pltpu.make_async_copy(g_ref.at[me], out_ref.at[me], local_sem)
        mine.start()
        copies = []
        for k in range(1, N_DEV):
            px, py, pc = _flip(k, x, y, c)
            copies.append(pltpu.make_async_remote_copy(
                src_ref=g_ref.at[4 * px + 2 * py + pc], dst_ref=out_ref.at[me],
                send_sem=send_sems.at[k - 1], recv_sem=recv_sems.at[k - 1], device_id=(px, py, pc), device_id_type=MESH))
        for cp in copies:
            cp.start()
        for cp in copies:
            cp.wait_recv()
        for cp in copies:
            cp.wait_send()
        mine.wait()

    return pl.pallas_call(
        body, name=name, out_shape=_sds(pieces.shape, pieces.dtype),
        in_specs=[pl.BlockSpec(memory_space=pl.ANY)], out_specs=pl.BlockSpec(memory_space=pl.ANY),
        scratch_shapes=[pltpu.SemaphoreType.DMA((7,)), pltpu.SemaphoreType.DMA((7,)), pltpu.SemaphoreType.DMA(())],
    )(pieces)


def _adamw(parts, w, m, v, *, name):
    r, c_dim = w.shape
    tr = r if r <= 256 else 256
    assert r % tr == 0, (name, r)

    def body(p_ref, w_ref, m_ref, v_ref, g_ref, d_ref, nm_ref, nv_ref):
        g = p_ref[0].astype(F32)
        for s in range(1, N_DEV):
            g = g + p_ref[s].astype(F32)
        m2 = ADAM_B1 * m_ref[...] + (1.0 - ADAM_B1) * g
        v2 = ADAM_B2 * v_ref[...] + (1.0 - ADAM_B2) * (g * g)
        m_hat = m2 / (1.0 - ADAM_B1 ** ADAM_STEP)
        v_hat = v2 / (1.0 - ADAM_B2 ** ADAM_STEP)
        g_ref[...] = g
        d_ref[...] = -ADAM_LR * (m_hat / (jnp.sqrt(v_hat) + ADAM_EPS) + ADAM_WD * w_ref[...])
        nm_ref[...] = m2
        nv_ref[...] = v2

    row = pl.BlockSpec((tr, c_dim), lambda i: (i, 0))
    out = _sds((r, c_dim), F32)
    return pl.pallas_call(
        body, name=name, grid=(r // tr,), in_specs=[pl.BlockSpec((N_DEV, tr, c_dim), lambda i: (0, i, 0)), row, row, row],
        out_specs=(row, row, row, row), out_shape=(out, out, out, out), compiler_params=_cp("parallel"),
    )(parts, w, m, v)


BIG = ("w_in", "w_out", "w_q", "w_k", "w_v", "w_o", "w_gate", "w_up", "w_down")
COL_SHARDED = ("w_in", "w_o", "w_gate", "w_up")
SMALL_REPL = ("norm_mix", "ssd_conv_b", "dt_bias", "a_log", "d_skip", "ssd_norm", "sc_norm", "mem_norm", "norm_xa",
              "norm_ffn", "norm_final")
SMALL_SHARDED = ("ssd_conv_w", "sc_conv_w")
WEIGHTS = ("norm_mix", "w_in", "ssd_conv_w", "ssd_conv_b", "dt_bias", "a_log", "d_skip", "ssd_norm", "sc_conv_w",
           "sc_norm", "w_out", "mem_norm", "norm_xa", "w_q", "w_k", "w_v", "w_o", "norm_ffn", "w_gate", "w_up",
           "w_down", "norm_final")


def _gather_full(shard, col_sharded, name):
    depth, r, c_dim = shard.shape
    got = _all_gather(shard, name=name)
    if col_sharded:
        return jnp.transpose(got, (0, 2, 1, 3)).reshape(depth, r, N_DEV * c_dim)
    return got.reshape(depth, N_DEV * r, c_dim)


def _to_pieces(full, col_sharded):
    depth, rr, cc = full.shape
    if col_sharded:
        c_dim = cc // N_DEV
        return jnp.transpose(full.reshape(depth, rr, N_DEV, c_dim), (2, 0, 1, 3)).reshape(N_DEV, depth * rr, c_dim)
    r = rr // N_DEV
    return jnp.transpose(full.reshape(depth, N_DEV, r, cc), (1, 0, 2, 3)).reshape(N_DEV, depth * r, cc)


def _pack(arrs, names):
    flat = jnp.concatenate([arrs[nm].reshape(-1).astype(F32) for nm in names])
    rows = -(-flat.shape[0] // (TB * LANES)) * TB
    return jnp.pad(flat, (0, rows * LANES - flat.shape[0])).reshape(rows, LANES)


def _unpack(packed, shapes, names):
    flat = packed.reshape(-1)
    out, off = {}, 0
    for nm in names:
        size = 1
        for dim in shapes[nm]:
            size *= dim
        out[nm] = flat[off:off + size].reshape(shapes[nm])
        off += size
    return out


def kernel(x, mem, norm_mix, w_in, ssd_conv_w, ssd_conv_b, dt_bias, a_log, d_skip, ssd_norm, sc_conv_w, sc_norm, w_out, mem_norm, norm_xa, w_q, w_k, w_v, w_o, norm_ffn, w_gate, w_up, w_down, norm_final, loss_target, m_norm_mix, m_w_in, m_ssd_conv_w, m_ssd_conv_b, m_dt_bias, m_a_log, m_d_skip, m_ssd_norm, m_sc_conv_w, m_sc_norm, m_w_out, m_mem_norm, m_norm_xa, m_w_q, m_w_k, m_w_v, m_w_o, m_norm_ffn, m_w_gate, m_w_up, m_w_down, m_norm_final, v_norm_mix, v_w_in, v_ssd_conv_w, v_ssd_conv_b, v_dt_bias, v_a_log, v_d_skip, v_ssd_norm, v_sc_conv_w, v_sc_norm, v_w_out, v_mem_norm, v_norm_xa, v_w_q, v_w_k, v_w_v, v_w_o, v_norm_ffn, v_w_gate, v_w_up, v_w_down, v_norm_final):
    w = dict(norm_mix=norm_mix, w_in=w_in, ssd_conv_w=ssd_conv_w, ssd_conv_b=ssd_conv_b, dt_bias=dt_bias, a_log=a_log,
             d_skip=d_skip, ssd_norm=ssd_norm, sc_conv_w=sc_conv_w, sc_norm=sc_norm, w_out=w_out, mem_norm=mem_norm,
             norm_xa=norm_xa, w_q=w_q, w_k=w_k, w_v=w_v, w_o=w_o, norm_ffn=norm_ffn, w_gate=w_gate, w_up=w_up,
             w_down=w_down, norm_final=norm_final)
    mom = dict(norm_mix=m_norm_mix, w_in=m_w_in, ssd_conv_w=m_ssd_conv_w, ssd_conv_b=m_ssd_conv_b, dt_bias=m_dt_bias,
               a_log=m_a_log, d_skip=m_d_skip, ssd_norm=m_ssd_norm, sc_conv_w=m_sc_conv_w, sc_norm=m_sc_norm,
               w_out=m_w_out, mem_norm=m_mem_norm, norm_xa=m_norm_xa, w_q=m_w_q, w_k=m_w_k, w_v=m_w_v, w_o=m_w_o,
               norm_ffn=m_norm_ffn, w_gate=m_w_gate, w_up=m_w_up, w_down=m_w_down, norm_final=m_norm_final)
    var = dict(norm_mix=v_norm_mix, w_in=v_w_in, ssd_conv_w=v_ssd_conv_w, ssd_conv_b=v_ssd_conv_b, dt_bias=v_dt_bias,
               a_log=v_a_log, d_skip=v_d_skip, ssd_norm=v_ssd_norm, sc_conv_w=v_sc_conv_w, sc_norm=v_sc_norm,
               w_out=v_w_out, mem_norm=v_mem_norm, norm_xa=v_norm_xa, w_q=v_w_q, w_k=v_w_k, w_v=v_w_v, w_o=v_w_o,
               norm_ffn=v_norm_ffn, w_gate=v_w_gate, w_up=v_w_up, w_down=v_w_down, norm_final=v_norm_final)
    depth = w_in.shape[0]
    my = 4 * lax.axis_index("x") + 2 * lax.axis_index("y") + lax.axis_index("c")

    full = {nm: _gather_full(w[nm].astype(BF16), nm in COL_SHARDED, "ag_" + nm) for nm in BIG}
    for nm in SMALL_SHARDED:
        full[nm] = _gather_full(w[nm], True, "ag_" + nm)
    layers, gains = [], []
    for li in range(depth):
        p = _prep_layer(full["w_in"][li], full["ssd_conv_w"][li], ssd_conv_b[li], dt_bias[li], a_log[li], d_skip[li],
                        full["w_out"][li], full["w_q"][li], full["w_k"][li], full["w_v"][li], full["w_o"][li],
                        full["w_gate"][li], full["w_up"][li], full["w_down"][li])
        p["sc_conv_w"] = full["sc_conv_w"][li]
        layers.append(p)
        gains.append({nm: w[nm][li].reshape(1, D) for nm in ("norm_mix", "ssd_norm", "sc_norm", "norm_xa", "norm_ffn")})

    loss_v, grad_x, grads, d_mem_norm, d_final = _local_step(
        x[0], mem[0], loss_target[0], layers, gains, mem_norm.reshape(1, D), norm_final.reshape(1, D))
    loss = lax.psum(loss_v[0, 0], ("x", "y", "c"))

    dt0 = D + D_XBC
    gfull = {}
    gfull["w_in"] = jnp.stack([jnp.concatenate([gr["w_main"][:, :dt0], gr["w_dt"][:, :N_HEADS], gr["w_main"][:, dt0:]],
                                               axis=1) for gr in grads])
    gfull["w_gate"] = jnp.stack([gr["w_gu"][:, :D_FF] for gr in grads])
    gfull["w_up"] = jnp.stack([gr["w_gu"][:, D_FF:] for gr in grads])
    for nm in ("w_out", "w_q", "w_k", "w_v", "w_o", "w_down"):
        gfull[nm] = jnp.stack([gr[nm] for gr in grads])
    outs = {}
    for nm in BIG:
        got = _exchange(_to_pieces(gfull[nm], nm in COL_SHARDED), name="rs_" + nm)
        shp = w[nm].shape
        flat = (shp[0] * shp[1], shp[2])
        res = _adamw(got, w[nm].reshape(flat), mom[nm].reshape(flat), var[nm].reshape(flat), name="adamw_" + nm)
        outs[nm] = tuple(t.reshape(shp) for t in res)

    small = SMALL_REPL + SMALL_SHARDED
    gsmall = dict(mem_norm=d_mem_norm.reshape(D), norm_final=d_final.reshape(D))
    for nm in ("norm_mix", "ssd_norm", "sc_norm", "norm_xa", "norm_ffn"):
        gsmall[nm] = jnp.stack([gr[nm].reshape(D) for gr in grads])
    for nm in ("ssd_conv_b", "dt_bias", "a_log", "d_skip", "ssd_conv_w", "sc_conv_w"):
        gsmall[nm] = jnp.stack([gr[nm] for gr in grads])
    packed_g = _pack(gsmall, small)
    all_g = _all_gather(packed_g[None], name="ag_small_grads")[0]

    def own(arrs):
        loc = {nm: arrs[nm] for nm in SMALL_REPL}
        for nm in SMALL_SHARDED:
            loc[nm] = jnp.zeros(gsmall[nm].shape, F32)
        return loc

    def put_shard(arrs):
        loc = own(arrs)
        for nm in SMALL_SHARDED:
            cs = arrs[nm].shape[-1]
            loc[nm] = lax.dynamic_update_slice_in_dim(loc[nm], arrs[nm], my * cs, axis=2)
        return _pack(loc, small)

    res = _adamw(all_g, put_shard(w), put_shard(mom), put_shard(var), name="adamw_small")
    shapes = {nm: gsmall[nm].shape for nm in small}
    for idx in range(4):
        un = _unpack(res[idx], shapes, small)
        for nm in SMALL_REPL:
            outs.setdefault(nm, [None] * 4)[idx] = un[nm]
        for nm in SMALL_SHARDED:
            cs = w[nm].shape[-1]
            outs.setdefault(nm, [None] * 4)[idx] = lax.dynamic_slice_in_dim(un[nm], my * cs, cs, axis=2)

    return (loss, grad_x[None], *[outs[nm][0] for nm in WEIGHTS], *[outs[nm][1] for nm in WEIGHTS],
            *[outs[nm][2] for nm in WEIGHTS], *[outs[nm][3] for nm in WEIGHTS])
```

```python
import functools

import jax
import jax.numpy as jnp
from jax import lax
from jax.experimental import pallas as pl
from jax.experimental.pallas import tpu as pltpu

F32 = jnp.float32
BF16 = jnp.bfloat16

D = 2048
HEAD_DIM = 64
N_HEADS = 32
N_GROUPS = 4
N_STATE = 128
CHUNK = 256
D_XBC = D + 2 * N_GROUPS * N_STATE
SSD_GROUP_W = D // N_GROUPS
SC_GROUP_W = 128
XA_HEADS = 4
XA_HD = 128
D_XA = XA_HEADS * XA_HD
D_FF = 5632
NORM_EPS = 1e-5
PROJ_W = 11264
DT_W = 128
N_DEV = 8

ADAM_LR = 0.001
ADAM_B1 = 0.9
ADAM_B2 = 0.999
ADAM_EPS = 1e-08
ADAM_WD = 0.01
ADAM_STEP = 10

TB = CHUNK
HALO = 8
LANES = 128
VMEM_LIMIT = 56 * 1024 * 1024

NT = (((1,), (1,)), ((), ()))
TN = (((0,), (0,)), ((), ()))
MESH = pl.DeviceIdType.MESH


def _cp(*sem):
    return pltpu.CompilerParams(dimension_semantics=sem, vmem_limit_bytes=VMEM_LIMIT)


def _sds(shape, dtype):
    return jax.ShapeDtypeStruct(shape, dtype)


def _sigmoid(x):
    return 1.0 / (1.0 + jnp.exp(-x))


def _split3_dot(t_bf16, x):
    hi = x.astype(BF16)
    r1 = x - hi.astype(F32)
    mid = r1.astype(BF16)
    lo = (r1 - mid.astype(F32)).astype(BF16)
    out = jnp.dot(t_bf16, hi, preferred_element_type=F32)
    out = out + jnp.dot(t_bf16, mid, preferred_element_type=F32)
    return out + jnp.dot(t_bf16, lo, preferred_element_type=F32)


def _split3_dot_r(x, t_bf16):
    hi = x.astype(BF16)
    r1 = x - hi.astype(F32)
    mid = r1.astype(BF16)
    lo = (r1 - mid.astype(F32)).astype(BF16)
    out = jnp.dot(hi, t_bf16, preferred_element_type=F32)
    out = out + jnp.dot(mid, t_bf16, preferred_element_type=F32)
    return out + jnp.dot(lo, t_bf16, preferred_element_type=F32)


def _group_bcast(stat_fn, v, gw):
    pieces = []
    for g in range(v.shape[1] // gw):
        vs = v[:, g * gw:(g + 1) * gw]
        pieces.append(jnp.broadcast_to(stat_fn(vs), vs.shape))
    return jnp.concatenate(pieces, axis=1) if len(pieces) > 1 else pieces[0]


def _group_rstd(v, gw):
    return _group_bcast(lambda s: lax.rsqrt(jnp.mean(s * s, axis=1, keepdims=True) + NORM_EPS), v, gw)


def _group_mean(v, gw):
    return _group_bcast(lambda s: jnp.mean(s, axis=1, keepdims=True), v, gw)


def _pair_sum(d, first):
    s0 = jnp.sum(jnp.where(first, d, 0.0), axis=1, keepdims=True)
    s1 = jnp.sum(jnp.where(first, 0.0, d), axis=1, keepdims=True)
    return jnp.where(first, s0, s1)


class _Side:
    def __init__(self, arrays, out_shapes, n_remote, n_local, start, finish):
        self.arrays, self.out_shapes = tuple(arrays), tuple(out_shapes)
        self.n_remote, self.n_local = n_remote, n_local
        self.start, self.finish = start, finish

    def scratch(self):
        return [pltpu.SemaphoreType.DMA((self.n_remote,)), pltpu.SemaphoreType.DMA((self.n_remote,)),
                pltpu.SemaphoreType.DMA((self.n_local,))]


def _ag_side(shards):
    n = len(shards)

    def plan(x_refs, out_refs, send, recv, local, starting=False):
        x, y, c = lax.axis_index("x"), lax.axis_index("y"), lax.axis_index("c")
        me, sibling = (x, y, c), (x, y, 1 - c)
        chips = [(1 - x, y), (x, 1 - y), (1 - x, 1 - y)]
        jobs = []
        for a in range(n):
            def rows(px, py, pc, out=out_refs[a]):
                return out.at[:, 4 * px + 2 * py + pc]

            def copy(k, block, to, src=None, a=a, rows=rows):
                return pltpu.make_async_remote_copy(
                    src_ref=rows(*block) if src is None else src, dst_ref=rows(*block),
                    send_sem=send.at[7 * a + k], recv_sem=recv.at[7 * a + k], device_id=to, device_id_type=MESH)

            mine = pltpu.make_async_copy(x_refs[a], rows(*me), local.at[a])
            first = [copy(0, me, sibling, src=x_refs[a])]
            first += [copy(1 + j, me, (*chip, c), src=x_refs[a]) for j, chip in enumerate(chips)]
            if starting:
                jobs.append((mine, first))
                continue
            passed = [copy(4 + j, (*chip, c), sibling) for j, chip in enumerate(chips)]
            arrive = [copy(1 + j, (*chip, c), me) for j, chip in enumerate(chips)]
            late = [copy(0, sibling, me)] + [copy(4 + j, (*chip, 1 - c), me) for j, chip in enumerate(chips)]
            jobs.append((mine, first, passed, arrive, late))
        return jobs

    def start(*refs):
        for mine, first in plan(*refs, starting=True):
            mine.start()
            for cp in first:
                cp.start()

    def finish(*refs):
        jobs = plan(*refs)
        for j in range(3):
            for _, _, passed, arrive, _ in jobs:
                arrive[j].wait_recv()
                passed[j].start()
        for mine, first, passed, _, late in jobs:
            for cp in late:
                cp.wait_recv()
            for cp in first + passed:
                cp.wait_send()
            mine.wait()

    outs = [_sds((s.shape[0], N_DEV) + s.shape[1:], s.dtype) for s in shards]
    return _Side(shards, outs, 7 * n, n, start, finish)


def _flip(k, x, y, c):
    return (1 - x if k & 4 else x, 1 - y if k & 2 else y, 1 - c if k & 1 else c)


def _rs_side(pieces):
    n = len(pieces)

    def plan(g_refs, out_refs, send, recv, local):
        x, y, c = lax.axis_index("x"), lax.axis_index("y"), lax.axis_index("c")
        me = 4 * x + 2 * y + c
        jobs = []
        for a in range(n):
            mine = pltpu.make_async_copy(g_refs[a].at[me], out_refs[a].at[me], local.at[a])
            copies = []
            for k in range(1, N_DEV):
                px, py, pc = _flip(k, x, y, c)
                copies.append(pltpu.make_async_remote_copy(
                    src_ref=g_refs[a].at[4 * px + 2 * py + pc], dst_ref=out_refs[a].at[me],
                    send_sem=send.at[7 * a + k - 1], recv_sem=recv.at[7 * a + k - 1],
                    device_id=(px, py, pc), device_id_type=MESH))
            jobs.append((mine, copies))
        return jobs

    def start(*refs):
        for mine, copies in plan(*refs):
            mine.start()
            for cp in copies:
                cp.start()

    def finish(*refs):
        for mine, copies in plan(*refs):
            for cp in copies:
                cp.wait_recv()
            for cp in copies:
                cp.wait_send()
            mine.wait()

    return _Side(pieces, [_sds(p.shape, p.dtype) for p in pieces], 7 * n, n, start, finish)


def _run_side(side, *, name):
    n_in, n_out = len(side.arrays), len(side.out_shapes)

    def body(*refs):
        parts = (refs[:n_in], refs[n_in:n_in + n_out]) + tuple(refs[n_in + n_out:])
        side.start(*parts)
        side.finish(*parts)

    hbm = pl.BlockSpec(memory_space=pl.ANY)
    return pl.pallas_call(
        body, name=name, out_shape=side.out_shapes, in_specs=[hbm] * n_in, out_specs=tuple([hbm] * n_out),
        scratch_shapes=side.scratch(),
    )(*side.arrays)


def _mm(a, b, *, name, ta=False, tb=False, res=None, out_dtype=F32, bm=1024, bn=1024, bk=None, side=None):
    if ta:
        k_dim, m_dim = a.shape
    else:
        m_dim, k_dim = a.shape
    if tb:
        n_dim, kb = b.shape
    else:
        kb, n_dim = b.shape
    assert k_dim == kb, (a.shape, b.shape)
    bm, bn = min(bm, m_dim), min(bn, n_dim)
    bk = k_dim if bk is None else min(bk, k_dim)
    assert m_dim % bm == 0 and n_dim % bn == 0 and k_dim % bk == 0, (name, a.shape, b.shape, bm, bn, bk)
    grid = (m_dim // bm, n_dim // bn, k_dim // bk)
    nk = grid[2]
    dn = (((0,) if ta else (1,), (1,) if tb else (0,)), ((), ()))
    has_res = res is not None
    n_main_in = 2 + has_res
    n_side_in = len(side.arrays) if side else 0
    n_side_out = len(side.out_shapes) if side else 0

    def body(*refs):
        a_ref, b_ref = refs[0], refs[1]
        r_ref = refs[2] if has_res else None
        o_ref = refs[n_main_in + n_side_in]
        scratch = refs[n_main_in + n_side_in + 1 + n_side_out:]
        if side:
            side_refs = (refs[n_main_in:n_main_in + n_side_in],
                         refs[n_main_in + n_side_in + 1:n_main_in + n_side_in + 1 + n_side_out]) + tuple(scratch[-3:])
            step = (pl.program_id(0) * grid[1] + pl.program_id(1)) * grid[2] + pl.program_id(2)

            @pl.when(step == 0)
            def _():
                side.start(*side_refs)

        p = lax.dot_general(a_ref[...].astype(BF16), b_ref[...].astype(BF16), dn, preferred_element_type=F32)
        if nk == 1:
            if has_res:
                p = p + r_ref[...]
            o_ref[...] = p.astype(o_ref.dtype)
        else:
            acc_ref = scratch[0]
            k = pl.program_id(2)

            @pl.when(k == 0)
            def _():
                acc_ref[...] = p

            @pl.when(k > 0)
            def _():
                acc_ref[...] += p

            @pl.when(k == nk - 1)
            def _():
                r = acc_ref[...]
                if has_res:
                    r = r + r_ref[...]
                o_ref[...] = r.astype(o_ref.dtype)

        if side:
            @pl.when(step == grid[0] * grid[1] * grid[2] - 1)
            def _():
                side.finish(*side_refs)

    a_spec = pl.BlockSpec((bk, bm), lambda i, j, k: (k, i)) if ta else pl.BlockSpec((bm, bk), lambda i, j, k: (i, k))
    b_spec = pl.BlockSpec((bn, bk), lambda i, j, k: (j, k)) if tb else pl.BlockSpec((bk, bn), lambda i, j, k: (k, j))
    o_spec = pl.BlockSpec((bm, bn), lambda i, j, k: (i, j))
    hbm = pl.BlockSpec(memory_space=pl.ANY)
    in_specs = [a_spec, b_spec] + ([o_spec] if has_res else []) + [hbm] * n_side_in
    args = (a, b) + ((res,) if has_res else ()) + (side.arrays if side else ())
    scratch_shapes = ([pltpu.VMEM((bm, bn), F32)] if nk > 1 else []) + (side.scratch() if side else [])
    out_main = _sds((m_dim, n_dim), out_dtype)
    if not side:
        return pl.pallas_call(
            body, name=name, grid=grid, in_specs=in_specs, out_specs=o_spec, out_shape=out_main,
            scratch_shapes=scratch_shapes, compiler_params=_cp("parallel", "parallel", "arbitrary"),
        )(*args)
    outs = pl.pallas_call(
        body, name=name, grid=grid, in_specs=in_specs, out_specs=(o_spec,) + tuple([hbm] * n_side_out),
        out_shape=(out_main,) + side.out_shapes, scratch_shapes=scratch_shapes,
        compiler_params=_cp("arbitrary", "arbitrary", "arbitrary"),
    )(*args)
    return outs[0], tuple(outs[1:])


def _norm_fwd(h, g, *, name, out_dtype=BF16):
    s, d = h.shape
    tb = min(TB, s)

    def body(h_ref, g_ref, o_ref):
        x = h_ref[...]
        r = lax.rsqrt(jnp.mean(x * x, axis=-1, keepdims=True) + NORM_EPS)
        o_ref[...] = (x * r * g_ref[...]).astype(o_ref.dtype)

    row = pl.BlockSpec((tb, d), lambda i: (i, 0))
    return pl.pallas_call(
        body, name=name, grid=(s // tb,), in_specs=[row, pl.BlockSpec((1, d), lambda i: (0, 0))], out_specs=row,
        out_shape=_sds((s, d), out_dtype), compiler_params=_cp("parallel"),
    )(h, g)


def _norm_bwd(h, g, dhn, dres, *, name):
    s, d = h.shape
    tb = min(TB, s)
    has_res = dres is not None

    def body(*refs):
        h_ref, g_ref, dhn_ref = refs[:3]
        r_ref = refs[3] if has_res else None
        dh_ref, dhb_ref, dg_ref = refs[3 + has_res:]
        x = h_ref[...]
        r = lax.rsqrt(jnp.mean(x * x, axis=-1, keepdims=True) + NORM_EPS)
        xhat = x * r
        dy = dhn_ref[...]
        gy = dy * g_ref[...]
        dx = r * (gy - xhat * jnp.mean(gy * xhat, axis=-1, keepdims=True))
        if has_res:
            dx = dx + r_ref[...]
        dh_ref[...] = dx
        dhb_ref[...] = dx.astype(BF16)
        part = jnp.sum(dy * xhat, axis=0, keepdims=True)

        @pl.when(pl.program_id(0) == 0)
        def _():
            dg_ref[...] = part

        @pl.when(pl.program_id(0) > 0)
        def _():
            dg_ref[...] += part

    row = pl.BlockSpec((tb, d), lambda i: (i, 0))
    vec = pl.BlockSpec((1, d), lambda i: (0, 0))
    return pl.pallas_call(
        body, name=name, grid=(s // tb,), in_specs=[row, vec, row] + ([row] if has_res else []),
        out_specs=(row, row, vec), out_shape=(_sds((s, d), F32), _sds((s, d), BF16), _sds((1, d), F32)),
        compiler_params=_cp("arbitrary"),
    )(*((h, g, dhn) + ((dres,) if has_res else ())))


def _loss_head(h, g, tgt, *, name):
    s, d = h.shape
    tb = min(TB, s)

    def body(h_ref, g_ref, t_ref, loss_ref, dh_ref, dhb_ref, dg_ref):
        x = h_ref[...]
        r = lax.rsqrt(jnp.mean(x * x, axis=-1, keepdims=True) + NORM_EPS)
        xhat = x * r
        gain = g_ref[...]
        err = xhat * gain - t_ref[...]
        part_loss = 0.5 * jnp.sum(jnp.mean(err * err, axis=-1, keepdims=True), axis=0, keepdims=True)
        dy = err * (1.0 / d)
        gy = dy * gain
        dx = r * (gy - xhat * jnp.mean(gy * xhat, axis=-1, keepdims=True))
        dh_ref[...] = dx
        dhb_ref[...] = dx.astype(BF16)
        part = jnp.sum(dy * xhat, axis=0, keepdims=True)
        lossv = jnp.broadcast_to(part_loss, (1, LANES))

        @pl.when(pl.program_id(0) == 0)
        def _():
            dg_ref[...] = part
            loss_ref[...] = lossv

        @pl.when(pl.program_id(0) > 0)
        def _():
            dg_ref[...] += part
            loss_ref[...] += lossv

    row = pl.BlockSpec((tb, d), lambda i: (i, 0))
    vec = pl.BlockSpec((1, d), lambda i: (0, 0))
    return pl.pallas_call(
        body, name=name, grid=(s // tb,), in_specs=[row, vec, row],
        out_specs=(pl.BlockSpec((1, LANES), lambda i: (0, 0)), row, row, vec),
        out_shape=(_sds((1, LANES), F32), _sds((s, d), F32), _sds((s, d), BF16), _sds((1, d), F32)),
        compiler_params=_cp("arbitrary"),
    )(h, g, tgt)


def _swiglu_fwd(gu, *, name):
    s = gu.shape[0]
    tb = min(TB, s)

    def body(g_ref, u_ref, o_ref):
        g = g_ref[...]
        o_ref[...] = (g * _sigmoid(g) * u_ref[...]).astype(o_ref.dtype)

    return pl.pallas_call(
        body, name=name, grid=(s // tb,),
        in_specs=[pl.BlockSpec((tb, D_FF), lambda i: (i, 0)), pl.BlockSpec((tb, D_FF), lambda i: (i, 1))],
        out_specs=pl.BlockSpec((tb, D_FF), lambda i: (i, 0)), out_shape=_sds((s, D_FF), BF16),
        compiler_params=_cp("parallel"),
    )(gu, gu)


def _swiglu_bwd(gu, dact, *, name):
    s = gu.shape[0]
    tb = min(TB, s)

    def body(g_ref, u_ref, d_ref, o_ref):
        g = g_ref[...]
        sg = _sigmoid(g)
        da = d_ref[...]
        o_ref[:, :D_FF] = (da * u_ref[...] * sg * (1.0 + g * (1.0 - sg))).astype(o_ref.dtype)
        o_ref[:, D_FF:] = (da * g * sg).astype(o_ref.dtype)

    return pl.pallas_call(
        body, name=name, grid=(s // tb,),
        in_specs=[pl.BlockSpec((tb, D_FF), lambda i: (i, 0)), pl.BlockSpec((tb, D_FF), lambda i: (i, 1)),
                  pl.BlockSpec((tb, D_FF), lambda i: (i, 0))],
        out_specs=pl.BlockSpec((tb, 2 * D_FF), lambda i: (i, 0)), out_shape=_sds((s, 2 * D_FF), BF16),
        compiler_params=_cp("parallel"),
    )(gu, gu, dact)


def _softmax_rows(qh, kh):
    sc = lax.dot_general(qh, kh, NT, preferred_element_type=F32) * (XA_HD ** -0.5)
    sc = sc - jnp.max(sc, axis=-1, keepdims=True)
    e = jnp.exp(sc)
    return e / jnp.sum(e, axis=-1, keepdims=True)


def _attn_fwd(q, k, v, *, name):
    s = q.shape[0]
    n_mem = k.shape[0]
    tq = min(512, s)

    def body(q_ref, k_ref, v_ref, o_ref):
        outs = []
        for h in range(XA_HEADS):
            sl = slice(h * XA_HD, (h + 1) * XA_HD)
            p = _softmax_rows(q_ref[:, sl], k_ref[:, sl])
            outs.append(jnp.dot(p.astype(BF16), v_ref[:, sl], preferred_element_type=F32))
        o_ref[...] = jnp.concatenate(outs, axis=1).astype(o_ref.dtype)

    row = pl.BlockSpec((tq, D_XA), lambda i: (i, 0))
    kv = pl.BlockSpec((n_mem, D_XA), lambda i: (0, 0))
    return pl.pallas_call(
        body, name=name, grid=(s // tq,), in_specs=[row, kv, kv], out_specs=row, out_shape=_sds((s, D_XA), BF16),
        compiler_params=_cp("parallel"),
    )(q, k, v)


def _attn_bwd(q, k, v, do, *, name):
    s = q.shape[0]
    n_mem = k.shape[0]
    tq = min(512, s)

    def body(q_ref, k_ref, v_ref, do_ref, dq_ref, dk_ref, dv_ref):
        dqs, dks, dvs = [], [], []
        for h in range(XA_HEADS):
            sl = slice(h * XA_HD, (h + 1) * XA_HD)
            qh, kh, vh, doh = q_ref[:, sl], k_ref[:, sl], v_ref[:, sl], do_ref[:, sl]
            p = _softmax_rows(qh, kh)
            dvs.append(lax.dot_general(p.astype(BF16), doh, TN, preferred_element_type=F32))
            dp = lax.dot_general(doh, vh, NT, preferred_element_type=F32)
            ds = (p * (dp - jnp.sum(dp * p, axis=-1, keepdims=True)) * (XA_HD ** -0.5)).astype(BF16)
            dqs.append(jnp.dot(ds, kh, preferred_element_type=F32))
            dks.append(lax.dot_general(ds, qh, TN, preferred_element_type=F32))
        dq_ref[...] = jnp.concatenate(dqs, axis=1).astype(dq_ref.dtype)
        dk = jnp.concatenate(dks, axis=1)
        dv = jnp.concatenate(dvs, axis=1)

        @pl.when(pl.program_id(0) == 0)
        def _():
            dk_ref[...] = dk
            dv_ref[...] = dv

        @pl.when(pl.program_id(0) > 0)
        def _():
            dk_ref[...] += dk
            dv_ref[...] += dv

    row = pl.BlockSpec((tq, D_XA), lambda i: (i, 0))
    kv = pl.BlockSpec((n_mem, D_XA), lambda i: (0, 0))
    return pl.pallas_call(
        body, name=name, grid=(s // tq,), in_specs=[row, kv, kv, row], out_specs=(row, kv, kv),
        out_shape=(_sds((s, D_XA), BF16), _sds((n_mem, D_XA), F32), _sds((n_mem, D_XA), F32)),
        compiler_params=_cp("arbitrary"),
    )(q, k, v, do)


CONV_CB = 1024
XBC_CB0 = D // CONV_CB


def _prev_rows(i, tb):
    return jnp.maximum(i * (tb // HALO) - 1, 0)


def _next_rows(i, tb, s):
    return jnp.minimum((i + 1) * (tb // HALO), s // HALO - 1)


def _conv4(xcat, w_ref, lo):
    acc = w_ref[3:4, :] * xcat[lo:]
    for k in range(3):
        acc = acc + w_ref[k:k + 1, :] * pltpu.roll(xcat, 3 - k, 0)[lo:]
    return acc


def _conv_silu_fwd(proj, w, b, *, name):
    s = proj.shape[0]
    tb = min(TB, s)

    def body(x_ref, xp_ref, w_ref, b_ref, o_ref):
        i = pl.program_id(0)
        prev = jnp.where(i > 0, xp_ref[...], 0.0)
        pre = _conv4(jnp.concatenate([prev, x_ref[...]], axis=0), w_ref, HALO) + b_ref[...]
        o_ref[...] = pre * _sigmoid(pre)

    return pl.pallas_call(
        body, name=name, grid=(s // tb, D_XBC // CONV_CB),
        in_specs=[pl.BlockSpec((tb, CONV_CB), lambda i, j: (i, XBC_CB0 + j)),
                  pl.BlockSpec((HALO, CONV_CB), lambda i, j: (_prev_rows(i, tb), XBC_CB0 + j)),
                  pl.BlockSpec((4, CONV_CB), lambda i, j: (0, j)),
                  pl.BlockSpec((1, CONV_CB), lambda i, j: (0, j))],
        out_specs=pl.BlockSpec((tb, CONV_CB), lambda i, j: (i, j)), out_shape=_sds((s, D_XBC), F32),
        compiler_params=_cp("parallel", "parallel"),
    )(proj, proj, w, b)


def _conv_silu_bwd(proj, dy, w, b, *, name):
    s = proj.shape[0]
    tb = min(TB, s)
    n_i = s // tb

    def body(x_ref, xp_ref, xn_ref, dy_ref, dyn_ref, w_ref, b_ref, dx_ref, dw_ref, db_ref):
        i = pl.program_id(1)
        prev = jnp.where(i > 0, xp_ref[...], 0.0)
        xcat = jnp.concatenate([prev, x_ref[...], xn_ref[...]], axis=0)
        pre = _conv4(xcat, w_ref, HALO) + b_ref[...]
        sg = _sigmoid(pre)
        dy_ext = jnp.concatenate([dy_ref[...], jnp.where(i < n_i - 1, dyn_ref[...], 0.0)], axis=0)
        dpre = dy_ext * sg * (1.0 + pre * (1.0 - sg))
        n2 = tb + HALO
        dx = w_ref[3:4, :] * dpre[:tb]
        for k in range(3):
            dx = dx + w_ref[k:k + 1, :] * pltpu.roll(dpre, n2 - (3 - k), 0)[:tb]
        dx_ref[...] = dx.astype(dx_ref.dtype)
        dpc = dpre[:tb]
        parts = [jnp.sum(dpc * pltpu.roll(xcat, 3 - k, 0)[HALO:HALO + tb], axis=0, keepdims=True) for k in range(3)]
        parts.append(jnp.sum(dpc * x_ref[...], axis=0, keepdims=True))
        dbp = jnp.sum(dpc, axis=0, keepdims=True)

        @pl.when(i == 0)
        def _():
            for k in range(4):
                dw_ref[k] = parts[k]
            db_ref[...] = dbp

        @pl.when(i > 0)
        def _():
            for k in range(4):
                dw_ref[k] += parts[k]
            db_ref[...] += dbp

    return pl.pallas_call(
        body, name=name, grid=(D_XBC // CONV_CB, n_i),
        in_specs=[pl.BlockSpec((tb, CONV_CB), lambda j, i: (i, XBC_CB0 + j)),
                  pl.BlockSpec((HALO, CONV_CB), lambda j, i: (_prev_rows(i, tb), XBC_CB0 + j)),
                  pl.BlockSpec((HALO, CONV_CB), lambda j, i: (_next_rows(i, tb, s), XBC_CB0 + j)),
                  pl.BlockSpec((tb, CONV_CB), lambda j, i: (i, j)),
                  pl.BlockSpec((HALO, CONV_CB), lambda j, i: (_next_rows(i, tb, s), j)),
                  pl.BlockSpec((4, CONV_CB), lambda j, i: (0, j)),
                  pl.BlockSpec((1, CONV_CB), lambda j, i: (0, j))],
        out_specs=(pl.BlockSpec((tb, CONV_CB), lambda j, i: (i, j)),
                   pl.BlockSpec((4, 1, CONV_CB), lambda j, i: (0, 0, j)),
                   pl.BlockSpec((1, CONV_CB), lambda j, i: (0, j))),
        out_shape=(_sds((s, D_XBC), BF16), _sds((4, 1, D_XBC), F32), _sds((1, D_XBC), F32)),
        compiler_params=_cp("parallel", "arbitrary"),
    )(proj, proj, proj, dy, dy, w, b)


SC_U0 = (D + D_XBC) // CONV_CB
SC_B0 = SC_U0 + D // CONV_CB
SC_C0 = SC_B0 + D // CONV_CB


def _conv3(cat, w_ref, lo):
    return (w_ref[2:3, :] * cat[lo:] + w_ref[1:2, :] * pltpu.roll(cat, 1, 0)[lo:]
            + w_ref[0:1, :] * pltpu.roll(cat, 2, 0)[lo:])


def _sc_fwd(proj, w, gain, *, name):
    s = proj.shape[0]
    tb = min(TB, s)

    def body(u_ref, b_ref, c_ref, up_ref, cp_ref, w_ref, g_ref, o_ref):
        i = pl.program_id(0)
        cup = jnp.where(i > 0, cp_ref[...] * up_ref[...], 0.0)
        cat = jnp.concatenate([cup, c_ref[...] * u_ref[...]], axis=0)
        v = b_ref[...] * _conv3(cat, w_ref, HALO)
        o_ref[...] = (v * _group_rstd(v, SC_GROUP_W) * g_ref[...]).astype(o_ref.dtype)

    def cur(c0):
        return pl.BlockSpec((tb, CONV_CB), lambda i, j: (i, c0 + j))

    def prev(c0):
        return pl.BlockSpec((HALO, CONV_CB), lambda i, j: (_prev_rows(i, tb), c0 + j))

    return pl.pallas_call(
        body, name=name, grid=(s // tb, D // CONV_CB),
        in_specs=[cur(SC_U0), cur(SC_B0), cur(SC_C0), prev(SC_U0), prev(SC_C0),
                  pl.BlockSpec((3, CONV_CB), lambda i, j: (0, j)), pl.BlockSpec((1, CONV_CB), lambda i, j: (0, j))],
        out_specs=pl.BlockSpec((tb, CONV_CB), lambda i, j: (i, j)), out_shape=_sds((s, D), BF16),
        compiler_params=_cp("parallel", "parallel"),
    )(proj, proj, proj, proj, proj, w, gain)


def _sc_bwd(proj, dcat, w, gain, *, name):
    s = proj.shape[0]
    tb = min(TB, s)
    n_i = s // tb
    dy0 = D // CONV_CB

    def body(u_ref, b_ref, c_ref, up_ref, cp_ref, un_ref, bn_ref, cn_ref, dy_ref, dyn_ref, w_ref, g_ref,
             du_ref, db_ref, dc_ref, dw_ref, dg_ref):
        i = pl.program_id(1)
        u, c = u_ref[...], c_ref[...]
        cup = jnp.where(i > 0, cp_ref[...] * up_ref[...], 0.0)
        cu_ext = jnp.concatenate([cup, c * u, cn_ref[...] * un_ref[...]], axis=0)
        conv = _conv3(cu_ext, w_ref, HALO)
        b_ext = jnp.concatenate([b_ref[...], bn_ref[...]], axis=0)
        dy_ext = jnp.concatenate([dy_ref[...], jnp.where(i < n_i - 1, dyn_ref[...], 0.0)], axis=0)
        v = b_ext * conv
        r = _group_rstd(v, SC_GROUP_W)
        vhat = v * r
        dvhat = dy_ext * g_ref[...]
        dv = r * (dvhat - vhat * _group_mean(dvhat * vhat, SC_GROUP_W))
        dconv = dv * b_ext
        n2 = tb + HALO
        dcu = (w_ref[2:3, :] * dconv[:tb] + w_ref[1:2, :] * pltpu.roll(dconv, n2 - 1, 0)[:tb]
               + w_ref[0:1, :] * pltpu.roll(dconv, n2 - 2, 0)[:tb])
        du_ref[...] = (dcu * c).astype(du_ref.dtype)
        dc_ref[...] = (dcu * u).astype(dc_ref.dtype)
        db_ref[...] = (dv * conv)[:tb].astype(db_ref.dtype)
        dcc = dconv[:tb]
        parts = [jnp.sum(dcc * pltpu.roll(cu_ext, 2 - k, 0)[HALO:HALO + tb], axis=0, keepdims=True) for k in range(2)]
        parts.append(jnp.sum(dcc * c * u, axis=0, keepdims=True))
        dgp = jnp.sum((dy_ext * vhat)[:tb], axis=0, keepdims=True)

        @pl.when(i == 0)
        def _():
            for k in range(3):
                dw_ref[k] = parts[k]
            dg_ref[...] = dgp

        @pl.when(i > 0)
        def _():
            for k in range(3):
                dw_ref[k] += parts[k]
            dg_ref[...] += dgp

    def cur(c0):
        return pl.BlockSpec((tb, CONV_CB), lambda j, i: (i, c0 + j))

    def prev(c0):
        return pl.BlockSpec((HALO, CONV_CB), lambda j, i: (_prev_rows(i, tb), c0 + j))

    def nxt(c0):
        return pl.BlockSpec((HALO, CONV_CB), lambda j, i: (_next_rows(i, tb, s), c0 + j))

    vec = pl.BlockSpec((1, CONV_CB), lambda j, i: (0, j))
    out_row = pl.BlockSpec((tb, CONV_CB), lambda j, i: (i, j))
    return pl.pallas_call(
        body, name=name, grid=(D // CONV_CB, n_i),
        in_specs=[cur(SC_U0), cur(SC_B0), cur(SC_C0), prev(SC_U0), prev(SC_C0), nxt(SC_U0), nxt(SC_B0), nxt(SC_C0),
                  cur(dy0), nxt(dy0), pl.BlockSpec((3, CONV_CB), lambda j, i: (0, j)), vec],
        out_specs=(out_row, out_row, out_row, pl.BlockSpec((3, 1, CONV_CB), lambda j, i: (0, 0, j)), vec),
        out_shape=(_sds((s, D), BF16), _sds((s, D), BF16), _sds((s, D), BF16), _sds((3, 1, D), F32), _sds((1, D), F32)),
        compiler_params=_cp("parallel", "arbitrary"),
    )(proj, proj, proj, proj, proj, proj, proj, proj, dcat, dcat, w, gain)


def _gated_norm_fwd(y, proj, gain, *, name):
    s = y.shape[0]
    tb = min(TB, s)

    def body(y_ref, z_ref, g_ref, o_ref):
        z = z_ref[...]
        t = y_ref[...] * z * _sigmoid(z)
        o_ref[...] = (t * _group_rstd(t, SSD_GROUP_W) * g_ref[...]).astype(o_ref.dtype)

    row = pl.BlockSpec((tb, D), lambda i: (i, 0))
    return pl.pallas_call(
        body, name=name, grid=(s // tb,), in_specs=[row, row, pl.BlockSpec((1, D), lambda i: (0, 0))], out_specs=row,
        out_shape=_sds((s, D), BF16), compiler_params=_cp("parallel"),
    )(y, proj, gain)


def _gated_norm_bwd(y, proj, dcat, gain, *, name):
    s = y.shape[0]
    tb = min(TB, s)

    def body(y_ref, z_ref, d_ref, g_ref, dy_ref, dz_ref, dg_ref):
        z, yv, dout = z_ref[...], y_ref[...], d_ref[...]
        sg = _sigmoid(z)
        sz = z * sg
        t = yv * sz
        r = _group_rstd(t, SSD_GROUP_W)
        that = t * r
        dthat = dout * g_ref[...]
        dt = r * (dthat - that * _group_mean(dthat * that, SSD_GROUP_W))
        dy_ref[...] = dt * sz
        dz_ref[...] = (dt * yv * sg * (1.0 + z * (1.0 - sg))).astype(dz_ref.dtype)
        part = jnp.sum(dout * that, axis=0, keepdims=True)

        @pl.when(pl.program_id(0) == 0)
        def _():
            dg_ref[...] = part

        @pl.when(pl.program_id(0) > 0)
        def _():
            dg_ref[...] += part

    row = pl.BlockSpec((tb, D), lambda i: (i, 0))
    vec = pl.BlockSpec((1, D), lambda i: (0, 0))
    return pl.pallas_call(
        body, name=name, grid=(s // tb,), in_specs=[row, row, row, vec], out_specs=(row, row, vec),
        out_shape=(_sds((s, D), F32), _sds((s, D), BF16), _sds((1, D), F32)), compiler_params=_cp("arbitrary"),
    )(y, proj, dcat, gain)


def _tri(lower):
    row = lax.broadcasted_iota(jnp.int32, (CHUNK, CHUNK), 0)
    col = lax.broadcasted_iota(jnp.int32, (CHUNK, CHUNK), 1)
    return jnp.where(row >= col if lower else col >= row, 1.0, 0.0).astype(BF16)


def _head_spread():
    row = lax.broadcasted_iota(jnp.int32, (DT_W, D), 0)
    col = lax.broadcasted_iota(jnp.int32, (DT_W, D), 1)
    return jnp.where(col // HEAD_DIM == row, 1.0, 0.0).astype(BF16)


def _head_pick():
    row = lax.broadcasted_iota(jnp.int32, (D, DT_W), 0)
    col = lax.broadcasted_iota(jnp.int32, (D, DT_W), 1)
    return jnp.where(row == col * HEAD_DIM, 1.0, 0.0).astype(BF16)


def _dt_fwd(dt_raw, bias, a_log, *, name):
    s = dt_raw.shape[0]
    nc = s // CHUNK

    def body(raw_ref, bias_ref, alog_ref, dt_ref, cum_ref, cumt_ref):
        x = raw_ref[...] + bias_ref[...]
        dt = jnp.maximum(x, 0.0) + jnp.log1p(jnp.exp(-jnp.abs(x)))
        cum = _split3_dot(_tri(True), dt * (-jnp.exp(alog_ref[...])))
        spread = _head_spread()
        dt_ref[...] = _split3_dot_r(dt, spread)
        cum_ref[...] = _split3_dot_r(cum, spread)
        cumt_ref[0] = cum.T

    row = pl.BlockSpec((CHUNK, DT_W), lambda i: (i, 0))
    wide = pl.BlockSpec((CHUNK, D), lambda i: (i, 0))
    vec = pl.BlockSpec((1, DT_W), lambda i: (0, 0))
    return pl.pallas_call(
        body, name=name, grid=(nc,), in_specs=[row, vec, vec],
        out_specs=(wide, wide, pl.BlockSpec((1, DT_W, CHUNK), lambda i: (i, 0, 0))),
        out_shape=(_sds((s, D), F32), _sds((s, D), F32), _sds((nc, DT_W, CHUNK), F32)),
        compiler_params=_cp("parallel"),
    )(dt_raw, bias, a_log)


def _dt_bwd(dt_raw, bias, ddt_b, *, name):
    s = dt_raw.shape[0]
    tb = min(TB, s)

    def body(raw_ref, bias_ref, d_ref, o_ref, db_ref):
        g = _split3_dot_r(d_ref[...], _head_pick()) * _sigmoid(raw_ref[...] + bias_ref[...])
        o_ref[...] = g.astype(o_ref.dtype)
        part = jnp.sum(g, axis=0, keepdims=True)

        @pl.when(pl.program_id(0) == 0)
        def _():
            db_ref[...] = part

        @pl.when(pl.program_id(0) > 0)
        def _():
            db_ref[...] += part

    row = pl.BlockSpec((tb, DT_W), lambda i: (i, 0))
    vec = pl.BlockSpec((1, DT_W), lambda i: (0, 0))
    return pl.pallas_call(
        body, name=name, grid=(s // tb,), in_specs=[row, vec, pl.BlockSpec((tb, D), lambda i: (i, 0))],
        out_specs=(row, vec), out_shape=(_sds((s, DT_W), BF16), _sds((1, DT_W), F32)),
        compiler_params=_cp("arbitrary"),
    )(dt_raw, bias, ddt_b)


N_PAIRS = N_HEADS // 2
PAIRS_PER_GROUP = N_PAIRS // N_GROUPS
GROUP_W = PAIRS_PER_GROUP * LANES
B_CB0 = D // LANES
C_CB0 = B_CB0 + N_GROUPS


def _decay(cum_col, cum_row, causal):
    return jnp.where(causal, jnp.exp(jnp.minimum(cum_col - cum_row, 0.0)), 0.0)


def _causal_mask():
    row = lax.broadcasted_iota(jnp.int32, (CHUNK, CHUNK), 0)
    col = lax.broadcasted_iota(jnp.int32, (CHUNK, CHUNK), 1)
    return row >= col


def _state_row_scale(cumt_ref, pp):
    last0 = cumt_ref[0, 2 * pp][:, CHUNK - 1:CHUNK]
    last1 = cumt_ref[0, 2 * pp + 1][:, CHUNK - 1:CHUNK]
    rown = lax.broadcasted_iota(jnp.int32, (LANES, 1), 0)
    return jnp.exp(jnp.where(rown < HEAD_DIM, last0, last1))


def _ssd_specs(nc, rev):
    def ch(c):
        return nc - 1 - c if rev else c

    wide = pl.BlockSpec((CHUNK, GROUP_W), lambda c, g: (ch(c), g))
    vec = pl.BlockSpec((1, GROUP_W), lambda c, g: (0, g))
    cumt_spec = pl.BlockSpec((1, 2 * PAIRS_PER_GROUP, 1, CHUNK), lambda c, g: (ch(c), g, 0, 0))
    hp_spec = pl.BlockSpec((1, PAIRS_PER_GROUP, LANES, N_STATE), lambda c, g: (ch(c), g, 0, 0))

    def bc(c0):
        return pl.BlockSpec((CHUNK, LANES), lambda c, g: (ch(c), c0 + g))

    return wide, vec, cumt_spec, hp_spec, bc


def _ssd_fwd(xbc_c, dt_b, cum_b, cumt4, dskip_b, *, name):
    s = xbc_c.shape[0]
    nc = s // CHUNK

    def body(xs_ref, b_ref, c_ref, dt_ref, cum_ref, cumt_ref, dsk_ref, y_ref, hp_ref, state):
        c, g = pl.program_id(0), pl.program_id(1)

        @pl.when(c == 0)
        def _():
            state[pl.ds(g * PAIRS_PER_GROUP, PAIRS_PER_GROUP)] = jnp.zeros((PAIRS_PER_GROUP, LANES, N_STATE), F32)

        bb, cbm = b_ref[...].astype(BF16), c_ref[...].astype(BF16)
        cbv = lax.dot_general(cbm, bb, NT, preferred_element_type=F32)
        first = lax.broadcasted_iota(jnp.int32, (CHUNK, LANES), 1) < HEAD_DIM
        causal = _causal_mask()
        for pp in range(PAIRS_PER_GROUP):
            sl = slice(pp * LANES, (pp + 1) * LANES)
            xs, cum = xs_ref[:, sl], cum_ref[:, sl]
            xt = xs * dt_ref[:, sl]
            xtb = xt.astype(BF16)
            hp = state[g * PAIRS_PER_GROUP + pp]
            hp_ref[0, pp] = hp
            y = jnp.exp(cum) * lax.dot_general(cbm, hp.astype(BF16), NT, preferred_element_type=F32)
            for hh in range(2):
                lm = _decay(cum[:, hh * HEAD_DIM:hh * HEAD_DIM + 1], cumt_ref[0, 2 * pp + hh], causal)
                xm = jnp.where(first if hh == 0 else jnp.logical_not(first), xtb, jnp.zeros_like(xtb))
                y = y + jnp.dot((cbv * lm).astype(BF16), xm, preferred_element_type=F32)
            decs = jnp.exp(cum[CHUNK - 1:CHUNK, :] - cum)
            st = lax.dot_general((xt * decs).astype(BF16), bb, TN, preferred_element_type=F32)
            state[g * PAIRS_PER_GROUP + pp] = _state_row_scale(cumt_ref, pp) * hp + st
            y_ref[:, sl] = y + xs * dsk_ref[:, sl]

    wide, vec, cumt_spec, hp_spec, bc = _ssd_specs(nc, False)
    return pl.pallas_call(
        body, name=name, grid=(nc, N_GROUPS),
        in_specs=[wide, bc(B_CB0), bc(C_CB0), wide, wide, cumt_spec, vec],
        out_specs=(wide, hp_spec),
        out_shape=(_sds((s, D), F32), _sds((nc, N_PAIRS, LANES, N_STATE), F32)),
        scratch_shapes=[pltpu.VMEM((N_PAIRS, LANES, N_STATE), F32)],
        compiler_params=_cp("arbitrary", "arbitrary"),
    )(xbc_c, xbc_c, xbc_c, dt_b, cum_b, cumt4, dskip_b)


def _ssd_bwd(xbc_c, dt_b, cum_b, cumt4, hprev, dy, alog_b, dskip_b, *, name):
    s = xbc_c.shape[0]
    nc = s // CHUNK

    def body(xs_ref, b_ref, c_ref, dt_ref, cum_ref, cumt_ref, hp_ref, dy_ref, alog_ref, dsk_ref,
             dxs_ref, db_ref, dc_ref, ddt_ref, dalog_ref, ddsk_ref, dstate):
        c, g = pl.program_id(0), pl.program_id(1)
        pairs = pl.ds(g * PAIRS_PER_GROUP, PAIRS_PER_GROUP)

        @pl.when(c == 0)
        def _():
            dstate[pairs] = jnp.zeros((PAIRS_PER_GROUP, LANES, N_STATE), F32)
            dalog_ref[pairs] = jnp.zeros((PAIRS_PER_GROUP, 1, LANES), F32)
            ddsk_ref[pairs] = jnp.zeros((PAIRS_PER_GROUP, 1, LANES), F32)

        bb, cbm = b_ref[...].astype(BF16), c_ref[...].astype(BF16)
        cbv = lax.dot_general(cbm, bb, NT, preferred_element_type=F32)
        first = lax.broadcasted_iota(jnp.int32, (CHUNK, LANES), 1) < HEAD_DIM
        causal = _causal_mask()
        rown = lax.broadcasted_iota(jnp.int32, (LANES, 1), 0)
        d_b = jnp.zeros((CHUNK, N_STATE), F32)
        d_c = jnp.zeros((CHUNK, N_STATE), F32)
        for pp in range(PAIRS_PER_GROUP):
            sl = slice(pp * LANES, (pp + 1) * LANES)
            idx = g * PAIRS_PER_GROUP + pp
            xs, dtb, cum, dy = xs_ref[:, sl], dt_ref[:, sl], cum_ref[:, sl], dy_ref[:, sl]
            a_b = -jnp.exp(alog_ref[:, sl])
            xt = xs * dtb
            xtb = xt.astype(BF16)
            dyb = dy.astype(BF16)
            hp = hp_ref[0, pp]
            hpb = hp.astype(BF16)
            dh = dstate[idx]
            dhb = dh.astype(BF16)
            exp_cum = jnp.exp(cum)
            decs = jnp.exp(cum[CHUNK - 1:CHUNK, :] - cum)
            row_scale = _state_row_scale(cumt_ref, pp)
            xd = (xt * decs).astype(BF16)
            dye = (dy * exp_cum).astype(BF16)
            zero_b = jnp.zeros_like(xtb)

            h_next = row_scale * hp + lax.dot_general(xd, bb, TN, preferred_element_type=F32)
            y = exp_cum * lax.dot_general(cbm, hpb, NT, preferred_element_type=F32)
            dxt = decs * lax.dot_general(bb, dhb, NT, preferred_element_type=F32)
            dcb = jnp.zeros((CHUNK, CHUNK), F32)
            for hh in range(2):
                mask = first if hh == 0 else jnp.logical_not(first)
                lm = _decay(cum[:, hh * HEAD_DIM:hh * HEAD_DIM + 1], cumt_ref[0, 2 * pp + hh], causal)
                m = (cbv * lm).astype(BF16)
                y = y + jnp.dot(m, jnp.where(mask, xtb, zero_b), preferred_element_type=F32)
                dxt = dxt + jnp.where(mask, lax.dot_general(m, dyb, TN, preferred_element_type=F32), 0.0)
                dm = lax.dot_general(jnp.where(mask, dyb, zero_b), xtb, NT, preferred_element_type=F32)
                dcb = dcb + dm * lm
            dcbb = dcb.astype(BF16)
            d_c = d_c + jnp.dot(dcbb, bb, preferred_element_type=F32) + jnp.dot(dye, hpb, preferred_element_type=F32)
            d_b = (d_b + lax.dot_general(dcbb, cbm, TN, preferred_element_type=F32)
                   + jnp.dot(xd, dhb, preferred_element_type=F32))
            dstate[idx] = row_scale * dh + lax.dot_general(dye, cbm, TN, preferred_element_type=F32)

            d_cum = _pair_sum(dyb.astype(F32) * y - dxt * xtb.astype(F32), first)
            e = jnp.sum(dh * h_next, axis=1, keepdims=True)
            t0 = jnp.sum(jnp.where(rown < HEAD_DIM, e, 0.0), axis=0, keepdims=True)
            t1 = jnp.sum(jnp.where(rown < HEAD_DIM, 0.0, e), axis=0, keepdims=True)
            d_da = _split3_dot(_tri(False), d_cum) + jnp.where(first[0:1, :], t0, t1)
            ddt_ref[:, sl] = a_b * d_da + _pair_sum(dxt * xs, first)
            dxs_ref[:, sl] = dxt * dtb + dy * dsk_ref[:, sl]
            dalog_ref[idx] += jnp.sum(d_da * dtb * a_b, axis=0, keepdims=True)
            ddsk_ref[idx] += jnp.sum(_pair_sum(dy * xs, first), axis=0, keepdims=True)
        db_ref[...] = d_b
        dc_ref[...] = d_c

    wide, vec, cumt_spec, hp_spec, bc = _ssd_specs(nc, True)
    acc = pl.BlockSpec((N_PAIRS, 1, LANES), lambda c, g: (0, 0, 0))
    return pl.pallas_call(
        body, name=name, grid=(nc, N_GROUPS),
        in_specs=[wide, bc(B_CB0), bc(C_CB0), wide, wide, cumt_spec, hp_spec, wide, vec, vec],
        out_specs=(wide, bc(0), bc(0), wide, acc, acc),
        out_shape=(_sds((s, D), F32), _sds((s, N_GROUPS * N_STATE), F32), _sds((s, N_GROUPS * N_STATE), F32),
                   _sds((s, D), F32), _sds((N_PAIRS, 1, LANES), F32), _sds((N_PAIRS, 1, LANES), F32)),
        scratch_shapes=[pltpu.VMEM((N_PAIRS, LANES, N_STATE), F32)],
        compiler_params=_cp("arbitrary", "arbitrary"),
    )(xbc_c, xbc_c, xbc_c, dt_b, cum_b, cumt4, hprev, dy, alog_b, dskip_b)


def _lane_bcast(v):
    return jnp.repeat(v, HEAD_DIM, axis=1)


def _split_w_in(w_in):
    dt0 = D + D_XBC
    return (jnp.concatenate([w_in[:, :dt0], w_in[:, dt0 + N_HEADS:]], axis=1),
            jnp.pad(w_in[:, dt0:dt0 + N_HEADS], ((0, 0), (0, DT_W - N_HEADS))))


def _prep_small(conv_w, conv_b, dt_bias, a_log, d_skip):
    pad = DT_W - N_HEADS
    return dict(
        conv_w=conv_w, conv_b=conv_b.reshape(1, D_XBC),
        dt_bias=jnp.pad(dt_bias.reshape(1, N_HEADS), ((0, 0), (0, pad))),
        a_log=jnp.pad(a_log.reshape(1, N_HEADS), ((0, 0), (0, pad))),
        alog_b=_lane_bcast(a_log.reshape(1, N_HEADS)), dskip_b=_lane_bcast(d_skip.reshape(1, N_HEADS)))


def _prep_layer(w_in, conv_w, conv_b, dt_bias, a_log, d_skip, w_out, w_q, w_k, w_v, w_o, w_gate, w_up, w_down):
    p = _prep_small(conv_w, conv_b, dt_bias, a_log, d_skip)
    p["w_main"], p["w_dt"] = _split_w_in(w_in)
    p.update(w_out=w_out, w_q=w_q, w_k=w_k, w_v=w_v, w_o=w_o, w_gu=jnp.concatenate([w_gate, w_up], axis=1), w_down=w_down)
    return p


def _layer_fwd(h0, memn, p, g, li, comm=None):
    s = h0.shape[0]
    nc = s // CHUNK
    n = f"l{li}_"
    if comm is not None:
        comm.before_layer(li, p)
    hn = _norm_fwd(h0, g["norm_mix"], name=n + "norm_mix")
    if comm is None:
        proj = _mm(hn, p["w_main"], name=n + "proj")
    else:
        proj, got = _mm(hn, p["w_main"], name=n + "proj", side=comm.proj_side(li))
        comm.after_proj(li, p, got)
    dt_raw = _mm(hn, p["w_dt"], name=n + "proj_dt")
    xbc_c = _conv_silu_fwd(proj, p["conv_w"], p["conv_b"], name=n + "conv")
    dt_b, cum_b, cumt = _dt_fwd(dt_raw, p["dt_bias"], p["a_log"], name=n + "dt")
    cumt4 = cumt[:, :N_HEADS, :].reshape(nc, N_HEADS, 1, CHUNK)
    y, hprev = _ssd_fwd(xbc_c, dt_b, cum_b, cumt4, p["dskip_b"], name=n + "ssd")
    y_ssd = _gated_norm_fwd(y, proj, g["ssd_norm"], name=n + "gnorm")
    y_sc = _sc_fwd(proj, p["sc_conv_w"], g["sc_norm"], name=n + "sc")
    cat = jnp.concatenate([y_ssd, y_sc], axis=1)
    h1 = _mm(cat, p["w_out"], res=h0, bk=2048, name=n + "out")
    hx = _norm_fwd(h1, g["norm_xa"], name=n + "norm_xa")
    q = _mm(hx, p["w_q"], out_dtype=BF16, name=n + "q")
    k = _mm(memn, p["w_k"], out_dtype=BF16, name=n + "k")
    v = _mm(memn, p["w_v"], out_dtype=BF16, name=n + "v")
    o = _attn_fwd(q, k, v, name=n + "attn")
    h2 = _mm(o, p["w_o"], res=h1, name=n + "o")
    hf = _norm_fwd(h2, g["norm_ffn"], name=n + "norm_ffn")
    if comm is None:
        gu = _mm(hf, p["w_gu"], name=n + "gu")
    else:
        gu, got = _mm(hf, p["w_gu"], name=n + "gu", side=comm.gu_side(li))
        comm.after_gu(li, p, got)
    act = _swiglu_fwd(gu, name=n + "swiglu")
    h3 = _mm(act, p["w_down"], res=h2, bk=1408, name=n + "down")
    saved = dict(h0=h0, hn=hn, proj=proj, dt_raw=dt_raw, xbc_c=xbc_c, dt_b=dt_b, cum_b=cum_b, cumt4=cumt4, hprev=hprev,
                 y=y, cat=cat, h1=h1, hx=hx, q=q, k=k, v=v, o=o, h2=h2, hf=hf, gu=gu, act=act)
    return h3, saved


def _layer_bwd(dh, dhb, dmemn, memn, p, g, sv, li, comm=None):
    n = f"l{li}b_"
    gr = {}

    def hosted(names, *args, **kw):
        if comm is None:
            return _mm(*args, **kw)
        out, got = _mm(*args, side=comm.grad_side(names, gr), **kw)
        comm.after_grads(li, names, got)
        return out

    dact = _mm(dhb, p["w_down"], tb=True, bn=512, name=n + "dact")
    gr["w_down"] = _mm(sv["act"], dhb, ta=True, bm=512, bk=2048, out_dtype=BF16, name=n + "dw_down")
    dgu = _swiglu_bwd(sv["gu"], dact, name=n + "swiglu")
    dw_gu = _mm(sv["hf"], dgu, ta=True, bk=2048, out_dtype=BF16, name=n + "dw_gu")
    gr["w_gate"], gr["w_up"] = dw_gu[:, :D_FF], dw_gu[:, D_FF:]
    dhf = hosted(("w_down", "w_gate"), dgu, p["w_gu"], tb=True, bk=2816, name=n + "dhf")
    dh, dhb, gr["norm_ffn"] = _norm_bwd(sv["h2"], g["norm_ffn"], dhf, dh, name=n + "norm_ffn")
    do = _mm(dhb, p["w_o"], tb=True, out_dtype=BF16, name=n + "do")
    gr["w_o"] = _mm(sv["o"], dhb, ta=True, bk=2048, out_dtype=BF16, name=n + "dw_o")
    dq, dk, dv = _attn_bwd(sv["q"], sv["k"], sv["v"], do, name=n + "attn")
    gr["w_q"] = _mm(sv["hx"], dq, ta=True, bk=2048, out_dtype=BF16, name=n + "dw_q")
    gr["w_k"] = _mm(memn, dk, ta=True, out_dtype=BF16, name=n + "dw_k")
    gr["w_v"] = _mm(memn, dv, ta=True, out_dtype=BF16, name=n + "dw_v")
    dmemn = _mm(dk, p["w_k"], tb=True, res=dmemn, name=n + "dmem_k")
    dmemn = _mm(dv, p["w_v"], tb=True, res=dmemn, name=n + "dmem_v")
    dhx = _mm(dq, p["w_q"], tb=True, name=n + "dhx")
    dh, dhb, gr["norm_xa"] = _norm_bwd(sv["h1"], g["norm_xa"], dhx, dh, name=n + "norm_xa")
    dcat = _mm(dhb, p["w_out"], tb=True, name=n + "dcat")
    gr["w_out"] = _mm(sv["cat"], dhb, ta=True, bk=2048, out_dtype=BF16, name=n + "dw_out")
    du, dgb, dgc, dsc_w, gr["sc_norm"] = _sc_bwd(sv["proj"], dcat, p["sc_conv_w"], g["sc_norm"], name=n + "sc")
    gr["sc_conv_w"] = dsc_w.reshape(3, D)
    dy, dz, gr["ssd_norm"] = _gated_norm_bwd(sv["y"], sv["proj"], dcat, g["ssd_norm"], name=n + "gnorm")
    dxs, d_b, d_c, ddt_b, dalog, ddsk = _ssd_bwd(sv["xbc_c"], sv["dt_b"], sv["cum_b"], sv["cumt4"], sv["hprev"], dy,
                                                 p["alog_b"], p["dskip_b"], name=n + "ssd")
    gr["a_log"] = dalog.reshape(N_HEADS, HEAD_DIM)[:, 0]
    gr["d_skip"] = ddsk.reshape(N_HEADS, HEAD_DIM)[:, 0]
    dxbc_c = jnp.concatenate([dxs, d_b, d_c], axis=1)
    dxbc, dconv_w, dconv_b = _conv_silu_bwd(sv["proj"], dxbc_c, p["conv_w"], p["conv_b"], name=n + "conv")
    gr["ssd_conv_w"] = dconv_w.reshape(4, D_XBC)
    gr["ssd_conv_b"] = dconv_b.reshape(D_XBC)
    ddt_raw, ddt_bias = _dt_bwd(sv["dt_raw"], p["dt_bias"], ddt_b, name=n + "dt")
    gr["dt_bias"] = ddt_bias[0, :N_HEADS]
    dproj = jnp.concatenate([dz, dxbc, du, dgb, dgc], axis=1)
    dw_main = hosted(("w_up", "w_out", "w_q", "w_k", "w_v", "w_o"), sv["hn"], dproj, ta=True, bk=2048, out_dtype=BF16,
                     name=n + "dw_main")
    dw_dt = _mm(sv["hn"], ddt_raw, ta=True, bk=2048, out_dtype=BF16, name=n + "dw_dt")
    dt0 = D + D_XBC
    gr["w_in"] = jnp.concatenate([dw_main[:, :dt0], dw_dt[:, :N_HEADS], dw_main[:, dt0:]], axis=1)
    dhn = hosted(("w_in",), dproj, p["w_main"], tb=True, bk=2816, name=n + "dhn")
    dhn = _mm(ddt_raw, p["w_dt"], tb=True, res=dhn, name=n + "dhn_dt")
    dh, dhb, gr["norm_mix"] = _norm_bwd(sv["h0"], g["norm_mix"], dhn, dh, name=n + "norm_mix")
    return dh, dhb, dmemn, gr


def _local_step(x, mem, tgt, layers, gains, mem_norm, norm_final, comm=None):
    depth = len(layers)
    memn_f = _norm_fwd(mem, mem_norm, out_dtype=F32, name="mem_norm")
    memn = memn_f.astype(BF16)
    h = x
    saved = []
    for li in range(depth):
        h, sv = _layer_fwd(h, memn, layers[li], gains[li], li, comm)
        saved.append(sv)
    loss, dh, dhb, d_final = _loss_head(h, norm_final, tgt, name="loss_head")
    dmemn = jnp.zeros(mem.shape, F32)
    grads = [None] * depth
    for li in reversed(range(depth)):
        dh, dhb, dmemn, grads[li] = _layer_bwd(dh, dhb, dmemn, memn, layers[li], gains[li], saved[li], li, comm)
    _, _, d_mem_norm = _norm_bwd(mem, mem_norm, dmemn, None, name="mem_norm_b")
    return loss, dh, grads, d_mem_norm, d_final


ADAMW_ROWS = (64, 32, 16, 8)


def _adamw(parts, w, m, v, *, name):
    depth = len(parts)
    r, c_dim = parts[0].shape[1:]
    assert w.shape == (depth * r, c_dim), (name, w.shape, parts[0].shape)
    tr = next(t for t in ADAMW_ROWS if r % t == 0)
    nb = r // tr

    def body(*refs):
        p_refs = refs[:depth]
        w_ref, m_ref, v_ref, g_ref, d_ref, nm_ref, nv_ref = refs[depth:]
        for l in range(depth):
            @pl.when(pl.program_id(0) == l)
            def _(l=l):
                g = p_refs[l][0].astype(F32)
                for s in range(1, N_DEV):
                    g = g + p_refs[l][s].astype(F32)
                g_ref[...] = g

        g = g_ref[...]
        m2 = ADAM_B1 * m_ref[...] + (1.0 - ADAM_B1) * g
        v2 = ADAM_B2 * v_ref[...] + (1.0 - ADAM_B2) * (g * g)
        m_hat = m2 / (1.0 - ADAM_B1 ** ADAM_STEP)
        v_hat = v2 / (1.0 - ADAM_B2 ** ADAM_STEP)
        d_ref[...] = -ADAM_LR * (m_hat / (jnp.sqrt(v_hat) + ADAM_EPS) + ADAM_WD * w_ref[...])
        nm_ref[...] = m2
        nv_ref[...] = v2

    def part_spec(l):
        return pl.BlockSpec((N_DEV, tr, c_dim),
                            lambda lay, i: (0, jnp.where(lay == l, i, jnp.where(lay < l, 0, nb - 1)), 0))

    row = pl.BlockSpec((tr, c_dim), lambda lay, i: (lay * nb + i, 0))
    out = _sds((depth * r, c_dim), F32)
    return pl.pallas_call(
        body, name=name, grid=(depth, nb), in_specs=[part_spec(l) for l in range(depth)] + [row, row, row],
        out_specs=(row, row, row, row), out_shape=(out, out, out, out), compiler_params=_cp("arbitrary", "arbitrary"),
    )(*parts, w, m, v)


BIG = ("w_in", "w_out", "w_q", "w_k", "w_v", "w_o", "w_gate", "w_up", "w_down")
COL_SHARDED = ("w_in", "w_o", "w_gate", "w_up")
PROJ_HOSTED = ("w_out", "w_q", "w_k", "w_v", "w_o", "w_gate", "w_up")
SMALL_REPL = ("norm_mix", "ssd_conv_b", "dt_bias", "a_log", "d_skip", "ssd_norm", "sc_norm", "mem_norm", "norm_xa",
              "norm_ffn", "norm_final")
SMALL_SHARDED = ("ssd_conv_w", "sc_conv_w")
WEIGHTS = ("norm_mix", "w_in", "ssd_conv_w", "ssd_conv_b", "dt_bias", "a_log", "d_skip", "ssd_norm", "sc_conv_w",
           "sc_norm", "w_out", "mem_norm", "norm_xa", "w_q", "w_k", "w_v", "w_o", "norm_ffn", "w_gate", "w_up",
           "w_down", "norm_final")


def _assemble(got, col_sharded):
    l, _, r, c_dim = got.shape
    if col_sharded:
        return jnp.transpose(got, (0, 2, 1, 3)).reshape(l, r, N_DEV * c_dim)
    return got.reshape(l, N_DEV * r, c_dim)


def _to_pieces(full, col_sharded):
    rr, cc = full.shape
    if col_sharded:
        return jnp.transpose(full.reshape(rr, N_DEV, cc // N_DEV), (1, 0, 2))
    return full.reshape(N_DEV, rr // N_DEV, cc)


class _Comm:
    def __init__(self, w, first_w_in):
        self.w = w
        self.depth = w["w_in"].shape[0]
        self.next_w_in = first_w_in
        self.got = {nm: [None] * self.depth for nm in BIG}

    def _shards(self, names, li):
        return [self.w[nm][li:li + 1].astype(BF16) for nm in names]

    def before_layer(self, li, p):
        p["w_main"], p["w_dt"] = _split_w_in(self.next_w_in)

    def proj_side(self, li):
        return _ag_side(self._shards(PROJ_HOSTED, li))

    def after_proj(self, li, p, got):
        full = {nm: _assemble(g, nm in COL_SHARDED)[0] for nm, g in zip(PROJ_HOSTED, got)}
        for nm in ("w_out", "w_q", "w_k", "w_v", "w_o"):
            p[nm] = full[nm]
        p["w_gu"] = jnp.concatenate([full["w_gate"], full["w_up"]], axis=1)

    def gu_side(self, li):
        shards = self._shards(("w_down",), li)
        if li + 1 < self.depth:
            shards += self._shards(("w_in",), li + 1)
        return _ag_side(shards)

    def after_gu(self, li, p, got):
        p["w_down"] = _assemble(got[0], False)[0]
        if li + 1 < self.depth:
            self.next_w_in = _assemble(got[1], True)[0]

    def grad_side(self, names, gr):
        return _rs_side([_to_pieces(gr[nm], nm in COL_SHARDED) for nm in names])

    def after_grads(self, li, names, got):
        for nm, g in zip(names, got):
            self.got[nm][li] = g


def _pack(arrs, names):
    flat = jnp.concatenate([arrs[nm].reshape(-1).astype(F32) for nm in names])
    rows = -(-flat.shape[0] // (TB * LANES)) * TB
    return jnp.pad(flat, (0, rows * LANES - flat.shape[0])).reshape(rows, LANES)


def _unpack(packed, shapes, names):
    flat = packed.reshape(-1)
    out, off = {}, 0
    for nm in names:
        size = 1
        for dim in shapes[nm]:
            size *= dim
        out[nm] = flat[off:off + size].reshape(shapes[nm])
        off += size
    return out


def kernel(x, mem, norm_mix, w_in, ssd_conv_w, ssd_conv_b, dt_bias, a_log, d_skip, ssd_norm, sc_conv_w, sc_norm, w_out, mem_norm, norm_xa, w_q, w_k, w_v, w_o, norm_ffn, w_gate, w_up, w_down, norm_final, loss_target, m_norm_mix, m_w_in, m_ssd_conv_w, m_ssd_conv_b, m_dt_bias, m_a_log, m_d_skip, m_ssd_norm, m_sc_conv_w, m_sc_norm, m_w_out, m_mem_norm, m_norm_xa, m_w_q, m_w_k, m_w_v, m_w_o, m_norm_ffn, m_w_gate, m_w_up, m_w_down, m_norm_final, v_norm_mix, v_w_in, v_ssd_conv_w, v_ssd_conv_b, v_dt_bias, v_a_log, v_d_skip, v_ssd_norm, v_sc_conv_w, v_sc_norm, v_w_out, v_mem_norm, v_norm_xa, v_w_q, v_w_k, v_w_v, v_w_o, v_norm_ffn, v_w_gate, v_w_up, v_w_down, v_norm_final):
    w = dict(norm_mix=norm_mix, w_in=w_in, ssd_conv_w=ssd_conv_w, ssd_conv_b=ssd_conv_b, dt_bias=dt_bias, a_log=a_log,
             d_skip=d_skip, ssd_norm=ssd_norm, sc_conv_w=sc_conv_w, sc_norm=sc_norm, w_out=w_out, mem_norm=mem_norm,
             norm_xa=norm_xa, w_q=w_q, w_k=w_k, w_v=w_v, w_o=w_o, norm_ffn=norm_ffn, w_gate=w_gate, w_up=w_up,
             w_down=w_down, norm_final=norm_final)
    mom = dict(norm_mix=m_norm_mix, w_in=m_w_in, ssd_conv_w=m_ssd_conv_w, ssd_conv_b=m_ssd_conv_b, dt_bias=m_dt_bias,
               a_log=m_a_log, d_skip=m_d_skip, ssd_norm=m_ssd_norm, sc_conv_w=m_sc_conv_w, sc_norm=m_sc_norm,
               w_out=m_w_out, mem_norm=m_mem_norm, norm_xa=m_norm_xa, w_q=m_w_q, w_k=m_w_k, w_v=m_w_v, w_o=m_w_o,
               norm_ffn=m_norm_ffn, w_gate=m_w_gate, w_up=m_w_up, w_down=m_w_down, norm_final=m_norm_final)
    var = dict(norm_mix=v_norm_mix, w_in=v_w_in, ssd_conv_w=v_ssd_conv_w, ssd_conv_b=v_ssd_conv_b, dt_bias=v_dt_bias,
               a_log=v_a_log, d_skip=v_d_skip, ssd_norm=v_ssd_norm, sc_conv_w=v_sc_conv_w, sc_norm=v_sc_norm,
               w_out=v_w_out, mem_norm=v_mem_norm, norm_xa=v_norm_xa, w_q=v_w_q, w_k=v_w_k, w_v=v_w_v, w_o=v_w_o,
               norm_ffn=v_norm_ffn, w_gate=v_w_gate, w_up=v_w_up, w_down=v_w_down, norm_final=v_norm_final)
    depth = w_in.shape[0]
    my = 4 * lax.axis_index("x") + 2 * lax.axis_index("y") + lax.axis_index("c")

    got = _run_side(_ag_side([w_in[0:1].astype(BF16), ssd_conv_w, sc_conv_w]), name="ag_first")
    comm = _Comm(w, _assemble(got[0], True)[0])
    conv_full = {"ssd_conv_w": _assemble(got[1], True), "sc_conv_w": _assemble(got[2], True)}
    layers, gains = [], []
    for li in range(depth):
        p = _prep_small(conv_full["ssd_conv_w"][li], ssd_conv_b[li], dt_bias[li], a_log[li], d_skip[li])
        p["sc_conv_w"] = conv_full["sc_conv_w"][li]
        layers.append(p)
        gains.append({nm: w[nm][li].reshape(1, D) for nm in ("norm_mix", "ssd_norm", "sc_norm", "norm_xa", "norm_ffn")})

    loss_v, grad_x, grads, d_mem_norm, d_final = _local_step(
        x[0], mem[0], loss_target[0], layers, gains, mem_norm.reshape(1, D), norm_final.reshape(1, D), comm)
    loss = lax.psum(loss_v[0, 0], ("x", "y", "c"))

    outs = {}
    for nm in BIG:
        shp = w[nm].shape
        flat = (shp[0] * shp[1], shp[2])
        res = _adamw(comm.got[nm], w[nm].reshape(flat), mom[nm].reshape(flat), var[nm].reshape(flat), name="adamw_" + nm)
        outs[nm] = tuple(t.reshape(shp) for t in res)

    small = SMALL_REPL + SMALL_SHARDED
    gsmall = dict(mem_norm=d_mem_norm.reshape(D), norm_final=d_final.reshape(D))
    for nm in ("norm_mix", "ssd_norm", "sc_norm", "norm_xa", "norm_ffn"):
        gsmall[nm] = jnp.stack([gr[nm].reshape(D) for gr in grads])
    for nm in ("ssd_conv_b", "dt_bias", "a_log", "d_skip", "ssd_conv_w", "sc_conv_w"):
        gsmall[nm] = jnp.stack([gr[nm] for gr in grads])
    packed_g = _pack(gsmall, small)
    all_g = _run_side(_ag_side([packed_g[None]]), name="ag_small_grads")[0][0]

    def put_shard(arrs):
        loc = {nm: arrs[nm] for nm in SMALL_REPL}
        for nm in SMALL_SHARDED:
            cs = arrs[nm].shape[-1]
            loc[nm] = lax.dynamic_update_slice_in_dim(jnp.zeros(gsmall[nm].shape, F32), arrs[nm], my * cs, axis=2)
        return _pack(loc, small)

    res = _adamw([all_g], put_shard(w), put_shard(mom), put_shard(var), name="adamw_small")
    shapes = {nm: gsmall[nm].shape for nm in small}
    for idx in range(4):
        un = _unpack(res[idx], shapes, small)
        for nm in SMALL_REPL:
            outs.setdefault(nm, [None] * 4)[idx] = un[nm]
        for nm in SMALL_SHARDED:
            cs = w[nm].shape[-1]
            outs.setdefault(nm, [None] * 4)[idx] = lax.dynamic_slice_in_dim(un[nm], my * cs, cs, axis=2)

    return (loss, grad_x[None], *[outs[nm][0] for nm in WEIGHTS], *[outs[nm][1] for nm in WEIGHTS],
            *[outs[nm][2] for nm in WEIGHTS], *[outs[nm][3] for nm in WEIGHTS])
```

```python
import functools

import jax
import jax.numpy as jnp
from jax import lax
from jax.experimental import pallas as pl
from jax.experimental.pallas import tpu as pltpu

F32 = jnp.float32
BF16 = jnp.bfloat16

D = 2048
HEAD_DIM = 64
N_HEADS = 32
N_GROUPS = 4
N_STATE = 128
CHUNK = 256
D_XBC = D + 2 * N_GROUPS * N_STATE
SSD_GROUP_W = D // N_GROUPS
SC_GROUP_W = 128
XA_HEADS = 4
XA_HD = 128
D_XA = XA_HEADS * XA_HD
D_FF = 5632
NORM_EPS = 1e-5
PROJ_W = 11264
DT_W = 128
N_DEV = 8

ADAM_LR = 0.001
ADAM_B1 = 0.9
ADAM_B2 = 0.999
ADAM_EPS = 1e-08
ADAM_WD = 0.01
ADAM_STEP = 10

TB = CHUNK
HALO = 8
LANES = 128
VMEM_LIMIT = 56 * 1024 * 1024

NT = (((1,), (1,)), ((), ()))
TN = (((0,), (0,)), ((), ()))
MESH = pl.DeviceIdType.MESH


def _cp(*sem):
    return pltpu.CompilerParams(dimension_semantics=sem, vmem_limit_bytes=VMEM_LIMIT)


def _sds(shape, dtype):
    return jax.ShapeDtypeStruct(shape, dtype)


def _sigmoid(x):
    return 1.0 / (1.0 + jnp.exp(-x))


def _split3_dot(t_bf16, x):
    hi = x.astype(BF16)
    r1 = x - hi.astype(F32)
    mid = r1.astype(BF16)
    lo = (r1 - mid.astype(F32)).astype(BF16)
    out = jnp.dot(t_bf16, hi, preferred_element_type=F32)
    out = out + jnp.dot(t_bf16, mid, preferred_element_type=F32)
    return out + jnp.dot(t_bf16, lo, preferred_element_type=F32)


def _split3_dot_r(x, t_bf16):
    hi = x.astype(BF16)
    r1 = x - hi.astype(F32)
    mid = r1.astype(BF16)
    lo = (r1 - mid.astype(F32)).astype(BF16)
    out = jnp.dot(hi, t_bf16, preferred_element_type=F32)
    out = out + jnp.dot(mid, t_bf16, preferred_element_type=F32)
    return out + jnp.dot(lo, t_bf16, preferred_element_type=F32)


def _group_bcast(stat_fn, v, gw):
    pieces = []
    for g in range(v.shape[1] // gw):
        vs = v[:, g * gw:(g + 1) * gw]
        pieces.append(jnp.broadcast_to(stat_fn(vs), vs.shape))
    return jnp.concatenate(pieces, axis=1) if len(pieces) > 1 else pieces[0]


def _group_rstd(v, gw):
    return _group_bcast(lambda s: lax.rsqrt(jnp.mean(s * s, axis=1, keepdims=True) + NORM_EPS), v, gw)


def _group_mean(v, gw):
    return _group_bcast(lambda s: jnp.mean(s, axis=1, keepdims=True), v, gw)


def _pair_sum(d, first):
    s0 = jnp.sum(jnp.where(first, d, 0.0), axis=1, keepdims=True)
    s1 = jnp.sum(jnp.where(first, 0.0, d), axis=1, keepdims=True)
    return jnp.where(first, s0, s1)


class _Side:
    def __init__(self, arrays, out_shapes, n_remote, n_local, start, finish):
        self.arrays, self.out_shapes = tuple(arrays), tuple(out_shapes)
        self.n_remote, self.n_local = n_remote, n_local
        self.start, self.finish = start, finish

    def scratch(self):
        return [pltpu.SemaphoreType.DMA((self.n_remote,)), pltpu.SemaphoreType.DMA((self.n_remote,)),
                pltpu.SemaphoreType.DMA((self.n_local,))]


def _ag_side(shards):
    n = len(shards)

    def plan(x_refs, out_refs, send, recv, local, starting=False):
        x, y, c = lax.axis_index("x"), lax.axis_index("y"), lax.axis_index("c")
        me, sibling = (x, y, c), (x, y, 1 - c)
        chips = [(1 - x, y), (x, 1 - y), (1 - x, 1 - y)]
        jobs = []
        for a in range(n):
            def rows(px, py, pc, out=out_refs[a]):
                return out.at[:, 4 * px + 2 * py + pc]

            def copy(k, block, to, src=None, a=a, rows=rows):
                return pltpu.make_async_remote_copy(
                    src_ref=rows(*block) if src is None else src, dst_ref=rows(*block),
                    send_sem=send.at[7 * a + k], recv_sem=recv.at[7 * a + k], device_id=to, device_id_type=MESH)

            mine = pltpu.make_async_copy(x_refs[a], rows(*me), local.at[a])
            first = [copy(0, me, sibling, src=x_refs[a])]
            first += [copy(1 + j, me, (*chip, c), src=x_refs[a]) for j, chip in enumerate(chips)]
            if starting:
                jobs.append((mine, first))
                continue
            passed = [copy(4 + j, (*chip, c), sibling) for j, chip in enumerate(chips)]
            arrive = [copy(1 + j, (*chip, c), me) for j, chip in enumerate(chips)]
            late = [copy(0, sibling, me)] + [copy(4 + j, (*chip, 1 - c), me) for j, chip in enumerate(chips)]
            jobs.append((mine, first, passed, arrive, late))
        return jobs

    def start(*refs):
        for mine, first in plan(*refs, starting=True):
            mine.start()
            for cp in first:
                cp.start()

    def finish(*refs):
        jobs = plan(*refs)
        for j in range(3):
            for _, _, passed, arrive, _ in jobs:
                arrive[j].wait_recv()
                passed[j].start()
        for mine, first, passed, _, late in jobs:
            for cp in late:
                cp.wait_recv()
            for cp in first + passed:
                cp.wait_send()
            mine.wait()

    outs = [_sds((s.shape[0], N_DEV) + s.shape[1:], s.dtype) for s in shards]
    return _Side(shards, outs, 7 * n, n, start, finish)


def _flip(k, x, y, c):
    return (1 - x if k & 4 else x, 1 - y if k & 2 else y, 1 - c if k & 1 else c)


def _rs_side(pieces):
    n = len(pieces)

    def plan(g_refs, out_refs, send, recv, local):
        x, y, c = lax.axis_index("x"), lax.axis_index("y"), lax.axis_index("c")
        me = 4 * x + 2 * y + c
        jobs = []
        for a in range(n):
            mine = pltpu.make_async_copy(g_refs[a].at[me], out_refs[a].at[me], local.at[a])
            copies = []
            for k in range(1, N_DEV):
                px, py, pc = _flip(k, x, y, c)
                copies.append(pltpu.make_async_remote_copy(
                    src_ref=g_refs[a].at[4 * px + 2 * py + pc], dst_ref=out_refs[a].at[me],
                    send_sem=send.at[7 * a + k - 1], recv_sem=recv.at[7 * a + k - 1],
                    device_id=(px, py, pc), device_id_type=MESH))
            jobs.append((mine, copies))
        return jobs

    def start(*refs):
        for mine, copies in plan(*refs):
            mine.start()
            for cp in copies:
                cp.start()

    def finish(*refs):
        for mine, copies in plan(*refs):
            for cp in copies:
                cp.wait_recv()
            for cp in copies:
                cp.wait_send()
            mine.wait()

    return _Side(pieces, [_sds(p.shape, p.dtype) for p in pieces], 7 * n, n, start, finish)


def _run_side(side, *, name):
    n_in, n_out = len(side.arrays), len(side.out_shapes)

    def body(*refs):
        parts = (refs[:n_in], refs[n_in:n_in + n_out]) + tuple(refs[n_in + n_out:])
        side.start(*parts)
        side.finish(*parts)

    hbm = pl.BlockSpec(memory_space=pl.ANY)
    return pl.pallas_call(
        body, name=name, out_shape=side.out_shapes, in_specs=[hbm] * n_in, out_specs=tuple([hbm] * n_out),
        scratch_shapes=side.scratch(),
    )(*side.arrays)


def _mm(a, b, *, name, ta=False, tb=False, res=None, out_dtype=F32, bm=1024, bn=1024, bk=None, j_outer=False,
        side=None):
    if ta:
        k_dim, m_dim = a.shape
    else:
        m_dim, k_dim = a.shape
    if tb:
        n_dim, kb = b.shape
    else:
        kb, n_dim = b.shape
    assert k_dim == kb, (a.shape, b.shape)
    bm, bn = min(bm, m_dim), min(bn, n_dim)
    bk = k_dim if bk is None else min(bk, k_dim)
    assert m_dim % bm == 0 and n_dim % bn == 0 and k_dim % bk == 0, (name, a.shape, b.shape, bm, bn, bk)
    ni, nj, nk = m_dim // bm, n_dim // bn, k_dim // bk
    grid = (nj, ni, nk) if j_outer else (ni, nj, nk)
    dn = (((0,) if ta else (1,), (1,) if tb else (0,)), ((), ()))
    has_res = res is not None
    n_main_in = 2 + has_res
    n_side_in = len(side.arrays) if side else 0
    n_side_out = len(side.out_shapes) if side else 0

    def body(*refs):
        a_ref, b_ref = refs[0], refs[1]
        r_ref = refs[2] if has_res else None
        o_ref = refs[n_main_in + n_side_in]
        scratch = refs[n_main_in + n_side_in + 1 + n_side_out:]
        if side:
            side_refs = (refs[n_main_in:n_main_in + n_side_in],
                         refs[n_main_in + n_side_in + 1:n_main_in + n_side_in + 1 + n_side_out]) + tuple(scratch[-3:])
            step = (pl.program_id(0) * grid[1] + pl.program_id(1)) * grid[2] + pl.program_id(2)

            @pl.when(step == 0)
            def _():
                side.start(*side_refs)

        p = lax.dot_general(a_ref[...].astype(BF16), b_ref[...].astype(BF16), dn, preferred_element_type=F32)
        if nk == 1:
            if has_res:
                p = p + r_ref[...]
            o_ref[...] = p.astype(o_ref.dtype)
        else:
            acc_ref = scratch[0]
            k = pl.program_id(2)

            @pl.when(k == 0)
            def _():
                acc_ref[...] = p

            @pl.when(k > 0)
            def _():
                acc_ref[...] += p

            @pl.when(k == nk - 1)
            def _():
                r = acc_ref[...]
                if has_res:
                    r = r + r_ref[...]
                o_ref[...] = r.astype(o_ref.dtype)

        if side:
            @pl.when(step == grid[0] * grid[1] * grid[2] - 1)
            def _():
                side.finish(*side_refs)

    def spec(shape, index):
        if j_outer:
            return pl.BlockSpec(shape, lambda j, i, k: index(i, j, k))
        return pl.BlockSpec(shape, index)

    a_spec = spec((bk, bm), lambda i, j, k: (k, i)) if ta else spec((bm, bk), lambda i, j, k: (i, k))
    b_spec = spec((bn, bk), lambda i, j, k: (j, k)) if tb else spec((bk, bn), lambda i, j, k: (k, j))
    o_spec = spec((bm, bn), lambda i, j, k: (i, j))
    hbm = pl.BlockSpec(memory_space=pl.ANY)
    in_specs = [a_spec, b_spec] + ([o_spec] if has_res else []) + [hbm] * n_side_in
    args = (a, b) + ((res,) if has_res else ()) + (side.arrays if side else ())
    scratch_shapes = ([pltpu.VMEM((bm, bn), F32)] if nk > 1 else []) + (side.scratch() if side else [])
    out_main = _sds((m_dim, n_dim), out_dtype)
    if not side:
        return pl.pallas_call(
            body, name=name, grid=grid, in_specs=in_specs, out_specs=o_spec, out_shape=out_main,
            scratch_shapes=scratch_shapes, compiler_params=_cp("parallel", "parallel", "arbitrary"),
        )(*args)
    outs = pl.pallas_call(
        body, name=name, grid=grid, in_specs=in_specs, out_specs=(o_spec,) + tuple([hbm] * n_side_out),
        out_shape=(out_main,) + side.out_shapes, scratch_shapes=scratch_shapes,
        compiler_params=_cp("arbitrary", "arbitrary", "arbitrary"),
    )(*args)
    return outs[0], tuple(outs[1:])


def _norm_fwd(h, g, *, name, out_dtype=BF16):
    s, d = h.shape
    tb = min(TB, s)

    def body(h_ref, g_ref, o_ref):
        x = h_ref[...]
        r = lax.rsqrt(jnp.mean(x * x, axis=-1, keepdims=True) + NORM_EPS)
        o_ref[...] = (x * r * g_ref[...]).astype(o_ref.dtype)

    row = pl.BlockSpec((tb, d), lambda i: (i, 0))
    return pl.pallas_call(
        body, name=name, grid=(s // tb,), in_specs=[row, pl.BlockSpec((1, d), lambda i: (0, 0))], out_specs=row,
        out_shape=_sds((s, d), out_dtype), compiler_params=_cp("parallel"),
    )(h, g)


def _norm_bwd(h, g, dhn, dres, *, name):
    s, d = h.shape
    tb = min(TB, s)
    has_res = dres is not None

    def body(*refs):
        h_ref, g_ref, dhn_ref = refs[:3]
        r_ref = refs[3] if has_res else None
        dh_ref, dhb_ref, dg_ref = refs[3 + has_res:]
        x = h_ref[...]
        r = lax.rsqrt(jnp.mean(x * x, axis=-1, keepdims=True) + NORM_EPS)
        xhat = x * r
        dy = dhn_ref[...]
        gy = dy * g_ref[...]
        dx = r * (gy - xhat * jnp.mean(gy * xhat, axis=-1, keepdims=True))
        if has_res:
            dx = dx + r_ref[...]
        dh_ref[...] = dx
        dhb_ref[...] = dx.astype(BF16)
        part = jnp.sum(dy * xhat, axis=0, keepdims=True)

        @pl.when(pl.program_id(0) == 0)
        def _():
            dg_ref[...] = part

        @pl.when(pl.program_id(0) > 0)
        def _():
            dg_ref[...] += part

    row = pl.BlockSpec((tb, d), lambda i: (i, 0))
    vec = pl.BlockSpec((1, d), lambda i: (0, 0))
    return pl.pallas_call(
        body, name=name, grid=(s // tb,), in_specs=[row, vec, row] + ([row] if has_res else []),
        out_specs=(row, row, vec), out_shape=(_sds((s, d), F32), _sds((s, d), BF16), _sds((1, d), F32)),
        compiler_params=_cp("arbitrary"),
    )(*((h, g, dhn) + ((dres,) if has_res else ())))


def _loss_head(h, g, tgt, *, name):
    s, d = h.shape
    tb = min(TB, s)

    def body(h_ref, g_ref, t_ref, loss_ref, dh_ref, dhb_ref, dg_ref):
        x = h_ref[...]
        r = lax.rsqrt(jnp.mean(x * x, axis=-1, keepdims=True) + NORM_EPS)
        xhat = x * r
        gain = g_ref[...]
        err = xhat * gain - t_ref[...]
        part_loss = 0.5 * jnp.sum(jnp.mean(err * err, axis=-1, keepdims=True), axis=0, keepdims=True)
        dy = err * (1.0 / d)
        gy = dy * gain
        dx = r * (gy - xhat * jnp.mean(gy * xhat, axis=-1, keepdims=True))
        dh_ref[...] = dx
        dhb_ref[...] = dx.astype(BF16)
        part = jnp.sum(dy * xhat, axis=0, keepdims=True)
        lossv = jnp.broadcast_to(part_loss, (1, LANES))

        @pl.when(pl.program_id(0) == 0)
        def _():
            dg_ref[...] = part
            loss_ref[...] = lossv

        @pl.when(pl.program_id(0) > 0)
        def _():
            dg_ref[...] += part
            loss_ref[...] += lossv

    row = pl.BlockSpec((tb, d), lambda i: (i, 0))
    vec = pl.BlockSpec((1, d), lambda i: (0, 0))
    return pl.pallas_call(
        body, name=name, grid=(s // tb,), in_specs=[row, vec, row],
        out_specs=(pl.BlockSpec((1, LANES), lambda i: (0, 0)), row, row, vec),
        out_shape=(_sds((1, LANES), F32), _sds((s, d), F32), _sds((s, d), BF16), _sds((1, d), F32)),
        compiler_params=_cp("arbitrary"),
    )(h, g, tgt)


def _swiglu_fwd(gu, *, name):
    s = gu.shape[0]
    tb = min(TB, s)

    def body(g_ref, u_ref, o_ref):
        g = g_ref[...].astype(F32)
        o_ref[...] = (g * _sigmoid(g) * u_ref[...].astype(F32)).astype(o_ref.dtype)

    return pl.pallas_call(
        body, name=name, grid=(s // tb,),
        in_specs=[pl.BlockSpec((tb, D_FF), lambda i: (i, 0)), pl.BlockSpec((tb, D_FF), lambda i: (i, 1))],
        out_specs=pl.BlockSpec((tb, D_FF), lambda i: (i, 0)), out_shape=_sds((s, D_FF), BF16),
        compiler_params=_cp("parallel"),
    )(gu, gu)


def _swiglu_bwd(gu, dact, *, name):
    s = gu.shape[0]
    tb = min(TB, s)

    def body(g_ref, u_ref, d_ref, o_ref):
        g = g_ref[...].astype(F32)
        sg = _sigmoid(g)
        da = d_ref[...].astype(F32)
        o_ref[:, :D_FF] = (da * u_ref[...].astype(F32) * sg * (1.0 + g * (1.0 - sg))).astype(o_ref.dtype)
        o_ref[:, D_FF:] = (da * g * sg).astype(o_ref.dtype)

    return pl.pallas_call(
        body, name=name, grid=(s // tb,),
        in_specs=[pl.BlockSpec((tb, D_FF), lambda i: (i, 0)), pl.BlockSpec((tb, D_FF), lambda i: (i, 1)),
                  pl.BlockSpec((tb, D_FF), lambda i: (i, 0))],
        out_specs=pl.BlockSpec((tb, 2 * D_FF), lambda i: (i, 0)), out_shape=_sds((s, 2 * D_FF), BF16),
        compiler_params=_cp("parallel"),
    )(gu, gu, dact)


def _softmax_rows(qh, kh):
    sc = lax.dot_general(qh, kh, NT, preferred_element_type=F32) * (XA_HD ** -0.5)
    sc = sc - jnp.max(sc, axis=-1, keepdims=True)
    e = jnp.exp(sc)
    return e / jnp.sum(e, axis=-1, keepdims=True)


def _attn_fwd(q, k, v, *, name):
    s = q.shape[0]
    n_mem = k.shape[0]
    tq = min(512, s)

    def body(q_ref, k_ref, v_ref, o_ref):
        outs = []
        for h in range(XA_HEADS):
            sl = slice(h * XA_HD, (h + 1) * XA_HD)
            p = _softmax_rows(q_ref[:, sl], k_ref[:, sl])
            outs.append(jnp.dot(p.astype(BF16), v_ref[:, sl], preferred_element_type=F32))
        o_ref[...] = jnp.concatenate(outs, axis=1).astype(o_ref.dtype)

    row = pl.BlockSpec((tq, D_XA), lambda i: (i, 0))
    kv = pl.BlockSpec((n_mem, D_XA), lambda i: (0, 0))
    return pl.pallas_call(
        body, name=name, grid=(s // tq,), in_specs=[row, kv, kv], out_specs=row, out_shape=_sds((s, D_XA), BF16),
        compiler_params=_cp("parallel"),
    )(q, k, v)


def _attn_bwd(q, k, v, do, *, name):
    s = q.shape[0]
    n_mem = k.shape[0]
    tq = min(512, s)

    def body(q_ref, k_ref, v_ref, do_ref, dq_ref, dk_ref, dv_ref):
        dqs, dks, dvs = [], [], []
        for h in range(XA_HEADS):
            sl = slice(h * XA_HD, (h + 1) * XA_HD)
            qh, kh, vh, doh = q_ref[:, sl], k_ref[:, sl], v_ref[:, sl], do_ref[:, sl]
            p = _softmax_rows(qh, kh)
            dvs.append(lax.dot_general(p.astype(BF16), doh, TN, preferred_element_type=F32))
            dp = lax.dot_general(doh, vh, NT, preferred_element_type=F32)
            ds = (p * (dp - jnp.sum(dp * p, axis=-1, keepdims=True)) * (XA_HD ** -0.5)).astype(BF16)
            dqs.append(jnp.dot(ds, kh, preferred_element_type=F32))
            dks.append(lax.dot_general(ds, qh, TN, preferred_element_type=F32))
        dq_ref[...] = jnp.concatenate(dqs, axis=1).astype(dq_ref.dtype)
        dk = jnp.concatenate(dks, axis=1)
        dv = jnp.concatenate(dvs, axis=1)

        @pl.when(pl.program_id(0) == 0)
        def _():
            dk_ref[...] = dk
            dv_ref[...] = dv

        @pl.when(pl.program_id(0) > 0)
        def _():
            dk_ref[...] += dk
            dv_ref[...] += dv

    row = pl.BlockSpec((tq, D_XA), lambda i: (i, 0))
    kv = pl.BlockSpec((n_mem, D_XA), lambda i: (0, 0))
    return pl.pallas_call(
        body, name=name, grid=(s // tq,), in_specs=[row, kv, kv, row], out_specs=(row, kv, kv),
        out_shape=(_sds((s, D_XA), BF16), _sds((n_mem, D_XA), F32), _sds((n_mem, D_XA), F32)),
        compiler_params=_cp("arbitrary"),
    )(q, k, v, do)


CONV_CB = 1024
XBC_CB0 = D // CONV_CB


def _prev_rows(i, tb):
    return jnp.maximum(i * (tb // HALO) - 1, 0)


def _next_rows(i, tb, s):
    return jnp.minimum((i + 1) * (tb // HALO), s // HALO - 1)


def _conv4(xcat, w_ref, lo):
    acc = w_ref[3:4, :] * xcat[lo:]
    for k in range(3):
        acc = acc + w_ref[k:k + 1, :] * pltpu.roll(xcat, 3 - k, 0)[lo:]
    return acc


def _conv_silu_fwd(proj, w, b, *, name):
    s = proj.shape[0]
    tb = min(TB, s)

    def body(x_ref, xp_ref, w_ref, b_ref, o_ref):
        i = pl.program_id(0)
        prev = jnp.where(i > 0, xp_ref[...], 0.0)
        pre = _conv4(jnp.concatenate([prev, x_ref[...]], axis=0), w_ref, HALO) + b_ref[...]
        o_ref[...] = pre * _sigmoid(pre)

    return pl.pallas_call(
        body, name=name, grid=(s // tb, D_XBC // CONV_CB),
        in_specs=[pl.BlockSpec((tb, CONV_CB), lambda i, j: (i, XBC_CB0 + j)),
                  pl.BlockSpec((HALO, CONV_CB), lambda i, j: (_prev_rows(i, tb), XBC_CB0 + j)),
                  pl.BlockSpec((4, CONV_CB), lambda i, j: (0, j)),
                  pl.BlockSpec((1, CONV_CB), lambda i, j: (0, j))],
        out_specs=pl.BlockSpec((tb, CONV_CB), lambda i, j: (i, j)), out_shape=_sds((s, D_XBC), F32),
        compiler_params=_cp("parallel", "parallel"),
    )(proj, proj, w, b)


def _conv_silu_bwd(proj, dy, w, b, *, name):
    s = proj.shape[0]
    tb = min(TB, s)
    n_i = s // tb

    def body(x_ref, xp_ref, xn_ref, dy_ref, dyn_ref, w_ref, b_ref, dx_ref, dw_ref, db_ref):
        i = pl.program_id(1)
        prev = jnp.where(i > 0, xp_ref[...], 0.0)
        xcat = jnp.concatenate([prev, x_ref[...], xn_ref[...]], axis=0)
        pre = _conv4(xcat, w_ref, HALO) + b_ref[...]
        sg = _sigmoid(pre)
        dy_ext = jnp.concatenate([dy_ref[...], jnp.where(i < n_i - 1, dyn_ref[...], 0.0)], axis=0)
        dpre = dy_ext * sg * (1.0 + pre * (1.0 - sg))
        n2 = tb + HALO
        dx = w_ref[3:4, :] * dpre[:tb]
        for k in range(3):
            dx = dx + w_ref[k:k + 1, :] * pltpu.roll(dpre, n2 - (3 - k), 0)[:tb]
        dx_ref[...] = dx.astype(dx_ref.dtype)
        dpc = dpre[:tb]
        parts = [jnp.sum(dpc * pltpu.roll(xcat, 3 - k, 0)[HALO:HALO + tb], axis=0, keepdims=True) for k in range(3)]
        parts.append(jnp.sum(dpc * x_ref[...], axis=0, keepdims=True))
        dbp = jnp.sum(dpc, axis=0, keepdims=True)

        @pl.when(i == 0)
        def _():
            for k in range(4):
                dw_ref[k] = parts[k]
            db_ref[...] = dbp

        @pl.when(i > 0)
        def _():
            for k in range(4):
                dw_ref[k] += parts[k]
            db_ref[...] += dbp

    return pl.pallas_call(
        body, name=name, grid=(D_XBC // CONV_CB, n_i),
        in_specs=[pl.BlockSpec((tb, CONV_CB), lambda j, i: (i, XBC_CB0 + j)),
                  pl.BlockSpec((HALO, CONV_CB), lambda j, i: (_prev_rows(i, tb), XBC_CB0 + j)),
                  pl.BlockSpec((HALO, CONV_CB), lambda j, i: (_next_rows(i, tb, s), XBC_CB0 + j)),
                  pl.BlockSpec((tb, CONV_CB), lambda j, i: (i, j)),
                  pl.BlockSpec((HALO, CONV_CB), lambda j, i: (_next_rows(i, tb, s), j)),
                  pl.BlockSpec((4, CONV_CB), lambda j, i: (0, j)),
                  pl.BlockSpec((1, CONV_CB), lambda j, i: (0, j))],
        out_specs=(pl.BlockSpec((tb, CONV_CB), lambda j, i: (i, j)),
                   pl.BlockSpec((4, 1, CONV_CB), lambda j, i: (0, 0, j)),
                   pl.BlockSpec((1, CONV_CB), lambda j, i: (0, j))),
        out_shape=(_sds((s, D_XBC), BF16), _sds((4, 1, D_XBC), F32), _sds((1, D_XBC), F32)),
        compiler_params=_cp("parallel", "arbitrary"),
    )(proj, proj, proj, dy, dy, w, b)


SC_U0 = (D + D_XBC) // CONV_CB
SC_B0 = SC_U0 + D // CONV_CB
SC_C0 = SC_B0 + D // CONV_CB


def _conv3(cat, w_ref, lo):
    return (w_ref[2:3, :] * cat[lo:] + w_ref[1:2, :] * pltpu.roll(cat, 1, 0)[lo:]
            + w_ref[0:1, :] * pltpu.roll(cat, 2, 0)[lo:])


def _sc_fwd(proj, w, gain, *, name):
    s = proj.shape[0]
    tb = min(TB, s)

    def body(u_ref, b_ref, c_ref, up_ref, cp_ref, w_ref, g_ref, o_ref):
        i = pl.program_id(0)
        cup = jnp.where(i > 0, cp_ref[...] * up_ref[...], 0.0)
        cat = jnp.concatenate([cup, c_ref[...] * u_ref[...]], axis=0)
        v = b_ref[...] * _conv3(cat, w_ref, HALO)
        o_ref[...] = (v * _group_rstd(v, SC_GROUP_W) * g_ref[...]).astype(o_ref.dtype)

    def cur(c0):
        return pl.BlockSpec((tb, CONV_CB), lambda i, j: (i, c0 + j))

    def prev(c0):
        return pl.BlockSpec((HALO, CONV_CB), lambda i, j: (_prev_rows(i, tb), c0 + j))

    return pl.pallas_call(
        body, name=name, grid=(s // tb, D // CONV_CB),
        in_specs=[cur(SC_U0), cur(SC_B0), cur(SC_C0), prev(SC_U0), prev(SC_C0),
                  pl.BlockSpec((3, CONV_CB), lambda i, j: (0, j)), pl.BlockSpec((1, CONV_CB), lambda i, j: (0, j))],
        out_specs=pl.BlockSpec((tb, CONV_CB), lambda i, j: (i, j)), out_shape=_sds((s, D), BF16),
        compiler_params=_cp("parallel", "parallel"),
    )(proj, proj, proj, proj, proj, w, gain)


def _sc_bwd(proj, dcat, w, gain, *, name):
    s = proj.shape[0]
    tb = min(TB, s)
    n_i = s // tb
    dy0 = D // CONV_CB

    def body(u_ref, b_ref, c_ref, up_ref, cp_ref, un_ref, bn_ref, cn_ref, dy_ref, dyn_ref, w_ref, g_ref,
             du_ref, db_ref, dc_ref, dw_ref, dg_ref):
        i = pl.program_id(1)
        u, c = u_ref[...], c_ref[...]
        cup = jnp.where(i > 0, cp_ref[...] * up_ref[...], 0.0)
        cu_ext = jnp.concatenate([cup, c * u, cn_ref[...] * un_ref[...]], axis=0)
        conv = _conv3(cu_ext, w_ref, HALO)
        b_ext = jnp.concatenate([b_ref[...], bn_ref[...]], axis=0)
        dy_ext = jnp.concatenate([dy_ref[...], jnp.where(i < n_i - 1, dyn_ref[...], 0.0)], axis=0)
        v = b_ext * conv
        r = _group_rstd(v, SC_GROUP_W)
        vhat = v * r
        dvhat = dy_ext * g_ref[...]
        dv = r * (dvhat - vhat * _group_mean(dvhat * vhat, SC_GROUP_W))
        dconv = dv * b_ext
        n2 = tb + HALO
        dcu = (w_ref[2:3, :] * dconv[:tb] + w_ref[1:2, :] * pltpu.roll(dconv, n2 - 1, 0)[:tb]
               + w_ref[0:1, :] * pltpu.roll(dconv, n2 - 2, 0)[:tb])
        du_ref[...] = (dcu * c).astype(du_ref.dtype)
        dc_ref[...] = (dcu * u).astype(dc_ref.dtype)
        db_ref[...] = (dv * conv)[:tb].astype(db_ref.dtype)
        dcc = dconv[:tb]
        parts = [jnp.sum(dcc * pltpu.roll(cu_ext, 2 - k, 0)[HALO:HALO + tb], axis=0, keepdims=True) for k in range(2)]
        parts.append(jnp.sum(dcc * c * u, axis=0, keepdims=True))
        dgp = jnp.sum((dy_ext * vhat)[:tb], axis=0, keepdims=True)

        @pl.when(i == 0)
        def _():
            for k in range(3):
                dw_ref[k] = parts[k]
            dg_ref[...] = dgp

        @pl.when(i > 0)
        def _():
            for k in range(3):
                dw_ref[k] += parts[k]
            dg_ref[...] += dgp

    def cur(c0):
        return pl.BlockSpec((tb, CONV_CB), lambda j, i: (i, c0 + j))

    def prev(c0):
        return pl.BlockSpec((HALO, CONV_CB), lambda j, i: (_prev_rows(i, tb), c0 + j))

    def nxt(c0):
        return pl.BlockSpec((HALO, CONV_CB), lambda j, i: (_next_rows(i, tb, s), c0 + j))

    vec = pl.BlockSpec((1, CONV_CB), lambda j, i: (0, j))
    out_row = pl.BlockSpec((tb, CONV_CB), lambda j, i: (i, j))
    return pl.pallas_call(
        body, name=name, grid=(D // CONV_CB, n_i),
        in_specs=[cur(SC_U0), cur(SC_B0), cur(SC_C0), prev(SC_U0), prev(SC_C0), nxt(SC_U0), nxt(SC_B0), nxt(SC_C0),
                  cur(dy0), nxt(dy0), pl.BlockSpec((3, CONV_CB), lambda j, i: (0, j)), vec],
        out_specs=(out_row, out_row, out_row, pl.BlockSpec((3, 1, CONV_CB), lambda j, i: (0, 0, j)), vec),
        out_shape=(_sds((s, D), BF16), _sds((s, D), BF16), _sds((s, D), BF16), _sds((3, 1, D), F32), _sds((1, D), F32)),
        compiler_params=_cp("parallel", "arbitrary"),
    )(proj, proj, proj, proj, proj, proj, proj, proj, dcat, dcat, w, gain)


def _gated_norm_fwd(y, proj, gain, *, name):
    s = y.shape[0]
    tb = min(TB, s)

    def body(y_ref, z_ref, g_ref, o_ref):
        z = z_ref[...]
        t = y_ref[...] * z * _sigmoid(z)
        o_ref[...] = (t * _group_rstd(t, SSD_GROUP_W) * g_ref[...]).astype(o_ref.dtype)

    row = pl.BlockSpec((tb, D), lambda i: (i, 0))
    return pl.pallas_call(
        body, name=name, grid=(s // tb,), in_specs=[row, row, pl.BlockSpec((1, D), lambda i: (0, 0))], out_specs=row,
        out_shape=_sds((s, D), BF16), compiler_params=_cp("parallel"),
    )(y, proj, gain)


def _gated_norm_bwd(y, proj, dcat, gain, *, name):
    s = y.shape[0]
    tb = min(TB, s)

    def body(y_ref, z_ref, d_ref, g_ref, dy_ref, dz_ref, dg_ref):
        z, yv, dout = z_ref[...], y_ref[...], d_ref[...]
        sg = _sigmoid(z)
        sz = z * sg
        t = yv * sz
        r = _group_rstd(t, SSD_GROUP_W)
        that = t * r
        dthat = dout * g_ref[...]
        dt = r * (dthat - that * _group_mean(dthat * that, SSD_GROUP_W))
        dy_ref[...] = dt * sz
        dz_ref[...] = (dt * yv * sg * (1.0 + z * (1.0 - sg))).astype(dz_ref.dtype)
        part = jnp.sum(dout * that, axis=0, keepdims=True)

        @pl.when(pl.program_id(0) == 0)
        def _():
            dg_ref[...] = part

        @pl.when(pl.program_id(0) > 0)
        def _():
            dg_ref[...] += part

    row = pl.BlockSpec((tb, D), lambda i: (i, 0))
    vec = pl.BlockSpec((1, D), lambda i: (0, 0))
    return pl.pallas_call(
        body, name=name, grid=(s // tb,), in_specs=[row, row, row, vec], out_specs=(row, row, vec),
        out_shape=(_sds((s, D), F32), _sds((s, D), BF16), _sds((1, D), F32)), compiler_params=_cp("arbitrary"),
    )(y, proj, dcat, gain)


def _tri(lower):
    row = lax.broadcasted_iota(jnp.int32, (CHUNK, CHUNK), 0)
    col = lax.broadcasted_iota(jnp.int32, (CHUNK, CHUNK), 1)
    return jnp.where(row >= col if lower else col >= row, 1.0, 0.0).astype(BF16)


def _head_spread():
    row = lax.broadcasted_iota(jnp.int32, (DT_W, D), 0)
    col = lax.broadcasted_iota(jnp.int32, (DT_W, D), 1)
    return jnp.where(col // HEAD_DIM == row, 1.0, 0.0).astype(BF16)


def _head_pick():
    row = lax.broadcasted_iota(jnp.int32, (D, DT_W), 0)
    col = lax.broadcasted_iota(jnp.int32, (D, DT_W), 1)
    return jnp.where(row == col * HEAD_DIM, 1.0, 0.0).astype(BF16)


def _dt_fwd(dt_raw, bias, a_log, *, name):
    s = dt_raw.shape[0]
    nc = s // CHUNK

    def body(raw_ref, bias_ref, alog_ref, dt_ref, cum_ref, cumt_ref):
        x = raw_ref[...] + bias_ref[...]
        dt = jnp.maximum(x, 0.0) + jnp.log1p(jnp.exp(-jnp.abs(x)))
        cum = _split3_dot(_tri(True), dt * (-jnp.exp(alog_ref[...])))
        spread = _head_spread()
        dt_ref[...] = _split3_dot_r(dt, spread)
        cum_ref[...] = _split3_dot_r(cum, spread)
        cumt_ref[0] = cum.T

    row = pl.BlockSpec((CHUNK, DT_W), lambda i: (i, 0))
    wide = pl.BlockSpec((CHUNK, D), lambda i: (i, 0))
    vec = pl.BlockSpec((1, DT_W), lambda i: (0, 0))
    return pl.pallas_call(
        body, name=name, grid=(nc,), in_specs=[row, vec, vec],
        out_specs=(wide, wide, pl.BlockSpec((1, DT_W, CHUNK), lambda i: (i, 0, 0))),
        out_shape=(_sds((s, D), F32), _sds((s, D), F32), _sds((nc, DT_W, CHUNK), F32)),
        compiler_params=_cp("parallel"),
    )(dt_raw, bias, a_log)


def _dt_bwd(dt_raw, bias, ddt_b, *, name):
    s = dt_raw.shape[0]
    tb = min(TB, s)

    def body(raw_ref, bias_ref, d_ref, o_ref, db_ref):
        g = _split3_dot_r(d_ref[...], _head_pick()) * _sigmoid(raw_ref[...] + bias_ref[...])
        o_ref[...] = g.astype(o_ref.dtype)
        part = jnp.sum(g, axis=0, keepdims=True)

        @pl.when(pl.program_id(0) == 0)
        def _():
            db_ref[...] = part

        @pl.when(pl.program_id(0) > 0)
        def _():
            db_ref[...] += part

    row = pl.BlockSpec((tb, DT_W), lambda i: (i, 0))
    vec = pl.BlockSpec((1, DT_W), lambda i: (0, 0))
    return pl.pallas_call(
        body, name=name, grid=(s // tb,), in_specs=[row, vec, pl.BlockSpec((tb, D), lambda i: (i, 0))],
        out_specs=(row, vec), out_shape=(_sds((s, DT_W), BF16), _sds((1, DT_W), F32)),
        compiler_params=_cp("arbitrary"),
    )(dt_raw, bias, ddt_b)


N_PAIRS = N_HEADS // 2
PAIRS_PER_GROUP = N_PAIRS // N_GROUPS
GROUP_W = PAIRS_PER_GROUP * LANES
B_CB0 = D // LANES
C_CB0 = B_CB0 + N_GROUPS


def _decay(cum_col, cum_row, causal):
    return jnp.where(causal, jnp.exp(jnp.minimum(cum_col - cum_row, 0.0)), 0.0)


def _causal_mask():
    row = lax.broadcasted_iota(jnp.int32, (CHUNK, CHUNK), 0)
    col = lax.broadcasted_iota(jnp.int32, (CHUNK, CHUNK), 1)
    return row >= col


def _state_row_scale(cumt_ref, pp):
    last0 = cumt_ref[0, 2 * pp][:, CHUNK - 1:CHUNK]
    last1 = cumt_ref[0, 2 * pp + 1][:, CHUNK - 1:CHUNK]
    rown = lax.broadcasted_iota(jnp.int32, (LANES, 1), 0)
    return jnp.exp(jnp.where(rown < HEAD_DIM, last0, last1))


def _ssd_specs(nc, rev):
    def ch(c):
        return nc - 1 - c if rev else c

    wide = pl.BlockSpec((CHUNK, GROUP_W), lambda c, g: (ch(c), g))
    vec = pl.BlockSpec((1, GROUP_W), lambda c, g: (0, g))
    cumt_spec = pl.BlockSpec((1, 2 * PAIRS_PER_GROUP, 1, CHUNK), lambda c, g: (ch(c), g, 0, 0))
    hp_spec = pl.BlockSpec((1, PAIRS_PER_GROUP, LANES, N_STATE), lambda c, g: (ch(c), g, 0, 0))

    def bc(c0):
        return pl.BlockSpec((CHUNK, LANES), lambda c, g: (ch(c), c0 + g))

    return wide, vec, cumt_spec, hp_spec, bc


def _ssd_fwd(xbc_c, dt_b, cum_b, cumt4, dskip_b, *, name):
    s = xbc_c.shape[0]
    nc = s // CHUNK

    def body(xs_ref, b_ref, c_ref, dt_ref, cum_ref, cumt_ref, dsk_ref, y_ref, hp_ref, state):
        c, g = pl.program_id(0), pl.program_id(1)

        @pl.when(c == 0)
        def _():
            state[pl.ds(g * PAIRS_PER_GROUP, PAIRS_PER_GROUP)] = jnp.zeros((PAIRS_PER_GROUP, LANES, N_STATE), F32)

        bb, cbm = b_ref[...].astype(BF16), c_ref[...].astype(BF16)
        cbv = lax.dot_general(cbm, bb, NT, preferred_element_type=F32)
        first = lax.broadcasted_iota(jnp.int32, (CHUNK, LANES), 1) < HEAD_DIM
        causal = _causal_mask()
        for pp in range(PAIRS_PER_GROUP):
            sl = slice(pp * LANES, (pp + 1) * LANES)
            xs, cum = xs_ref[:, sl], cum_ref[:, sl]
            xt = xs * dt_ref[:, sl]
            xtb = xt.astype(BF16)
            hp = state[g * PAIRS_PER_GROUP + pp]
            hp_ref[0, pp] = hp
            y = jnp.exp(cum) * lax.dot_general(cbm, hp.astype(BF16), NT, preferred_element_type=F32)
            for hh in range(2):
                lm = _decay(cum[:, hh * HEAD_DIM:hh * HEAD_DIM + 1], cumt_ref[0, 2 * pp + hh], causal)
                xm = jnp.where(first if hh == 0 else jnp.logical_not(first), xtb, jnp.zeros_like(xtb))
                y = y + jnp.dot((cbv * lm).astype(BF16), xm, preferred_element_type=F32)
            decs = jnp.exp(cum[CHUNK - 1:CHUNK, :] - cum)
            st = lax.dot_general((xt * decs).astype(BF16), bb, TN, preferred_element_type=F32)
            state[g * PAIRS_PER_GROUP + pp] = _state_row_scale(cumt_ref, pp) * hp + st
            y_ref[:, sl] = y + xs * dsk_ref[:, sl]

    wide, vec, cumt_spec, hp_spec, bc = _ssd_specs(nc, False)
    return pl.pallas_call(
        body, name=name, grid=(nc, N_GROUPS),
        in_specs=[wide, bc(B_CB0), bc(C_CB0), wide, wide, cumt_spec, vec],
        out_specs=(wide, hp_spec),
        out_shape=(_sds((s, D), F32), _sds((nc, N_PAIRS, LANES, N_STATE), F32)),
        scratch_shapes=[pltpu.VMEM((N_PAIRS, LANES, N_STATE), F32)],
        compiler_params=_cp("arbitrary", "arbitrary"),
    )(xbc_c, xbc_c, xbc_c, dt_b, cum_b, cumt4, dskip_b)


def _ssd_bwd(xbc_c, dt_b, cum_b, cumt4, hprev, dy, alog_b, dskip_b, *, name):
    s = xbc_c.shape[0]
    nc = s // CHUNK

    def body(xs_ref, b_ref, c_ref, dt_ref, cum_ref, cumt_ref, hp_ref, dy_ref, alog_ref, dsk_ref,
             dxs_ref, db_ref, dc_ref, ddt_ref, dalog_ref, ddsk_ref, dstate):
        c, g = pl.program_id(0), pl.program_id(1)
        pairs = pl.ds(g * PAIRS_PER_GROUP, PAIRS_PER_GROUP)

        @pl.when(c == 0)
        def _():
            dstate[pairs] = jnp.zeros((PAIRS_PER_GROUP, LANES, N_STATE), F32)
            dalog_ref[pairs] = jnp.zeros((PAIRS_PER_GROUP, 1, LANES), F32)
            ddsk_ref[pairs] = jnp.zeros((PAIRS_PER_GROUP, 1, LANES), F32)

        bb, cbm = b_ref[...].astype(BF16), c_ref[...].astype(BF16)
        cbv = lax.dot_general(cbm, bb, NT, preferred_element_type=F32)
        first = lax.broadcasted_iota(jnp.int32, (CHUNK, LANES), 1) < HEAD_DIM
        causal = _causal_mask()
        rown = lax.broadcasted_iota(jnp.int32, (LANES, 1), 0)
        d_b = jnp.zeros((CHUNK, N_STATE), F32)
        d_c = jnp.zeros((CHUNK, N_STATE), F32)
        for pp in range(PAIRS_PER_GROUP):
            sl = slice(pp * LANES, (pp + 1) * LANES)
            idx = g * PAIRS_PER_GROUP + pp
            xs, dtb, cum, dy = xs_ref[:, sl], dt_ref[:, sl], cum_ref[:, sl], dy_ref[:, sl]
            a_b = -jnp.exp(alog_ref[:, sl])
            xt = xs * dtb
            xtb = xt.astype(BF16)
            dyb = dy.astype(BF16)
            hp = hp_ref[0, pp]
            hpb = hp.astype(BF16)
            dh = dstate[idx]
            dhb = dh.astype(BF16)
            exp_cum = jnp.exp(cum)
            decs = jnp.exp(cum[CHUNK - 1:CHUNK, :] - cum)
            row_scale = _state_row_scale(cumt_ref, pp)
            xd = (xt * decs).astype(BF16)
            dye = (dy * exp_cum).astype(BF16)
            zero_b = jnp.zeros_like(xtb)

            h_next = row_scale * hp + lax.dot_general(xd, bb, TN, preferred_element_type=F32)
            y = exp_cum * lax.dot_general(cbm, hpb, NT, preferred_element_type=F32)
            dxt = decs * lax.dot_general(bb, dhb, NT, preferred_element_type=F32)
            dcb = jnp.zeros((CHUNK, CHUNK), F32)
            for hh in range(2):
                mask = first if hh == 0 else jnp.logical_not(first)
                lm = _decay(cum[:, hh * HEAD_DIM:hh * HEAD_DIM + 1], cumt_ref[0, 2 * pp + hh], causal)
                m = (cbv * lm).astype(BF16)
                y = y + jnp.dot(m, jnp.where(mask, xtb, zero_b), preferred_element_type=F32)
                dxt = dxt + jnp.where(mask, lax.dot_general(m, dyb, TN, preferred_element_type=F32), 0.0)
                dm = lax.dot_general(jnp.where(mask, dyb, zero_b), xtb, NT, preferred_element_type=F32)
                dcb = dcb + dm * lm
            dcbb = dcb.astype(BF16)
            d_c = d_c + jnp.dot(dcbb, bb, preferred_element_type=F32) + jnp.dot(dye, hpb, preferred_element_type=F32)
            d_b = (d_b + lax.dot_general(dcbb, cbm, TN, preferred_element_type=F32)
                   + jnp.dot(xd, dhb, preferred_element_type=F32))
            dstate[idx] = row_scale * dh + lax.dot_general(dye, cbm, TN, preferred_element_type=F32)

            d_cum = _pair_sum(dyb.astype(F32) * y - dxt * xtb.astype(F32), first)
            e = jnp.sum(dh * h_next, axis=1, keepdims=True)
            t0 = jnp.sum(jnp.where(rown < HEAD_DIM, e, 0.0), axis=0, keepdims=True)
            t1 = jnp.sum(jnp.where(rown < HEAD_DIM, 0.0, e), axis=0, keepdims=True)
            d_da = _split3_dot(_tri(False), d_cum) + jnp.where(first[0:1, :], t0, t1)
            ddt_ref[:, sl] = a_b * d_da + _pair_sum(dxt * xs, first)
            dxs_ref[:, sl] = dxt * dtb + dy * dsk_ref[:, sl]
            dalog_ref[idx] += jnp.sum(d_da * dtb * a_b, axis=0, keepdims=True)
            ddsk_ref[idx] += jnp.sum(_pair_sum(dy * xs, first), axis=0, keepdims=True)
        db_ref[...] = d_b
        dc_ref[...] = d_c

    wide, vec, cumt_spec, hp_spec, bc = _ssd_specs(nc, True)
    acc = pl.BlockSpec((N_PAIRS, 1, LANES), lambda c, g: (0, 0, 0))
    return pl.pallas_call(
        body, name=name, grid=(nc, N_GROUPS),
        in_specs=[wide, bc(B_CB0), bc(C_CB0), wide, wide, cumt_spec, hp_spec, wide, vec, vec],
        out_specs=(wide, bc(0), bc(0), wide, acc, acc),
        out_shape=(_sds((s, D), F32), _sds((s, N_GROUPS * N_STATE), F32), _sds((s, N_GROUPS * N_STATE), F32),
                   _sds((s, D), F32), _sds((N_PAIRS, 1, LANES), F32), _sds((N_PAIRS, 1, LANES), F32)),
        scratch_shapes=[pltpu.VMEM((N_PAIRS, LANES, N_STATE), F32)],
        compiler_params=_cp("arbitrary", "arbitrary"),
    )(xbc_c, xbc_c, xbc_c, dt_b, cum_b, cumt4, hprev, dy, alog_b, dskip_b)


def _lane_bcast(v):
    return jnp.repeat(v, HEAD_DIM, axis=1)


DT0 = D + D_XBC
W_IN_SHARD = (DT0 + N_HEADS + 3 * D) // N_DEV
DT_PIECE = DT0 // W_IN_SHARD
DT_OFF = DT0 - DT_PIECE * W_IN_SHARD
assert DT_OFF + N_HEADS <= W_IN_SHARD


def _split_w_in(got):
    hold = got[DT_PIECE]
    cols = ([got[d] for d in range(DT_PIECE)] + [hold[:, :DT_OFF], hold[:, DT_OFF + N_HEADS:]]
            + [got[d] for d in range(DT_PIECE + 1, N_DEV)])
    return (jnp.concatenate(cols, axis=1),
            jnp.pad(hold[:, DT_OFF:DT_OFF + N_HEADS], ((0, 0), (0, DT_W - N_HEADS))))


def _w_in_pieces(dw_main, dw_dt):
    pieces = []
    for d in range(N_DEV):
        lo = d * W_IN_SHARD
        if d < DT_PIECE:
            pieces.append(dw_main[:, lo:lo + W_IN_SHARD])
        elif d == DT_PIECE:
            pieces.append(jnp.concatenate(
                [dw_main[:, lo:DT0], dw_dt[:, :N_HEADS], dw_main[:, DT0:lo + W_IN_SHARD - N_HEADS]], axis=1))
        else:
            pieces.append(dw_main[:, lo - N_HEADS:lo - N_HEADS + W_IN_SHARD])
    return jnp.stack(pieces)


def _prep_small(conv_w, conv_b, dt_bias, a_log, d_skip):
    pad = DT_W - N_HEADS
    return dict(
        conv_w=conv_w, conv_b=conv_b.reshape(1, D_XBC),
        dt_bias=jnp.pad(dt_bias.reshape(1, N_HEADS), ((0, 0), (0, pad))),
        a_log=jnp.pad(a_log.reshape(1, N_HEADS), ((0, 0), (0, pad))),
        alog_b=_lane_bcast(a_log.reshape(1, N_HEADS)), dskip_b=_lane_bcast(d_skip.reshape(1, N_HEADS)))


def _layer_fwd(h0, memn, p, g, li, comm=None):
    s = h0.shape[0]
    nc = s // CHUNK
    n = f"l{li}_"

    def hosted(what, *args, **kw):
        side = comm.ag_side(what, li) if comm is not None else None
        if side is None:
            return _mm(*args, **kw)
        out, got = _mm(*args, side=side, **kw)
        comm.after_ag(what, li, p, got)
        return out

    if comm is not None:
        comm.before_layer(li, p)
    hn = _norm_fwd(h0, g["norm_mix"], name=n + "norm_mix")
    proj = hosted("proj", hn, p["w_main"], name=n + "proj")
    dt_raw = _mm(hn, p["w_dt"], name=n + "proj_dt")
    xbc_c = _conv_silu_fwd(proj, p["conv_w"], p["conv_b"], name=n + "conv")
    dt_b, cum_b, cumt = _dt_fwd(dt_raw, p["dt_bias"], p["a_log"], name=n + "dt")
    cumt4 = cumt[:, :N_HEADS, :].reshape(nc, N_HEADS, 1, CHUNK)
    y, hprev = _ssd_fwd(xbc_c, dt_b, cum_b, cumt4, p["dskip_b"], name=n + "ssd")
    y_ssd = _gated_norm_fwd(y, proj, g["ssd_norm"], name=n + "gnorm")
    y_sc = _sc_fwd(proj, p["sc_conv_w"], g["sc_norm"], name=n + "sc")
    cat = jnp.concatenate([y_ssd, y_sc], axis=1)
    h1 = hosted("out", cat, p["w_out"], res=h0, bm=512, j_outer=True, name=n + "out")
    hx = _norm_fwd(h1, g["norm_xa"], name=n + "norm_xa")
    q = _mm(hx, p["w_q"], out_dtype=BF16, name=n + "q")
    k = _mm(memn, p["w_k"], out_dtype=BF16, name=n + "k")
    v = _mm(memn, p["w_v"], out_dtype=BF16, name=n + "v")
    o = _attn_fwd(q, k, v, name=n + "attn")
    h2 = _mm(o, p["w_o"], res=h1, name=n + "o")
    hf = _norm_fwd(h2, g["norm_ffn"], name=n + "norm_ffn")
    gu = hosted("gu", hf, p["w_gu"], out_dtype=BF16, name=n + "gu")
    act = _swiglu_fwd(gu, name=n + "swiglu")
    h3 = hosted("down", act, p["w_down"], res=h2, bm=512, j_outer=True, name=n + "down")
    saved = dict(h0=h0, hn=hn, proj=proj, dt_raw=dt_raw, xbc_c=xbc_c, dt_b=dt_b, cum_b=cum_b, cumt4=cumt4, hprev=hprev,
                 y=y, cat=cat, h1=h1, hx=hx, q=q, k=k, v=v, o=o, h2=h2, hf=hf, gu=gu, act=act)
    return h3, saved


def _layer_bwd(dh, dhb, dmemn, memn, p, g, sv, li, comm=None):
    n = f"l{li}b_"
    gr = {}

    def hosted(names, *args, **kw):
        if comm is None:
            return _mm(*args, **kw)
        out, got = _mm(*args, side=comm.grad_side(names, gr), **kw)
        comm.after_grads(li, names, got)
        return out

    wide = dict(bm=512, bn=512)
    dact = _mm(dhb, p["w_down"], tb=True, bn=512, out_dtype=BF16, name=n + "dact")
    gr["w_down"] = _mm(sv["act"], dhb, ta=True, out_dtype=BF16, name=n + "dw_down", **wide)
    dgu = _swiglu_bwd(sv["gu"], dact, name=n + "swiglu")
    dw_gu = _mm(sv["hf"], dgu, ta=True, out_dtype=BF16, name=n + "dw_gu", **wide)
    gr["w_gate"], gr["w_up"] = dw_gu[:, :D_FF], dw_gu[:, D_FF:]
    dhf = hosted(("w_down", "w_gate"), dgu, p["w_gu"], tb=True, j_outer=True, name=n + "dhf", **wide)
    dh, dhb, gr["norm_ffn"] = _norm_bwd(sv["h2"], g["norm_ffn"], dhf, dh, name=n + "norm_ffn")
    do = _mm(dhb, p["w_o"], tb=True, out_dtype=BF16, name=n + "do")
    gr["w_o"] = _mm(sv["o"], dhb, ta=True, out_dtype=BF16, name=n + "dw_o", **wide)
    dq, dk, dv = _attn_bwd(sv["q"], sv["k"], sv["v"], do, name=n + "attn")
    gr["w_q"] = _mm(sv["hx"], dq, ta=True, out_dtype=BF16, name=n + "dw_q", **wide)
    gr["w_k"] = _mm(memn, dk, ta=True, out_dtype=BF16, name=n + "dw_k")
    gr["w_v"] = _mm(memn, dv, ta=True, out_dtype=BF16, name=n + "dw_v")
    dmemn = _mm(dk, p["w_k"], tb=True, res=dmemn, name=n + "dmem_k")
    dmemn = _mm(dv, p["w_v"], tb=True, res=dmemn, name=n + "dmem_v")
    dhx = _mm(dq, p["w_q"], tb=True, name=n + "dhx")
    dh, dhb, gr["norm_xa"] = _norm_bwd(sv["h1"], g["norm_xa"], dhx, dh, name=n + "norm_xa")
    dcat = _mm(dhb, p["w_out"], tb=True, name=n + "dcat")
    gr["w_out"] = _mm(sv["cat"], dhb, ta=True, out_dtype=BF16, name=n + "dw_out", **wide)
    du, dgb, dgc, dsc_w, gr["sc_norm"] = _sc_bwd(sv["proj"], dcat, p["sc_conv_w"], g["sc_norm"], name=n + "sc")
    gr["sc_conv_w"] = dsc_w.reshape(3, D)
    dy, dz, gr["ssd_norm"] = _gated_norm_bwd(sv["y"], sv["proj"], dcat, g["ssd_norm"], name=n + "gnorm")
    dxs, d_b, d_c, ddt_b, dalog, ddsk = _ssd_bwd(sv["xbc_c"], sv["dt_b"], sv["cum_b"], sv["cumt4"], sv["hprev"], dy,
                                                 p["alog_b"], p["dskip_b"], name=n + "ssd")
    gr["a_log"] = dalog.reshape(N_HEADS, HEAD_DIM)[:, 0]
    gr["d_skip"] = ddsk.reshape(N_HEADS, HEAD_DIM)[:, 0]
    dxbc_c = jnp.concatenate([dxs, d_b, d_c], axis=1)
    dxbc, dconv_w, dconv_b = _conv_silu_bwd(sv["proj"], dxbc_c, p["conv_w"], p["conv_b"], name=n + "conv")
    gr["ssd_conv_w"] = dconv_w.reshape(4, D_XBC)
    gr["ssd_conv_b"] = dconv_b.reshape(D_XBC)
    ddt_raw, ddt_bias = _dt_bwd(sv["dt_raw"], p["dt_bias"], ddt_b, name=n + "dt")
    gr["dt_bias"] = ddt_bias[0, :N_HEADS]
    dproj = jnp.concatenate([dz, dxbc, du, dgb, dgc], axis=1)
    dw_main = hosted(("w_up", "w_out", "w_q", "w_k", "w_v", "w_o"), sv["hn"], dproj, ta=True, out_dtype=BF16,
                     name=n + "dw_main", **wide)
    dw_dt = _mm(sv["hn"], ddt_raw, ta=True, out_dtype=BF16, name=n + "dw_dt", **wide)
    gr["w_in"] = _w_in_pieces(dw_main, dw_dt)
    dhn = hosted(("w_in",), dproj, p["w_main"], tb=True, j_outer=True, name=n + "dhn", **wide)
    dhn = _mm(ddt_raw, p["w_dt"], tb=True, res=dhn, name=n + "dhn_dt")
    dh, dhb, gr["norm_mix"] = _norm_bwd(sv["h0"], g["norm_mix"], dhn, dh, name=n + "norm_mix")
    return dh, dhb, dmemn, gr


def _local_step(x, mem, tgt, layers, gains, mem_norm, norm_final, comm=None):
    depth = len(layers)
    memn_f = _norm_fwd(mem, mem_norm, out_dtype=F32, name="mem_norm")
    memn = memn_f.astype(BF16)
    h = x
    saved = []
    for li in range(depth):
        h, sv = _layer_fwd(h, memn, layers[li], gains[li], li, comm)
        saved.append(sv)
    loss, dh, dhb, d_final = _loss_head(h, norm_final, tgt, name="loss_head")
    dmemn = jnp.zeros(mem.shape, F32)
    grads = [None] * depth
    for li in reversed(range(depth)):
        dh, dhb, dmemn, grads[li] = _layer_bwd(dh, dhb, dmemn, memn, layers[li], gains[li], saved[li], li, comm)
    _, _, d_mem_norm = _norm_bwd(mem, mem_norm, dmemn, None, name="mem_norm_b")
    return loss, dh, grads, d_mem_norm, d_final


ADAMW_ROWS = (64, 32, 16, 8)


def _adamw(parts, w, m, v, *, name):
    depth = len(parts)
    r, c_dim = parts[0].shape[1:]
    assert w.shape == (depth * r, c_dim), (name, w.shape, parts[0].shape)
    tr = next(t for t in ADAMW_ROWS if r % t == 0)
    nb = r // tr

    def body(*refs):
        p_refs = refs[:depth]
        w_ref, m_ref, v_ref, g_ref, d_ref, nm_ref, nv_ref = refs[depth:]
        for l in range(depth):
            @pl.when(pl.program_id(0) == l)
            def _(l=l):
                g = p_refs[l][0].astype(F32)
                for s in range(1, N_DEV):
                    g = g + p_refs[l][s].astype(F32)
                g_ref[...] = g

        g = g_ref[...]
        m2 = ADAM_B1 * m_ref[...] + (1.0 - ADAM_B1) * g
        v2 = ADAM_B2 * v_ref[...] + (1.0 - ADAM_B2) * (g * g)
        m_hat = m2 / (1.0 - ADAM_B1 ** ADAM_STEP)
        v_hat = v2 / (1.0 - ADAM_B2 ** ADAM_STEP)
        d_ref[...] = -ADAM_LR * (m_hat / (jnp.sqrt(v_hat) + ADAM_EPS) + ADAM_WD * w_ref[...])
        nm_ref[...] = m2
        nv_ref[...] = v2

    def part_spec(l):
        return pl.BlockSpec((N_DEV, tr, c_dim),
                            lambda lay, i: (0, jnp.where(lay == l, i, jnp.where(lay < l, 0, nb - 1)), 0))

    row = pl.BlockSpec((tr, c_dim), lambda lay, i: (lay * nb + i, 0))
    out = _sds((depth * r, c_dim), F32)
    return pl.pallas_call(
        body, name=name, grid=(depth, nb), in_specs=[part_spec(l) for l in range(depth)] + [row, row, row],
        out_specs=(row, row, row, row), out_shape=(out, out, out, out), compiler_params=_cp("arbitrary", "arbitrary"),
    )(*parts, w, m, v)


BIG = ("w_in", "w_out", "w_q", "w_k", "w_v", "w_o", "w_gate", "w_up", "w_down")
COL_SHARDED = ("w_in", "w_o", "w_gate", "w_up")
PROJ_HOSTED = ("w_out", "w_q", "w_k", "w_v", "w_o", "w_gate")
SMALL_REPL = ("norm_mix", "ssd_conv_b", "dt_bias", "a_log", "d_skip", "ssd_norm", "sc_norm", "mem_norm", "norm_xa",
              "norm_ffn", "norm_final")
SMALL_SHARDED = ("ssd_conv_w", "sc_conv_w")
WEIGHTS = ("norm_mix", "w_in", "ssd_conv_w", "ssd_conv_b", "dt_bias", "a_log", "d_skip", "ssd_norm", "sc_conv_w",
           "sc_norm", "w_out", "mem_norm", "norm_xa", "w_q", "w_k", "w_v", "w_o", "norm_ffn", "w_gate", "w_up",
           "w_down", "norm_final")


def _assemble(got, col_sharded):
    l, _, r, c_dim = got.shape
    if col_sharded:
        return jnp.transpose(got, (0, 2, 1, 3)).reshape(l, r, N_DEV * c_dim)
    return got.reshape(l, N_DEV * r, c_dim)


def _to_pieces(full, col_sharded):
    rr, cc = full.shape
    if col_sharded:
        return jnp.transpose(full.reshape(rr, N_DEV, cc // N_DEV), (1, 0, 2))
    return full.reshape(N_DEV, rr // N_DEV, cc)


class _Comm:
    def __init__(self, w, first_w_in):
        self.w = w
        self.depth = w["w_in"].shape[0]
        self.next_w_in = first_w_in
        self.next_w_down = None
        self.gate = None
        self.got = {nm: [None] * self.depth for nm in BIG}

    def _wanted(self, what, li):
        more = li + 1 < self.depth
        if what == "proj":
            return [(nm, li) for nm in PROJ_HOSTED]
        if what == "out":
            return [("w_up", li)]
        if what == "gu":
            return ([("w_in", li + 1)] if more else []) + ([("w_down", 0)] if li == 0 else [])
        return [("w_down", li + 1)] if more else []

    def before_layer(self, li, p):
        p["w_main"], p["w_dt"] = _split_w_in(self.next_w_in)
        if li > 0:
            p["w_down"] = self.next_w_down

    def ag_side(self, what, li):
        wanted = self._wanted(what, li)
        return _ag_side([self.w[nm][l:l + 1].astype(BF16) for nm, l in wanted]) if wanted else None

    def after_ag(self, what, li, p, got):
        for (nm, l), g in zip(self._wanted(what, li), got):
            if nm == "w_in":
                self.next_w_in = g[0]
                continue
            full = _assemble(g, nm in COL_SHARDED)[0]
            if nm == "w_gate":
                self.gate = full
            elif nm == "w_up":
                p["w_gu"] = jnp.concatenate([self.gate, full], axis=1)
            elif nm == "w_down" and l > li:
                self.next_w_down = full
            else:
                p[nm] = full

    def grad_side(self, names, gr):
        return _rs_side([gr[nm] if nm == "w_in" else _to_pieces(gr[nm], nm in COL_SHARDED) for nm in names])

    def after_grads(self, li, names, got):
        for nm, g in zip(names, got):
            self.got[nm][li] = g


def _pack(arrs, names):
    flat = jnp.concatenate([arrs[nm].reshape(-1).astype(F32) for nm in names])
    rows = -(-flat.shape[0] // (TB * LANES)) * TB
    return jnp.pad(flat, (0, rows * LANES - flat.shape[0])).reshape(rows, LANES)


def _unpack(packed, shapes, names):
    flat = packed.reshape(-1)
    out, off = {}, 0
    for nm in names:
        size = 1
        for dim in shapes[nm]:
            size *= dim
        out[nm] = flat[off:off + size].reshape(shapes[nm])
        off += size
    return out


def kernel(x, mem, norm_mix, w_in, ssd_conv_w, ssd_conv_b, dt_bias, a_log, d_skip, ssd_norm, sc_conv_w, sc_norm, w_out, mem_norm, norm_xa, w_q, w_k, w_v, w_o, norm_ffn, w_gate, w_up, w_down, norm_final, loss_target, m_norm_mix, m_w_in, m_ssd_conv_w, m_ssd_conv_b, m_dt_bias, m_a_log, m_d_skip, m_ssd_norm, m_sc_conv_w, m_sc_norm, m_w_out, m_mem_norm, m_norm_xa, m_w_q, m_w_k, m_w_v, m_w_o, m_norm_ffn, m_w_gate, m_w_up, m_w_down, m_norm_final, v_norm_mix, v_w_in, v_ssd_conv_w, v_ssd_conv_b, v_dt_bias, v_a_log, v_d_skip, v_ssd_norm, v_sc_conv_w, v_sc_norm, v_w_out, v_mem_norm, v_norm_xa, v_w_q, v_w_k, v_w_v, v_w_o, v_norm_ffn, v_w_gate, v_w_up, v_w_down, v_norm_final):
    w = dict(norm_mix=norm_mix, w_in=w_in, ssd_conv_w=ssd_conv_w, ssd_conv_b=ssd_conv_b, dt_bias=dt_bias, a_log=a_log,
             d_skip=d_skip, ssd_norm=ssd_norm, sc_conv_w=sc_conv_w, sc_norm=sc_norm, w_out=w_out, mem_norm=mem_norm,
             norm_xa=norm_xa, w_q=w_q, w_k=w_k, w_v=w_v, w_o=w_o, norm_ffn=norm_ffn, w_gate=w_gate, w_up=w_up,
             w_down=w_down, norm_final=norm_final)
    mom = dict(norm_mix=m_norm_mix, w_in=m_w_in, ssd_conv_w=m_ssd_conv_w, ssd_conv_b=m_ssd_conv_b, dt_bias=m_dt_bias,
               a_log=m_a_log, d_skip=m_d_skip, ssd_norm=m_ssd_norm, sc_conv_w=m_sc_conv_w, sc_norm=m_sc_norm,
               w_out=m_w_out, mem_norm=m_mem_norm, norm_xa=m_norm_xa, w_q=m_w_q, w_k=m_w_k, w_v=m_w_v, w_o=m_w_o,
               norm_ffn=m_norm_ffn, w_gate=m_w_gate, w_up=m_w_up, w_down=m_w_down, norm_final=m_norm_final)
    var = dict(norm_mix=v_norm_mix, w_in=v_w_in, ssd_conv_w=v_ssd_conv_w, ssd_conv_b=v_ssd_conv_b, dt_bias=v_dt_bias,
               a_log=v_a_log, d_skip=v_d_skip, ssd_norm=v_ssd_norm, sc_conv_w=v_sc_conv_w, sc_norm=v_sc_norm,
               w_out=v_w_out, mem_norm=v_mem_norm, norm_xa=v_norm_xa, w_q=v_w_q, w_k=v_w_k, w_v=v_w_v, w_o=v_w_o,
               norm_ffn=v_norm_ffn, w_gate=v_w_gate, w_up=v_w_up, w_down=v_w_down, norm_final=v_norm_final)
    depth = w_in.shape[0]
    my = 4 * lax.axis_index("x") + 2 * lax.axis_index("y") + lax.axis_index("c")

    got = _run_side(_ag_side([w_in[0:1].astype(BF16), ssd_conv_w, sc_conv_w]), name="ag_first")
    comm = _Comm(w, got[0][0])
    conv_full = {"ssd_conv_w": _assemble(got[1], True), "sc_conv_w": _assemble(got[2], True)}
    layers, gains = [], []
    for li in range(depth):
        p = _prep_small(conv_full["ssd_conv_w"][li], ssd_conv_b[li], dt_bias[li], a_log[li], d_skip[li])
        p["sc_conv_w"] = conv_full["sc_conv_w"][li]
        layers.append(p)
        gains.append({nm: w[nm][li].reshape(1, D) for nm in ("norm_mix", "ssd_norm", "sc_norm", "norm_xa", "norm_ffn")})

    loss_v, grad_x, grads, d_mem_norm, d_final = _local_step(
        x[0], mem[0], loss_target[0], layers, gains, mem_norm.reshape(1, D), norm_final.reshape(1, D), comm)
    loss = lax.psum(loss_v[0, 0], ("x", "y", "c"))

    outs = {}
    for nm in BIG:
        shp = w[nm].shape
        flat = (shp[0] * shp[1], shp[2])
        res = _adamw(comm.got[nm], w[nm].reshape(flat), mom[nm].reshape(flat), var[nm].reshape(flat), name="adamw_" + nm)
        outs[nm] = tuple(t.reshape(shp) for t in res)

    small = SMALL_REPL + SMALL_SHARDED
    gsmall = dict(mem_norm=d_mem_norm.reshape(D), norm_final=d_final.reshape(D))
    for nm in ("norm_mix", "ssd_norm", "sc_norm", "norm_xa", "norm_ffn"):
        gsmall[nm] = jnp.stack([gr[nm].reshape(D) for gr in grads])
    for nm in ("ssd_conv_b", "dt_bias", "a_log", "d_skip", "ssd_conv_w", "sc_conv_w"):
        gsmall[nm] = jnp.stack([gr[nm] for gr in grads])
    packed_g = _pack(gsmall, small)
    all_g = _run_side(_ag_side([packed_g[None]]), name="ag_small_grads")[0][0]

    def put_shard(arrs):
        loc = {nm: arrs[nm] for nm in SMALL_REPL}
        for nm in SMALL_SHARDED:
            cs = arrs[nm].shape[-1]
            loc[nm] = lax.dynamic_update_slice_in_dim(jnp.zeros(gsmall[nm].shape, F32), arrs[nm], my * cs, axis=2)
        return _pack(loc, small)

    res = _adamw([all_g], put_shard(w), put_shard(mom), put_shard(var), name="adamw_small")
    shapes = {nm: gsmall[nm].shape for nm in small}
    for idx in range(4):
        un = _unpack(res[idx], shapes, small)
        for nm in SMALL_REPL:
            outs.setdefault(nm, [None] * 4)[idx] = un[nm]
        for nm in SMALL_SHARDED:
            cs = w[nm].shape[-1]
            outs.setdefault(nm, [None] * 4)[idx] = lax.dynamic_slice_in_dim(un[nm], my * cs, cs, axis=2)

    return (loss, grad_x[None], *[outs[nm][0] for nm in WEIGHTS], *[outs[nm][1] for nm in WEIGHTS],
            *[outs[nm][2] for nm in WEIGHTS], *[outs[nm][3] for nm in WEIGHTS])
```

```python
import functools

import jax
import jax.numpy as jnp
from jax import lax
from jax.experimental import pallas as pl
from jax.experimental.pallas import tpu as pltpu

F32 = jnp.float32
BF16 = jnp.bfloat16

D = 2048
HEAD_DIM = 64
N_HEADS = 32
N_GROUPS = 4
N_STATE = 128
CHUNK = 256
D_XBC = D + 2 * N_GROUPS * N_STATE
SSD_GROUP_W = D // N_GROUPS
SC_GROUP_W = 128
XA_HEADS = 4
XA_HD = 128
D_XA = XA_HEADS * XA_HD
D_FF = 5632
NORM_EPS = 1e-5
PROJ_W = 11264
DT_W = 128
N_DEV = 8

ADAM_LR = 0.001
ADAM_B1 = 0.9
ADAM_B2 = 0.999
ADAM_EPS = 1e-08
ADAM_WD = 0.01
ADAM_STEP = 10

TB = CHUNK
HALO = 8
LANES = 128
VMEM_LIMIT = 56 * 1024 * 1024

NT = (((1,), (1,)), ((), ()))
TN = (((0,), (0,)), ((), ()))
MESH = pl.DeviceIdType.MESH


def _cp(*sem):
    return pltpu.CompilerParams(dimension_semantics=sem, vmem_limit_bytes=VMEM_LIMIT)


def _sds(shape, dtype):
    return jax.ShapeDtypeStruct(shape, dtype)


def _sigmoid(x):
    return 1.0 / (1.0 + jnp.exp(-x))


def _split3_dot(t_bf16, x):
    hi = x.astype(BF16)
    r1 = x - hi.astype(F32)
    mid = r1.astype(BF16)
    lo = (r1 - mid.astype(F32)).astype(BF16)
    out = jnp.dot(t_bf16, hi, preferred_element_type=F32)
    out = out + jnp.dot(t_bf16, mid, preferred_element_type=F32)
    return out + jnp.dot(t_bf16, lo, preferred_element_type=F32)


def _split3_dot_r(x, t_bf16):
    hi = x.astype(BF16)
    r1 = x - hi.astype(F32)
    mid = r1.astype(BF16)
    lo = (r1 - mid.astype(F32)).astype(BF16)
    out = jnp.dot(hi, t_bf16, preferred_element_type=F32)
    out = out + jnp.dot(mid, t_bf16, preferred_element_type=F32)
    return out + jnp.dot(lo, t_bf16, preferred_element_type=F32)


def _group_bcast(stat_fn, v, gw):
    pieces = []
    for g in range(v.shape[1] // gw):
        vs = v[:, g * gw:(g + 1) * gw]
        pieces.append(jnp.broadcast_to(stat_fn(vs), vs.shape))
    return jnp.concatenate(pieces, axis=1) if len(pieces) > 1 else pieces[0]


def _group_rstd(v, gw):
    return _group_bcast(lambda s: lax.rsqrt(jnp.mean(s * s, axis=1, keepdims=True) + NORM_EPS), v, gw)


def _group_mean(v, gw):
    return _group_bcast(lambda s: jnp.mean(s, axis=1, keepdims=True), v, gw)


def _pair_sum(d, first):
    s0 = jnp.sum(jnp.where(first, d, 0.0), axis=1, keepdims=True)
    s1 = jnp.sum(jnp.where(first, 0.0, d), axis=1, keepdims=True)
    return jnp.where(first, s0, s1)


class _Side:
    def __init__(self, arrays, out_shapes, n_remote, n_local, start, finish):
        self.arrays, self.out_shapes = tuple(arrays), tuple(out_shapes)
        self.n_remote, self.n_local = n_remote, n_local
        self.start, self.finish = start, finish

    def scratch(self):
        return [pltpu.SemaphoreType.DMA((self.n_remote,)), pltpu.SemaphoreType.DMA((self.n_remote,)),
                pltpu.SemaphoreType.DMA((self.n_local,))]


def _ag_side(shards):
    n = len(shards)

    def plan(x_refs, out_refs, send, recv, local, starting=False):
        x, y, c = lax.axis_index("x"), lax.axis_index("y"), lax.axis_index("c")
        me, sibling = (x, y, c), (x, y, 1 - c)
        chips = [(1 - x, y), (x, 1 - y), (1 - x, 1 - y)]
        jobs = []
        for a in range(n):
            def rows(px, py, pc, out=out_refs[a]):
                return out.at[:, 4 * px + 2 * py + pc]

            def copy(k, block, to, src=None, a=a, rows=rows):
                return pltpu.make_async_remote_copy(
                    src_ref=rows(*block) if src is None else src, dst_ref=rows(*block),
                    send_sem=send.at[7 * a + k], recv_sem=recv.at[7 * a + k], device_id=to, device_id_type=MESH)

            mine = pltpu.make_async_copy(x_refs[a], rows(*me), local.at[a])
            first = [copy(0, me, sibling, src=x_refs[a])]
            first += [copy(1 + j, me, (*chip, c), src=x_refs[a]) for j, chip in enumerate(chips)]
            if starting:
                jobs.append((mine, first))
                continue
            passed = [copy(4 + j, (*chip, c), sibling) for j, chip in enumerate(chips)]
            arrive = [copy(1 + j, (*chip, c), me) for j, chip in enumerate(chips)]
            late = [copy(0, sibling, me)] + [copy(4 + j, (*chip, 1 - c), me) for j, chip in enumerate(chips)]
            jobs.append((mine, first, passed, arrive, late))
        return jobs

    def start(*refs):
        for mine, first in plan(*refs, starting=True):
            mine.start()
            for cp in first:
                cp.start()

    def finish(*refs):
        jobs = plan(*refs)
        for j in range(3):
            for _, _, passed, arrive, _ in jobs:
                arrive[j].wait_recv()
                passed[j].start()
        for mine, first, passed, _, late in jobs:
            for cp in late:
                cp.wait_recv()
            for cp in first + passed:
                cp.wait_send()
            mine.wait()

    outs = [_sds((s.shape[0], N_DEV) + s.shape[1:], s.dtype) for s in shards]
    return _Side(shards, outs, 7 * n, n, start, finish)


def _flip(k, x, y, c):
    return (1 - x if k & 4 else x, 1 - y if k & 2 else y, 1 - c if k & 1 else c)


def _rs_side(pieces):
    n = len(pieces)

    def plan(g_refs, out_refs, send, recv, local):
        x, y, c = lax.axis_index("x"), lax.axis_index("y"), lax.axis_index("c")
        me = 4 * x + 2 * y + c
        jobs = []
        for a in range(n):
            mine = pltpu.make_async_copy(g_refs[a].at[me], out_refs[a].at[me], local.at[a])
            copies = []
            for k in range(1, N_DEV):
                px, py, pc = _flip(k, x, y, c)
                copies.append(pltpu.make_async_remote_copy(
                    src_ref=g_refs[a].at[4 * px + 2 * py + pc], dst_ref=out_refs[a].at[me],
                    send_sem=send.at[7 * a + k - 1], recv_sem=recv.at[7 * a + k - 1],
                    device_id=(px, py, pc), device_id_type=MESH))
            jobs.append((mine, copies))
        return jobs

    def start(*refs):
        for mine, copies in plan(*refs):
            mine.start()
            for cp in copies:
                cp.start()

    def finish(*refs):
        for mine, copies in plan(*refs):
            for cp in copies:
                cp.wait_recv()
            for cp in copies:
                cp.wait_send()
            mine.wait()

    return _Side(pieces, [_sds(p.shape, p.dtype) for p in pieces], 7 * n, n, start, finish)


def _run_side(side, *, name):
    n_in, n_out = len(side.arrays), len(side.out_shapes)

    def body(*refs):
        parts = (refs[:n_in], refs[n_in:n_in + n_out]) + tuple(refs[n_in + n_out:])
        side.start(*parts)
        side.finish(*parts)

    hbm = pl.BlockSpec(memory_space=pl.ANY)
    return pl.pallas_call(
        body, name=name, out_shape=side.out_shapes, in_specs=[hbm] * n_in, out_specs=tuple([hbm] * n_out),
        scratch_shapes=side.scratch(),
    )(*side.arrays)


def _mm(a, b, *, name, ta=False, tb=False, res=None, out_dtype=F32, bm=1024, bn=1024, bk=None, j_outer=False,
        side=None):
    if ta:
        k_dim, m_dim = a.shape
    else:
        m_dim, k_dim = a.shape
    if tb:
        n_dim, kb = b.shape
    else:
        kb, n_dim = b.shape
    assert k_dim == kb, (a.shape, b.shape)
    bm, bn = min(bm, m_dim), min(bn, n_dim)
    bk = k_dim if bk is None else min(bk, k_dim)
    assert m_dim % bm == 0 and n_dim % bn == 0 and k_dim % bk == 0, (name, a.shape, b.shape, bm, bn, bk)
    ni, nj, nk = m_dim // bm, n_dim // bn, k_dim // bk
    grid = (nj, ni, nk) if j_outer else (ni, nj, nk)
    dn = (((0,) if ta else (1,), (1,) if tb else (0,)), ((), ()))
    has_res = res is not None
    n_main_in = 2 + has_res
    n_side_in = len(side.arrays) if side else 0
    n_side_out = len(side.out_shapes) if side else 0

    def body(*refs):
        a_ref, b_ref = refs[0], refs[1]
        r_ref = refs[2] if has_res else None
        o_ref = refs[n_main_in + n_side_in]
        scratch = refs[n_main_in + n_side_in + 1 + n_side_out:]
        if side:
            side_refs = (refs[n_main_in:n_main_in + n_side_in],
                         refs[n_main_in + n_side_in + 1:n_main_in + n_side_in + 1 + n_side_out]) + tuple(scratch[-3:])
            step = (pl.program_id(0) * grid[1] + pl.program_id(1)) * grid[2] + pl.program_id(2)

            @pl.when(step == 0)
            def _():
                side.start(*side_refs)

        p = lax.dot_general(a_ref[...].astype(BF16), b_ref[...].astype(BF16), dn, preferred_element_type=F32)
        if nk == 1:
            if has_res:
                p = p + r_ref[...]
            o_ref[...] = p.astype(o_ref.dtype)
        else:
            acc_ref = scratch[0]
            k = pl.program_id(2)

            @pl.when(k == 0)
            def _():
                acc_ref[...] = p

            @pl.when(k > 0)
            def _():
                acc_ref[...] += p

            @pl.when(k == nk - 1)
            def _():
                r = acc_ref[...]
                if has_res:
                    r = r + r_ref[...]
                o_ref[...] = r.astype(o_ref.dtype)

        if side:
            @pl.when(step == grid[0] * grid[1] * grid[2] - 1)
            def _():
                side.finish(*side_refs)

    def spec(shape, index):
        if j_outer:
            return pl.BlockSpec(shape, lambda j, i, k: index(i, j, k))
        return pl.BlockSpec(shape, index)

    a_spec = spec((bk, bm), lambda i, j, k: (k, i)) if ta else spec((bm, bk), lambda i, j, k: (i, k))
    b_spec = spec((bn, bk), lambda i, j, k: (j, k)) if tb else spec((bk, bn), lambda i, j, k: (k, j))
    o_spec = spec((bm, bn), lambda i, j, k: (i, j))
    hbm = pl.BlockSpec(memory_space=pl.ANY)
    in_specs = [a_spec, b_spec] + ([o_spec] if has_res else []) + [hbm] * n_side_in
    args = (a, b) + ((res,) if has_res else ()) + (side.arrays if side else ())
    scratch_shapes = ([pltpu.VMEM((bm, bn), F32)] if nk > 1 else []) + (side.scratch() if side else [])
    out_main = _sds((m_dim, n_dim), out_dtype)
    if not side:
        return pl.pallas_call(
            body, name=name, grid=grid, in_specs=in_specs, out_specs=o_spec, out_shape=out_main,
            scratch_shapes=scratch_shapes, compiler_params=_cp("parallel", "parallel", "arbitrary"),
        )(*args)
    outs = pl.pallas_call(
        body, name=name, grid=grid, in_specs=in_specs, out_specs=(o_spec,) + tuple([hbm] * n_side_out),
        out_shape=(out_main,) + side.out_shapes, scratch_shapes=scratch_shapes,
        compiler_params=_cp("arbitrary", "arbitrary", "arbitrary"),
    )(*args)
    return outs[0], tuple(outs[1:])


def _norm_fwd(h, g, *, name, out_dtype=BF16):
    s, d = h.shape
    tb = min(TB, s)

    def body(h_ref, g_ref, o_ref):
        x = h_ref[...]
        r = lax.rsqrt(jnp.mean(x * x, axis=-1, keepdims=True) + NORM_EPS)
        o_ref[...] = (x * r * g_ref[...]).astype(o_ref.dtype)

    row = pl.BlockSpec((tb, d), lambda i: (i, 0))
    return pl.pallas_call(
        body, name=name, grid=(s // tb,), in_specs=[row, pl.BlockSpec((1, d), lambda i: (0, 0))], out_specs=row,
        out_shape=_sds((s, d), out_dtype), compiler_params=_cp("parallel"),
    )(h, g)


def _norm_bwd(h, g, dhn, dres, *, name):
    s, d = h.shape
    tb = min(TB, s)
    has_res = dres is not None

    def body(*refs):
        h_ref, g_ref, dhn_ref = refs[:3]
        r_ref = refs[3] if has_res else None
        dh_ref, dhb_ref, dg_ref = refs[3 + has_res:]
        x = h_ref[...]
        r = lax.rsqrt(jnp.mean(x * x, axis=-1, keepdims=True) + NORM_EPS)
        xhat = x * r
        dy = dhn_ref[...]
        gy = dy * g_ref[...]
        dx = r * (gy - xhat * jnp.mean(gy * xhat, axis=-1, keepdims=True))
        if has_res:
            dx = dx + r_ref[...]
        dh_ref[...] = dx
        dhb_ref[...] = dx.astype(BF16)
        part = jnp.sum(dy * xhat, axis=0, keepdims=True)

        @pl.when(pl.program_id(0) == 0)
        def _():
            dg_ref[...] = part

        @pl.when(pl.program_id(0) > 0)
        def _():
            dg_ref[...] += part

    row = pl.BlockSpec((tb, d), lambda i: (i, 0))
    vec = pl.BlockSpec((1, d), lambda i: (0, 0))
    return pl.pallas_call(
        body, name=name, grid=(s // tb,), in_specs=[row, vec, row] + ([row] if has_res else []),
        out_specs=(row, row, vec), out_shape=(_sds((s, d), F32), _sds((s, d), BF16), _sds((1, d), F32)),
        compiler_params=_cp("arbitrary"),
    )(*((h, g, dhn) + ((dres,) if has_res else ())))


def _loss_head(h, g, tgt, *, name):
    s, d = h.shape
    tb = min(TB, s)

    def body(h_ref, g_ref, t_ref, loss_ref, dh_ref, dhb_ref, dg_ref):
        x = h_ref[...]
        r = lax.rsqrt(jnp.mean(x * x, axis=-1, keepdims=True) + NORM_EPS)
        xhat = x * r
        gain = g_ref[...]
        err = xhat * gain - t_ref[...]
        part_loss = 0.5 * jnp.sum(jnp.mean(err * err, axis=-1, keepdims=True), axis=0, keepdims=True)
        dy = err * (1.0 / d)
        gy = dy * gain
        dx = r * (gy - xhat * jnp.mean(gy * xhat, axis=-1, keepdims=True))
        dh_ref[...] = dx
        dhb_ref[...] = dx.astype(BF16)
        part = jnp.sum(dy * xhat, axis=0, keepdims=True)
        lossv = jnp.broadcast_to(part_loss, (1, LANES))

        @pl.when(pl.program_id(0) == 0)
        def _():
            dg_ref[...] = part
            loss_ref[...] = lossv

        @pl.when(pl.program_id(0) > 0)
        def _():
            dg_ref[...] += part
            loss_ref[...] += lossv

    row = pl.BlockSpec((tb, d), lambda i: (i, 0))
    vec = pl.BlockSpec((1, d), lambda i: (0, 0))
    return pl.pallas_call(
        body, name=name, grid=(s // tb,), in_specs=[row, vec, row],
        out_specs=(pl.BlockSpec((1, LANES), lambda i: (0, 0)), row, row, vec),
        out_shape=(_sds((1, LANES), F32), _sds((s, d), F32), _sds((s, d), BF16), _sds((1, d), F32)),
        compiler_params=_cp("arbitrary"),
    )(h, g, tgt)


def _swiglu_fwd(gu, *, name):
    s = gu.shape[0]
    tb = min(TB, s)

    def body(g_ref, u_ref, o_ref):
        g = g_ref[...].astype(F32)
        o_ref[...] = (g * _sigmoid(g) * u_ref[...].astype(F32)).astype(o_ref.dtype)

    return pl.pallas_call(
        body, name=name, grid=(s // tb,),
        in_specs=[pl.BlockSpec((tb, D_FF), lambda i: (i, 0)), pl.BlockSpec((tb, D_FF), lambda i: (i, 1))],
        out_specs=pl.BlockSpec((tb, D_FF), lambda i: (i, 0)), out_shape=_sds((s, D_FF), BF16),
        compiler_params=_cp("parallel"),
    )(gu, gu)


def _swiglu_bwd(gu, dact, *, name):
    s = gu.shape[0]
    tb = min(TB, s)

    def body(g_ref, u_ref, d_ref, o_ref):
        g = g_ref[...].astype(F32)
        sg = _sigmoid(g)
        da = d_ref[...].astype(F32)
        o_ref[:, :D_FF] = (da * u_ref[...].astype(F32) * sg * (1.0 + g * (1.0 - sg))).astype(o_ref.dtype)
        o_ref[:, D_FF:] = (da * g * sg).astype(o_ref.dtype)

    return pl.pallas_call(
        body, name=name, grid=(s // tb,),
        in_specs=[pl.BlockSpec((tb, D_FF), lambda i: (i, 0)), pl.BlockSpec((tb, D_FF), lambda i: (i, 1)),
                  pl.BlockSpec((tb, D_FF), lambda i: (i, 0))],
        out_specs=pl.BlockSpec((tb, 2 * D_FF), lambda i: (i, 0)), out_shape=_sds((s, 2 * D_FF), BF16),
        compiler_params=_cp("parallel"),
    )(gu, gu, dact)


def _softmax_rows(qh, kh):
    sc = lax.dot_general(qh, kh, NT, preferred_element_type=F32) * (XA_HD ** -0.5)
    sc = sc - jnp.max(sc, axis=-1, keepdims=True)
    e = jnp.exp(sc)
    return e / jnp.sum(e, axis=-1, keepdims=True)


def _attn_fwd(q, k, v, *, name):
    s = q.shape[0]
    n_mem = k.shape[0]
    tq = min(512, s)

    def body(q_ref, k_ref, v_ref, o_ref):
        outs = []
        for h in range(XA_HEADS):
            sl = slice(h * XA_HD, (h + 1) * XA_HD)
            p = _softmax_rows(q_ref[:, sl], k_ref[:, sl])
            outs.append(jnp.dot(p.astype(BF16), v_ref[:, sl], preferred_element_type=F32))
        o_ref[...] = jnp.concatenate(outs, axis=1).astype(o_ref.dtype)

    row = pl.BlockSpec((tq, D_XA), lambda i: (i, 0))
    kv = pl.BlockSpec((n_mem, D_XA), lambda i: (0, 0))
    return pl.pallas_call(
        body, name=name, grid=(s // tq,), in_specs=[row, kv, kv], out_specs=row, out_shape=_sds((s, D_XA), BF16),
        compiler_params=_cp("parallel"),
    )(q, k, v)


def _attn_bwd(q, k, v, do, *, name):
    s = q.shape[0]
    n_mem = k.shape[0]
    tq = min(512, s)

    def body(q_ref, k_ref, v_ref, do_ref, dq_ref, dk_ref, dv_ref):
        dqs, dks, dvs = [], [], []
        for h in range(XA_HEADS):
            sl = slice(h * XA_HD, (h + 1) * XA_HD)
            qh, kh, vh, doh = q_ref[:, sl], k_ref[:, sl], v_ref[:, sl], do_ref[:, sl]
            p = _softmax_rows(qh, kh)
            dvs.append(lax.dot_general(p.astype(BF16), doh, TN, preferred_element_type=F32))
            dp = lax.dot_general(doh, vh, NT, preferred_element_type=F32)
            ds = (p * (dp - jnp.sum(dp * p, axis=-1, keepdims=True)) * (XA_HD ** -0.5)).astype(BF16)
            dqs.append(jnp.dot(ds, kh, preferred_element_type=F32))
            dks.append(lax.dot_general(ds, qh, TN, preferred_element_type=F32))
        dq_ref[...] = jnp.concatenate(dqs, axis=1).astype(dq_ref.dtype)
        dk = jnp.concatenate(dks, axis=1)
        dv = jnp.concatenate(dvs, axis=1)

        @pl.when(pl.program_id(0) == 0)
        def _():
            dk_ref[...] = dk
            dv_ref[...] = dv

        @pl.when(pl.program_id(0) > 0)
        def _():
            dk_ref[...] += dk
            dv_ref[...] += dv

    row = pl.BlockSpec((tq, D_XA), lambda i: (i, 0))
    kv = pl.BlockSpec((n_mem, D_XA), lambda i: (0, 0))
    return pl.pallas_call(
        body, name=name, grid=(s // tq,), in_specs=[row, kv, kv, row], out_specs=(row, kv, kv),
        out_shape=(_sds((s, D_XA), BF16), _sds((n_mem, D_XA), F32), _sds((n_mem, D_XA), F32)),
        compiler_params=_cp("arbitrary"),
    )(q, k, v, do)


CONV_CB = 1024
XBC_CB0 = D // CONV_CB


def _prev_rows(i, tb):
    return jnp.maximum(i * (tb // HALO) - 1, 0)


def _next_rows(i, tb, s):
    return jnp.minimum((i + 1) * (tb // HALO), s // HALO - 1)


def _taps(xcat, width):
    return [pltpu.roll(xcat, width - 1 - k, 0) for k in range(width - 1)] + [xcat]


def _conv_taps(taps, w_ref, lo):
    acc = w_ref[0:1, :] * taps[0][lo:]
    for k in range(1, len(taps)):
        acc = acc + w_ref[k:k + 1, :] * taps[k][lo:]
    return acc


def _conv_silu_fwd(proj, w, b, *, name):
    s = proj.shape[0]
    tb = min(TB, s)

    def body(x_ref, xp_ref, w_ref, b_ref, o_ref):
        i = pl.program_id(0)
        prev = jnp.where(i > 0, xp_ref[...], 0.0)
        pre = _conv_taps(_taps(jnp.concatenate([prev, x_ref[...]], axis=0), 4), w_ref, HALO) + b_ref[...]
        o_ref[...] = pre * _sigmoid(pre)

    return pl.pallas_call(
        body, name=name, grid=(s // tb, D_XBC // CONV_CB),
        in_specs=[pl.BlockSpec((tb, CONV_CB), lambda i, j: (i, XBC_CB0 + j)),
                  pl.BlockSpec((HALO, CONV_CB), lambda i, j: (_prev_rows(i, tb), XBC_CB0 + j)),
                  pl.BlockSpec((4, CONV_CB), lambda i, j: (0, j)),
                  pl.BlockSpec((1, CONV_CB), lambda i, j: (0, j))],
        out_specs=pl.BlockSpec((tb, CONV_CB), lambda i, j: (i, j)), out_shape=_sds((s, D_XBC), F32),
        compiler_params=_cp("parallel", "parallel"),
    )(proj, proj, w, b)


def _conv_silu_bwd(proj, dy, w, b, *, name):
    s = proj.shape[0]
    tb = min(TB, s)
    n_i = s // tb

    def body(x_ref, xp_ref, xn_ref, dy_ref, dyn_ref, w_ref, b_ref, dx_ref, dw_ref, db_ref):
        i = pl.program_id(1)
        prev = jnp.where(i > 0, xp_ref[...], 0.0)
        taps = _taps(jnp.concatenate([prev, x_ref[...], xn_ref[...]], axis=0), 4)
        pre = _conv_taps(taps, w_ref, HALO) + b_ref[...]
        sg = _sigmoid(pre)
        dy_ext = jnp.concatenate([dy_ref[...], jnp.where(i < n_i - 1, dyn_ref[...], 0.0)], axis=0)
        dpre = dy_ext * sg * (1.0 + pre * (1.0 - sg))
        n2 = tb + HALO
        dx = w_ref[3:4, :] * dpre[:tb]
        for k in range(3):
            dx = dx + w_ref[k:k + 1, :] * pltpu.roll(dpre, n2 - (3 - k), 0)[:tb]
        dx_ref[...] = dx.astype(dx_ref.dtype)
        dpc = dpre[:tb]
        parts = [jnp.sum(dpc * taps[k][HALO:HALO + tb], axis=0, keepdims=True) for k in range(4)]
        dbp = jnp.sum(dpc, axis=0, keepdims=True)

        @pl.when(i == 0)
        def _():
            for k in range(4):
                dw_ref[k] = parts[k]
            db_ref[...] = dbp

        @pl.when(i > 0)
        def _():
            for k in range(4):
                dw_ref[k] += parts[k]
            db_ref[...] += dbp

    return pl.pallas_call(
        body, name=name, grid=(D_XBC // CONV_CB, n_i),
        in_specs=[pl.BlockSpec((tb, CONV_CB), lambda j, i: (i, XBC_CB0 + j)),
                  pl.BlockSpec((HALO, CONV_CB), lambda j, i: (_prev_rows(i, tb), XBC_CB0 + j)),
                  pl.BlockSpec((HALO, CONV_CB), lambda j, i: (_next_rows(i, tb, s), XBC_CB0 + j)),
                  pl.BlockSpec((tb, CONV_CB), lambda j, i: (i, j)),
                  pl.BlockSpec((HALO, CONV_CB), lambda j, i: (_next_rows(i, tb, s), j)),
                  pl.BlockSpec((4, CONV_CB), lambda j, i: (0, j)),
                  pl.BlockSpec((1, CONV_CB), lambda j, i: (0, j))],
        out_specs=(pl.BlockSpec((tb, CONV_CB), lambda j, i: (i, j)),
                   pl.BlockSpec((4, 1, CONV_CB), lambda j, i: (0, 0, j)),
                   pl.BlockSpec((1, CONV_CB), lambda j, i: (0, j))),
        out_shape=(_sds((s, D_XBC), BF16), _sds((4, 1, D_XBC), F32), _sds((1, D_XBC), F32)),
        compiler_params=_cp("parallel", "arbitrary"),
    )(proj, proj, proj, dy, dy, w, b)


SC_U0 = (D + D_XBC) // CONV_CB
SC_B0 = SC_U0 + D // CONV_CB
SC_C0 = SC_B0 + D // CONV_CB


def _sc_fwd(proj, w, gain, *, name):
    s = proj.shape[0]
    tb = min(TB, s)

    def body(u_ref, b_ref, c_ref, up_ref, cp_ref, w_ref, g_ref, o_ref):
        i = pl.program_id(0)
        cup = jnp.where(i > 0, cp_ref[...] * up_ref[...], 0.0)
        cat = jnp.concatenate([cup, c_ref[...] * u_ref[...]], axis=0)
        v = b_ref[...] * _conv_taps(_taps(cat, 3), w_ref, HALO)
        o_ref[...] = (v * _group_rstd(v, SC_GROUP_W) * g_ref[...]).astype(o_ref.dtype)

    def cur(c0):
        return pl.BlockSpec((tb, CONV_CB), lambda i, j: (i, c0 + j))

    def prev(c0):
        return pl.BlockSpec((HALO, CONV_CB), lambda i, j: (_prev_rows(i, tb), c0 + j))

    return pl.pallas_call(
        body, name=name, grid=(s // tb, D // CONV_CB),
        in_specs=[cur(SC_U0), cur(SC_B0), cur(SC_C0), prev(SC_U0), prev(SC_C0),
                  pl.BlockSpec((3, CONV_CB), lambda i, j: (0, j)), pl.BlockSpec((1, CONV_CB), lambda i, j: (0, j))],
        out_specs=pl.BlockSpec((tb, CONV_CB), lambda i, j: (i, j)), out_shape=_sds((s, D), BF16),
        compiler_params=_cp("parallel", "parallel"),
    )(proj, proj, proj, proj, proj, w, gain)


def _sc_bwd(proj, dcat, w, gain, *, name):
    s = proj.shape[0]
    tb = min(TB, s)
    n_i = s // tb
    dy0 = D // CONV_CB

    def body(u_ref, b_ref, c_ref, up_ref, cp_ref, un_ref, bn_ref, cn_ref, dy_ref, dyn_ref, w_ref, g_ref,
             du_ref, db_ref, dc_ref, dw_ref, dg_ref):
        i = pl.program_id(1)
        u, c = u_ref[...], c_ref[...]
        cup = jnp.where(i > 0, cp_ref[...] * up_ref[...], 0.0)
        taps = _taps(jnp.concatenate([cup, c * u, cn_ref[...] * un_ref[...]], axis=0), 3)
        conv = _conv_taps(taps, w_ref, HALO)
        b_ext = jnp.concatenate([b_ref[...], bn_ref[...]], axis=0)
        dy_ext = jnp.concatenate([dy_ref[...], jnp.where(i < n_i - 1, dyn_ref[...], 0.0)], axis=0)
        v = b_ext * conv
        r = _group_rstd(v, SC_GROUP_W)
        vhat = v * r
        dvhat = dy_ext * g_ref[...]
        dv = r * (dvhat - vhat * _group_mean(dvhat * vhat, SC_GROUP_W))
        dconv = dv * b_ext
        n2 = tb + HALO
        dcu = (w_ref[2:3, :] * dconv[:tb] + w_ref[1:2, :] * pltpu.roll(dconv, n2 - 1, 0)[:tb]
               + w_ref[0:1, :] * pltpu.roll(dconv, n2 - 2, 0)[:tb])
        du_ref[...] = (dcu * c).astype(du_ref.dtype)
        dc_ref[...] = (dcu * u).astype(dc_ref.dtype)
        db_ref[...] = (dv * conv)[:tb].astype(db_ref.dtype)
        dcc = dconv[:tb]
        parts = [jnp.sum(dcc * taps[k][HALO:HALO + tb], axis=0, keepdims=True) for k in range(3)]
        dgp = jnp.sum((dy_ext * vhat)[:tb], axis=0, keepdims=True)

        @pl.when(i == 0)
        def _():
            for k in range(3):
                dw_ref[k] = parts[k]
            dg_ref[...] = dgp

        @pl.when(i > 0)
        def _():
            for k in range(3):
                dw_ref[k] += parts[k]
            dg_ref[...] += dgp

    def cur(c0):
        return pl.BlockSpec((tb, CONV_CB), lambda j, i: (i, c0 + j))

    def prev(c0):
        return pl.BlockSpec((HALO, CONV_CB), lambda j, i: (_prev_rows(i, tb), c0 + j))

    def nxt(c0):
        return pl.BlockSpec((HALO, CONV_CB), lambda j, i: (_next_rows(i, tb, s), c0 + j))

    vec = pl.BlockSpec((1, CONV_CB), lambda j, i: (0, j))
    out_row = pl.BlockSpec((tb, CONV_CB), lambda j, i: (i, j))
    return pl.pallas_call(
        body, name=name, grid=(D // CONV_CB, n_i),
        in_specs=[cur(SC_U0), cur(SC_B0), cur(SC_C0), prev(SC_U0), prev(SC_C0), nxt(SC_U0), nxt(SC_B0), nxt(SC_C0),
                  cur(dy0), nxt(dy0), pl.BlockSpec((3, CONV_CB), lambda j, i: (0, j)), vec],
        out_specs=(out_row, out_row, out_row, pl.BlockSpec((3, 1, CONV_CB), lambda j, i: (0, 0, j)), vec),
        out_shape=(_sds((s, D), BF16), _sds((s, D), BF16), _sds((s, D), BF16), _sds((3, 1, D), F32), _sds((1, D), F32)),
        compiler_params=_cp("parallel", "arbitrary"),
    )(proj, proj, proj, proj, proj, proj, proj, proj, dcat, dcat, w, gain)


def _gated_norm_fwd(y, proj, gain, *, name):
    s = y.shape[0]
    tb = min(TB, s)

    def body(y_ref, z_ref, g_ref, o_ref):
        z = z_ref[...]
        t = y_ref[...] * z * _sigmoid(z)
        o_ref[...] = (t * _group_rstd(t, SSD_GROUP_W) * g_ref[...]).astype(o_ref.dtype)

    row = pl.BlockSpec((tb, D), lambda i: (i, 0))
    return pl.pallas_call(
        body, name=name, grid=(s // tb,), in_specs=[row, row, pl.BlockSpec((1, D), lambda i: (0, 0))], out_specs=row,
        out_shape=_sds((s, D), BF16), compiler_params=_cp("parallel"),
    )(y, proj, gain)


def _gated_norm_bwd(y, proj, dcat, gain, *, name):
    s = y.shape[0]
    tb = min(TB, s)

    def body(y_ref, z_ref, d_ref, g_ref, dy_ref, dz_ref, dg_ref):
        z, yv, dout = z_ref[...], y_ref[...], d_ref[...]
        sg = _sigmoid(z)
        sz = z * sg
        t = yv * sz
        r = _group_rstd(t, SSD_GROUP_W)
        that = t * r
        dthat = dout * g_ref[...]
        dt = r * (dthat - that * _group_mean(dthat * that, SSD_GROUP_W))
        dy_ref[...] = dt * sz
        dz_ref[...] = (dt * yv * sg * (1.0 + z * (1.0 - sg))).astype(dz_ref.dtype)
        part = jnp.sum(dout * that, axis=0, keepdims=True)

        @pl.when(pl.program_id(0) == 0)
        def _():
            dg_ref[...] = part

        @pl.when(pl.program_id(0) > 0)
        def _():
            dg_ref[...] += part

    row = pl.BlockSpec((tb, D), lambda i: (i, 0))
    vec = pl.BlockSpec((1, D), lambda i: (0, 0))
    return pl.pallas_call(
        body, name=name, grid=(s // tb,), in_specs=[row, row, row, vec], out_specs=(row, row, vec),
        out_shape=(_sds((s, D), F32), _sds((s, D), BF16), _sds((1, D), F32)), compiler_params=_cp("arbitrary"),
    )(y, proj, dcat, gain)


def _tri(lower):
    row = lax.broadcasted_iota(jnp.int32, (CHUNK, CHUNK), 0)
    col = lax.broadcasted_iota(jnp.int32, (CHUNK, CHUNK), 1)
    return jnp.where(row >= col if lower else col >= row, 1.0, 0.0).astype(BF16)


def _head_spread():
    row = lax.broadcasted_iota(jnp.int32, (DT_W, D), 0)
    col = lax.broadcasted_iota(jnp.int32, (DT_W, D), 1)
    return jnp.where(col // HEAD_DIM == row, 1.0, 0.0).astype(BF16)


def _head_pick():
    row = lax.broadcasted_iota(jnp.int32, (D, DT_W), 0)
    col = lax.broadcasted_iota(jnp.int32, (D, DT_W), 1)
    return jnp.where(row == col * HEAD_DIM, 1.0, 0.0).astype(BF16)


def _dt_fwd(dt_raw, bias, a_log, *, name):
    s = dt_raw.shape[0]
    nc = s // CHUNK

    def body(raw_ref, bias_ref, alog_ref, dt_ref, cum_ref, cumt_ref):
        x = raw_ref[...] + bias_ref[...]
        dt = jnp.maximum(x, 0.0) + jnp.log1p(jnp.exp(-jnp.abs(x)))
        cum = _split3_dot(_tri(True), dt * (-jnp.exp(alog_ref[...])))
        spread = _head_spread()
        dt_ref[...] = _split3_dot_r(dt, spread)
        cum_ref[...] = _split3_dot_r(cum, spread)
        cumt_ref[0] = cum.T

    row = pl.BlockSpec((CHUNK, DT_W), lambda i: (i, 0))
    wide = pl.BlockSpec((CHUNK, D), lambda i: (i, 0))
    vec = pl.BlockSpec((1, DT_W), lambda i: (0, 0))
    return pl.pallas_call(
        body, name=name, grid=(nc,), in_specs=[row, vec, vec],
        out_specs=(wide, wide, pl.BlockSpec((1, DT_W, CHUNK), lambda i: (i, 0, 0))),
        out_shape=(_sds((s, D), F32), _sds((s, D), F32), _sds((nc, DT_W, CHUNK), F32)),
        compiler_params=_cp("parallel"),
    )(dt_raw, bias, a_log)


def _dt_bwd(dt_raw, bias, ddt_b, *, name):
    s = dt_raw.shape[0]
    tb = min(TB, s)

    def body(raw_ref, bias_ref, d_ref, o_ref, db_ref):
        g = _split3_dot_r(d_ref[...], _head_pick()) * _sigmoid(raw_ref[...] + bias_ref[...])
        o_ref[...] = g.astype(o_ref.dtype)
        part = jnp.sum(g, axis=0, keepdims=True)

        @pl.when(pl.program_id(0) == 0)
        def _():
            db_ref[...] = part

        @pl.when(pl.program_id(0) > 0)
        def _():
            db_ref[...] += part

    row = pl.BlockSpec((tb, DT_W), lambda i: (i, 0))
    vec = pl.BlockSpec((1, DT_W), lambda i: (0, 0))
    return pl.pallas_call(
        body, name=name, grid=(s // tb,), in_specs=[row, vec, pl.BlockSpec((tb, D), lambda i: (i, 0))],
        out_specs=(row, vec), out_shape=(_sds((s, DT_W), BF16), _sds((1, DT_W), F32)),
        compiler_params=_cp("arbitrary"),
    )(dt_raw, bias, ddt_b)


N_PAIRS = N_HEADS // 2
PAIRS_PER_GROUP = N_PAIRS // N_GROUPS
GROUP_W = PAIRS_PER_GROUP * LANES
B_CB0 = D // LANES
C_CB0 = B_CB0 + N_GROUPS


def _decay(cum_col, cum_row, causal):
    return jnp.where(causal, jnp.exp(jnp.minimum(cum_col - cum_row, 0.0)), 0.0)


def _causal_mask():
    row = lax.broadcasted_iota(jnp.int32, (CHUNK, CHUNK), 0)
    col = lax.broadcasted_iota(jnp.int32, (CHUNK, CHUNK), 1)
    return row >= col


def _state_row_scale(cumt_ref, pp):
    last0 = cumt_ref[0, 2 * pp][:, CHUNK - 1:CHUNK]
    last1 = cumt_ref[0, 2 * pp + 1][:, CHUNK - 1:CHUNK]
    rown = lax.broadcasted_iota(jnp.int32, (LANES, 1), 0)
    return jnp.exp(jnp.where(rown < HEAD_DIM, last0, last1))


def _ssd_specs(nc, rev):
    def ch(c):
        return nc - 1 - c if rev else c

    wide = pl.BlockSpec((CHUNK, GROUP_W), lambda c, g: (ch(c), g))
    vec = pl.BlockSpec((1, GROUP_W), lambda c, g: (0, g))
    cumt_spec = pl.BlockSpec((1, 2 * PAIRS_PER_GROUP, 1, CHUNK), lambda c, g: (ch(c), g, 0, 0))
    hp_spec = pl.BlockSpec((1, PAIRS_PER_GROUP, LANES, N_STATE), lambda c, g: (ch(c), g, 0, 0))

    def bc(c0):
        return pl.BlockSpec((CHUNK, LANES), lambda c, g: (ch(c), c0 + g))

    return wide, vec, cumt_spec, hp_spec, bc


def _ssd_fwd(xbc_c, dt_b, cum_b, cumt4, dskip_b, *, name):
    s = xbc_c.shape[0]
    nc = s // CHUNK

    def body(xs_ref, b_ref, c_ref, dt_ref, cum_ref, cumt_ref, dsk_ref, y_ref, hp_ref, state):
        c, g = pl.program_id(0), pl.program_id(1)

        @pl.when(c == 0)
        def _():
            state[pl.ds(g * PAIRS_PER_GROUP, PAIRS_PER_GROUP)] = jnp.zeros((PAIRS_PER_GROUP, LANES, N_STATE), F32)

        bb, cbm = b_ref[...].astype(BF16), c_ref[...].astype(BF16)
        cbv = lax.dot_general(cbm, bb, NT, preferred_element_type=F32)
        first = lax.broadcasted_iota(jnp.int32, (CHUNK, LANES), 1) < HEAD_DIM
        causal = _causal_mask()
        for pp in range(PAIRS_PER_GROUP):
            sl = slice(pp * LANES, (pp + 1) * LANES)
            xs, cum = xs_ref[:, sl], cum_ref[:, sl]
            xt = xs * dt_ref[:, sl]
            xtb = xt.astype(BF16)
            hp = state[g * PAIRS_PER_GROUP + pp]
            hp_ref[0, pp] = hp
            y = jnp.exp(cum) * lax.dot_general(cbm, hp.astype(BF16), NT, preferred_element_type=F32)
            for hh in range(2):
                lm = _decay(cum[:, hh * HEAD_DIM:hh * HEAD_DIM + 1], cumt_ref[0, 2 * pp + hh], causal)
                xm = jnp.where(first if hh == 0 else jnp.logical_not(first), xtb, jnp.zeros_like(xtb))
                y = y + jnp.dot((cbv * lm).astype(BF16), xm, preferred_element_type=F32)
            decs = jnp.exp(cum[CHUNK - 1:CHUNK, :] - cum)
            st = lax.dot_general((xt * decs).astype(BF16), bb, TN, preferred_element_type=F32)
            state[g * PAIRS_PER_GROUP + pp] = _state_row_scale(cumt_ref, pp) * hp + st
            y_ref[:, sl] = y + xs * dsk_ref[:, sl]

    wide, vec, cumt_spec, hp_spec, bc = _ssd_specs(nc, False)
    return pl.pallas_call(
        body, name=name, grid=(nc, N_GROUPS),
        in_specs=[wide, bc(B_CB0), bc(C_CB0), wide, wide, cumt_spec, vec],
        out_specs=(wide, hp_spec),
        out_shape=(_sds((s, D), F32), _sds((nc, N_PAIRS, LANES, N_STATE), F32)),
        scratch_shapes=[pltpu.VMEM((N_PAIRS, LANES, N_STATE), F32)],
        compiler_params=_cp("arbitrary", "arbitrary"),
    )(xbc_c, xbc_c, xbc_c, dt_b, cum_b, cumt4, dskip_b)


def _ssd_bwd(xbc_c, dt_b, cum_b, cumt4, hprev, dy, alog_b, dskip_b, *, name):
    s = xbc_c.shape[0]
    nc = s // CHUNK

    def body(xs_ref, b_ref, c_ref, dt_ref, cum_ref, cumt_ref, hp_ref, dy_ref, alog_ref, dsk_ref,
             dxs_ref, db_ref, dc_ref, ddt_ref, dalog_ref, ddsk_ref, dstate):
        c, g = pl.program_id(0), pl.program_id(1)
        pairs = pl.ds(g * PAIRS_PER_GROUP, PAIRS_PER_GROUP)

        @pl.when(c == 0)
        def _():
            dstate[pairs] = jnp.zeros((PAIRS_PER_GROUP, LANES, N_STATE), F32)
            dalog_ref[pairs] = jnp.zeros((PAIRS_PER_GROUP, 1, LANES), F32)
            ddsk_ref[pairs] = jnp.zeros((PAIRS_PER_GROUP, 1, LANES), F32)

        bb, cbm = b_ref[...].astype(BF16), c_ref[...].astype(BF16)
        cbv = lax.dot_general(cbm, bb, NT, preferred_element_type=F32)
        first = lax.broadcasted_iota(jnp.int32, (CHUNK, LANES), 1) < HEAD_DIM
        causal = _causal_mask()
        rown = lax.broadcasted_iota(jnp.int32, (LANES, 1), 0)
        d_b = jnp.zeros((CHUNK, N_STATE), F32)
        d_c = jnp.zeros((CHUNK, N_STATE), F32)
        for pp in range(PAIRS_PER_GROUP):
            sl = slice(pp * LANES, (pp + 1) * LANES)
            idx = g * PAIRS_PER_GROUP + pp
            xs, dtb, cum, dy = xs_ref[:, sl], dt_ref[:, sl], cum_ref[:, sl], dy_ref[:, sl]
            a_b = -jnp.exp(alog_ref[:, sl])
            xt = xs * dtb
            xtb = xt.astype(BF16)
            dyb = dy.astype(BF16)
            hp = hp_ref[0, pp]
            hpb = hp.astype(BF16)
            dh = dstate[idx]
            dhb = dh.astype(BF16)
            exp_cum = jnp.exp(cum)
            decs = jnp.exp(cum[CHUNK - 1:CHUNK, :] - cum)
            row_scale = _state_row_scale(cumt_ref, pp)
            xd = (xt * decs).astype(BF16)
            dye = (dy * exp_cum).astype(BF16)
            zero_b = jnp.zeros_like(xtb)

            h_next = row_scale * hp + lax.dot_general(xd, bb, TN, preferred_element_type=F32)
            y = exp_cum * lax.dot_general(cbm, hpb, NT, preferred_element_type=F32)
            dxt = decs * lax.dot_general(bb, dhb, NT, preferred_element_type=F32)
            dcb = jnp.zeros((CHUNK, CHUNK), F32)
            for hh in range(2):
                mask = first if hh == 0 else jnp.logical_not(first)
                lm = _decay(cum[:, hh * HEAD_DIM:hh * HEAD_DIM + 1], cumt_ref[0, 2 * pp + hh], causal)
                m = (cbv * lm).astype(BF16)
                y = y + jnp.dot(m, jnp.where(mask, xtb, zero_b), preferred_element_type=F32)
                dxt = dxt + jnp.where(mask, lax.dot_general(m, dyb, TN, preferred_element_type=F32), 0.0)
                dm = lax.dot_general(jnp.where(mask, dyb, zero_b), xtb, NT, preferred_element_type=F32)
                dcb = dcb + dm * lm
            dcbb = dcb.astype(BF16)
            d_c = d_c + jnp.dot(dcbb, bb, preferred_element_type=F32) + jnp.dot(dye, hpb, preferred_element_type=F32)
            d_b = (d_b + lax.dot_general(dcbb, cbm, TN, preferred_element_type=F32)
                   + jnp.dot(xd, dhb, preferred_element_type=F32))
            dstate[idx] = row_scale * dh + lax.dot_general(dye, cbm, TN, preferred_element_type=F32)

            d_cum = _pair_sum(dyb.astype(F32) * y - dxt * xtb.astype(F32), first)
            e = jnp.sum(dh * h_next, axis=1, keepdims=True)
            t0 = jnp.sum(jnp.where(rown < HEAD_DIM, e, 0.0), axis=0, keepdims=True)
            t1 = jnp.sum(jnp.where(rown < HEAD_DIM, 0.0, e), axis=0, keepdims=True)
            d_da = _split3_dot(_tri(False), d_cum) + jnp.where(first[0:1, :], t0, t1)
            ddt_ref[:, sl] = a_b * d_da + _pair_sum(dxt * xs, first)
            dxs_ref[:, sl] = dxt * dtb + dy * dsk_ref[:, sl]
            dalog_ref[idx] += jnp.sum(d_da * dtb * a_b, axis=0, keepdims=True)
            ddsk_ref[idx] += jnp.sum(_pair_sum(dy * xs, first), axis=0, keepdims=True)
        db_ref[...] = d_b
        dc_ref[...] = d_c

    wide, vec, cumt_spec, hp_spec, bc = _ssd_specs(nc, True)
    acc = pl.BlockSpec((N_PAIRS, 1, LANES), lambda c, g: (0, 0, 0))
    return pl.pallas_call(
        body, name=name, grid=(nc, N_GROUPS),
        in_specs=[wide, bc(B_CB0), bc(C_CB0), wide, wide, cumt_spec, hp_spec, wide, vec, vec],
        out_specs=(wide, bc(0), bc(0), wide, acc, acc),
        out_shape=(_sds((s, D), F32), _sds((s, N_GROUPS * N_STATE), F32), _sds((s, N_GROUPS * N_STATE), F32),
                   _sds((s, D), F32), _sds((N_PAIRS, 1, LANES), F32), _sds((N_PAIRS, 1, LANES), F32)),
        scratch_shapes=[pltpu.VMEM((N_PAIRS, LANES, N_STATE), F32)],
        compiler_params=_cp("arbitrary", "arbitrary"),
    )(xbc_c, xbc_c, xbc_c, dt_b, cum_b, cumt4, hprev, dy, alog_b, dskip_b)


def _lane_bcast(v):
    return jnp.repeat(v, HEAD_DIM, axis=1)


DT0 = D + D_XBC
W_IN_SHARD = (DT0 + N_HEADS + 3 * D) // N_DEV
DT_PIECE = DT0 // W_IN_SHARD
DT_OFF = DT0 - DT_PIECE * W_IN_SHARD
assert DT_OFF + N_HEADS <= W_IN_SHARD


def _split_w_in(got):
    hold = got[DT_PIECE]
    cols = ([got[d] for d in range(DT_PIECE)] + [hold[:, :DT_OFF], hold[:, DT_OFF + N_HEADS:]]
            + [got[d] for d in range(DT_PIECE + 1, N_DEV)])
    return (jnp.concatenate(cols, axis=1),
            jnp.pad(hold[:, DT_OFF:DT_OFF + N_HEADS], ((0, 0), (0, DT_W - N_HEADS))))


def _w_in_pieces(dw_main, dw_dt):
    pieces = []
    for d in range(N_DEV):
        lo = d * W_IN_SHARD
        if d < DT_PIECE:
            pieces.append(dw_main[:, lo:lo + W_IN_SHARD])
        elif d == DT_PIECE:
            pieces.append(jnp.concatenate(
                [dw_main[:, lo:DT0], dw_dt[:, :N_HEADS], dw_main[:, DT0:lo + W_IN_SHARD - N_HEADS]], axis=1))
        else:
            pieces.append(dw_main[:, lo - N_HEADS:lo - N_HEADS + W_IN_SHARD])
    return jnp.stack(pieces)


def _prep_small(conv_w, conv_b, dt_bias, a_log, d_skip):
    pad = DT_W - N_HEADS
    return dict(
        conv_w=conv_w, conv_b=conv_b.reshape(1, D_XBC),
        dt_bias=jnp.pad(dt_bias.reshape(1, N_HEADS), ((0, 0), (0, pad))),
        a_log=jnp.pad(a_log.reshape(1, N_HEADS), ((0, 0), (0, pad))),
        alog_b=_lane_bcast(a_log.reshape(1, N_HEADS)), dskip_b=_lane_bcast(d_skip.reshape(1, N_HEADS)))


def _layer_fwd(h0, memn, p, g, li, comm=None):
    s = h0.shape[0]
    nc = s // CHUNK
    n = f"l{li}_"

    def hosted(what, *args, **kw):
        side = comm.ag_side(what, li) if comm is not None else None
        if side is None:
            return _mm(*args, **kw)
        out, got = _mm(*args, side=side, **kw)
        comm.after_ag(what, li, p, got)
        return out

    if comm is not None:
        comm.before_layer(li, p)
    hn = _norm_fwd(h0, g["norm_mix"], name=n + "norm_mix")
    proj = hosted("proj", hn, p["w_main"], name=n + "proj")
    dt_raw = _mm(hn, p["w_dt"], name=n + "proj_dt")
    xbc_c = _conv_silu_fwd(proj, p["conv_w"], p["conv_b"], name=n + "conv")
    dt_b, cum_b, cumt = _dt_fwd(dt_raw, p["dt_bias"], p["a_log"], name=n + "dt")
    cumt4 = cumt[:, :N_HEADS, :].reshape(nc, N_HEADS, 1, CHUNK)
    y, hprev = _ssd_fwd(xbc_c, dt_b, cum_b, cumt4, p["dskip_b"], name=n + "ssd")
    y_ssd = _gated_norm_fwd(y, proj, g["ssd_norm"], name=n + "gnorm")
    y_sc = _sc_fwd(proj, p["sc_conv_w"], g["sc_norm"], name=n + "sc")
    cat = jnp.concatenate([y_ssd, y_sc], axis=1)
    h1 = hosted("out", cat, p["w_out"], res=h0, bm=512, j_outer=True, name=n + "out")
    hx = _norm_fwd(h1, g["norm_xa"], name=n + "norm_xa")
    q = _mm(hx, p["w_q"], out_dtype=BF16, name=n + "q")
    k = _mm(memn, p["w_k"], out_dtype=BF16, name=n + "k")
    v = _mm(memn, p["w_v"], out_dtype=BF16, name=n + "v")
    o = _attn_fwd(q, k, v, name=n + "attn")
    h2 = _mm(o, p["w_o"], res=h1, name=n + "o")
    hf = _norm_fwd(h2, g["norm_ffn"], name=n + "norm_ffn")
    gu = hosted("gu", hf, p["w_gu"], out_dtype=BF16, name=n + "gu")
    act = _swiglu_fwd(gu, name=n + "swiglu")
    h3 = hosted("down", act, p["w_down"], res=h2, bm=512, j_outer=True, name=n + "down")
    saved = dict(h0=h0, hn=hn, proj=proj, dt_raw=dt_raw, xbc_c=xbc_c, dt_b=dt_b, cum_b=cum_b, cumt4=cumt4, hprev=hprev,
                 y=y, cat=cat, h1=h1, hx=hx, q=q, k=k, v=v, o=o, h2=h2, hf=hf, gu=gu, act=act)
    return h3, saved


def _layer_bwd(dh, dhb, dmemn, memn, p, g, sv, li, comm=None):
    n = f"l{li}b_"
    gr = {}

    def hosted(names, *args, **kw):
        side = comm.grad_side(li, names, gr) if comm is not None else None
        if side is None:
            return _mm(*args, **kw)
        out, got = _mm(*args, side=side, **kw)
        comm.after_grads(li, names, got)
        return out

    wide = dict(bm=512, bn=512)
    dact = _mm(dhb, p["w_down"], tb=True, bn=512, out_dtype=BF16, name=n + "dact")
    gr["w_down"] = hosted(("carry",), sv["act"], dhb, ta=True, out_dtype=BF16, name=n + "dw_down", **wide)
    dgu = _swiglu_bwd(sv["gu"], dact, name=n + "swiglu")
    dw_gu = hosted(("w_down",), sv["hf"], dgu, ta=True, out_dtype=BF16, name=n + "dw_gu", **wide)
    gr["w_gate"], gr["w_up"] = dw_gu[:, :D_FF], dw_gu[:, D_FF:]
    dhf = hosted(("w_gate",), dgu, p["w_gu"], tb=True, j_outer=True, name=n + "dhf", **wide)
    dh, dhb, gr["norm_ffn"] = _norm_bwd(sv["h2"], g["norm_ffn"], dhf, dh, name=n + "norm_ffn")
    do = _mm(dhb, p["w_o"], tb=True, out_dtype=BF16, name=n + "do")
    gr["w_o"] = _mm(sv["o"], dhb, ta=True, out_dtype=BF16, name=n + "dw_o", **wide)
    dq, dk, dv = _attn_bwd(sv["q"], sv["k"], sv["v"], do, name=n + "attn")
    gr["w_q"] = _mm(sv["hx"], dq, ta=True, out_dtype=BF16, name=n + "dw_q", **wide)
    gr["w_k"] = _mm(memn, dk, ta=True, out_dtype=BF16, name=n + "dw_k")
    gr["w_v"] = _mm(memn, dv, ta=True, out_dtype=BF16, name=n + "dw_v")
    dmemn = _mm(dk, p["w_k"], tb=True, res=dmemn, name=n + "dmem_k")
    dmemn = _mm(dv, p["w_v"], tb=True, res=dmemn, name=n + "dmem_v")
    dhx = _mm(dq, p["w_q"], tb=True, name=n + "dhx")
    dh, dhb, gr["norm_xa"] = _norm_bwd(sv["h1"], g["norm_xa"], dhx, dh, name=n + "norm_xa")
    dcat = hosted(("w_q", "w_k", "w_v", "w_o"), dhb, p["w_out"], tb=True, name=n + "dcat")
    gr["w_out"] = _mm(sv["cat"], dhb, ta=True, out_dtype=BF16, name=n + "dw_out", **wide)
    du, dgb, dgc, dsc_w, gr["sc_norm"] = _sc_bwd(sv["proj"], dcat, p["sc_conv_w"], g["sc_norm"], name=n + "sc")
    gr["sc_conv_w"] = dsc_w.reshape(3, D)
    dy, dz, gr["ssd_norm"] = _gated_norm_bwd(sv["y"], sv["proj"], dcat, g["ssd_norm"], name=n + "gnorm")
    dxs, d_b, d_c, ddt_b, dalog, ddsk = _ssd_bwd(sv["xbc_c"], sv["dt_b"], sv["cum_b"], sv["cumt4"], sv["hprev"], dy,
                                                 p["alog_b"], p["dskip_b"], name=n + "ssd")
    gr["a_log"] = dalog.reshape(N_HEADS, HEAD_DIM)[:, 0]
    gr["d_skip"] = ddsk.reshape(N_HEADS, HEAD_DIM)[:, 0]
    dxbc_c = jnp.concatenate([dxs, d_b, d_c], axis=1)
    dxbc, dconv_w, dconv_b = _conv_silu_bwd(sv["proj"], dxbc_c, p["conv_w"], p["conv_b"], name=n + "conv")
    gr["ssd_conv_w"] = dconv_w.reshape(4, D_XBC)
    gr["ssd_conv_b"] = dconv_b.reshape(D_XBC)
    ddt_raw, ddt_bias = _dt_bwd(sv["dt_raw"], p["dt_bias"], ddt_b, name=n + "dt")
    gr["dt_bias"] = ddt_bias[0, :N_HEADS]
    dproj = jnp.concatenate([dz, dxbc, du, dgb, dgc], axis=1)
    dw_main = hosted(("w_up", "w_out"), sv["hn"], dproj, ta=True, out_dtype=BF16, name=n + "dw_main", **wide)
    dw_dt = _mm(sv["hn"], ddt_raw, ta=True, out_dtype=BF16, name=n + "dw_dt", **wide)
    gr["w_in"] = [_w_in_pieces(dw_main[rows], dw_dt[rows]) for rows in (slice(0, D // 2), slice(D // 2, D))]
    dhn = hosted(("w_in",), dproj, p["w_main"], tb=True, j_outer=True, name=n + "dhn", **wide)
    dhn = _mm(ddt_raw, p["w_dt"], tb=True, res=dhn, name=n + "dhn_dt")
    dh, dhb, gr["norm_mix"] = _norm_bwd(sv["h0"], g["norm_mix"], dhn, dh, name=n + "norm_mix")
    return dh, dhb, dmemn, gr


def _local_step(x, mem, tgt, layers, gains, mem_norm, norm_final, comm=None):
    depth = len(layers)
    memn_f = _norm_fwd(mem, mem_norm, out_dtype=F32, name="mem_norm")
    memn = memn_f.astype(BF16)
    h = x
    saved = []
    for li in range(depth):
        h, sv = _layer_fwd(h, memn, layers[li], gains[li], li, comm)
        saved.append(sv)
    loss, dh, dhb, d_final = _loss_head(h, norm_final, tgt, name="loss_head")
    dmemn = jnp.zeros(mem.shape, F32)
    grads = [None] * depth
    for li in reversed(range(depth)):
        dh, dhb, dmemn, grads[li] = _layer_bwd(dh, dhb, dmemn, memn, layers[li], gains[li], saved[li], li, comm)
    _, _, d_mem_norm = _norm_bwd(mem, mem_norm, dmemn, None, name="mem_norm_b")
    return loss, dh, grads, d_mem_norm, d_final


ADAMW_ROWS = (64, 32, 16, 8)


def _adamw(parts, w, m, v, *, name):
    depth = len(parts)
    r, c_dim = parts[0].shape[1:]
    assert w.shape == (depth * r, c_dim), (name, w.shape, parts[0].shape)
    tr = next(t for t in ADAMW_ROWS if r % t == 0)
    nb = r // tr

    def body(*refs):
        p_refs = refs[:depth]
        w_ref, m_ref, v_ref, g_ref, d_ref, nm_ref, nv_ref = refs[depth:]
        for l in range(depth):
            @pl.when(pl.program_id(0) == l)
            def _(l=l):
                g = p_refs[l][0].astype(F32)
                for s in range(1, N_DEV):
                    g = g + p_refs[l][s].astype(F32)
                g_ref[...] = g

        g = g_ref[...]
        m2 = ADAM_B1 * m_ref[...] + (1.0 - ADAM_B1) * g
        v2 = ADAM_B2 * v_ref[...] + (1.0 - ADAM_B2) * (g * g)
        m_hat = m2 / (1.0 - ADAM_B1 ** ADAM_STEP)
        v_hat = v2 / (1.0 - ADAM_B2 ** ADAM_STEP)
        d_ref[...] = -ADAM_LR * (m_hat / (jnp.sqrt(v_hat) + ADAM_EPS) + ADAM_WD * w_ref[...])
        nm_ref[...] = m2
        nv_ref[...] = v2

    def part_spec(l):
        return pl.BlockSpec((N_DEV, tr, c_dim),
                            lambda lay, i: (0, jnp.where(lay == l, i, jnp.where(lay < l, 0, nb - 1)), 0))

    row = pl.BlockSpec((tr, c_dim), lambda lay, i: (lay * nb + i, 0))
    out = _sds((depth * r, c_dim), F32)
    return pl.pallas_call(
        body, name=name, grid=(depth, nb), in_specs=[part_spec(l) for l in range(depth)] + [row, row, row],
        out_specs=(row, row, row, row), out_shape=(out, out, out, out), compiler_params=_cp("arbitrary", "arbitrary"),
    )(*parts, w, m, v)


BIG = ("w_in", "w_out", "w_q", "w_k", "w_v", "w_o", "w_gate", "w_up", "w_down")
COL_SHARDED = ("w_in", "w_o", "w_gate", "w_up")
PROJ_HOSTED = ("w_out", "w_q", "w_k", "w_v", "w_o", "w_gate")
SMALL_REPL = ("norm_mix", "ssd_conv_b", "dt_bias", "a_log", "d_skip", "ssd_norm", "sc_norm", "mem_norm", "norm_xa",
              "norm_ffn", "norm_final")
SMALL_SHARDED = ("ssd_conv_w", "sc_conv_w")
WEIGHTS = ("norm_mix", "w_in", "ssd_conv_w", "ssd_conv_b", "dt_bias", "a_log", "d_skip", "ssd_norm", "sc_conv_w",
           "sc_norm", "w_out", "mem_norm", "norm_xa", "w_q", "w_k", "w_v", "w_o", "norm_ffn", "w_gate", "w_up",
           "w_down", "norm_final")


def _assemble(got, col_sharded):
    l, _, r, c_dim = got.shape
    if col_sharded:
        return jnp.transpose(got, (0, 2, 1, 3)).reshape(l, r, N_DEV * c_dim)
    return got.reshape(l, N_DEV * r, c_dim)


def _to_pieces(full, col_sharded):
    rr, cc = full.shape
    if col_sharded:
        return jnp.transpose(full.reshape(rr, N_DEV, cc // N_DEV), (1, 0, 2))
    return full.reshape(N_DEV, rr // N_DEV, cc)


class _Comm:
    def __init__(self, w, first_w_in):
        self.w = w
        self.depth = w["w_in"].shape[0]
        self.next_w_in = first_w_in
        self.next_w_down = None
        self.gate = None
        self.carry = None
        self.got = {nm: [None] * self.depth for nm in BIG}
        self.got["w_in"] = [None] * (2 * self.depth)

    def _wanted(self, what, li):
        more = li + 1 < self.depth
        if what == "proj":
            return [(nm, li) for nm in PROJ_HOSTED]
        if what == "out":
            return [("w_up", li)]
        if what == "gu":
            return ([("w_in", li + 1)] if more else []) + ([("w_down", 0)] if li == 0 else [])
        return [("w_down", li + 1)] if more else []

    def before_layer(self, li, p):
        p["w_main"], p["w_dt"] = _split_w_in(self.next_w_in)
        if li > 0:
            p["w_down"] = self.next_w_down

    def ag_side(self, what, li):
        wanted = self._wanted(what, li)
        return _ag_side([self.w[nm][l:l + 1].astype(BF16) for nm, l in wanted]) if wanted else None

    def after_ag(self, what, li, p, got):
        for (nm, l), g in zip(self._wanted(what, li), got):
            if nm == "w_in":
                self.next_w_in = g[0]
                continue
            full = _assemble(g, nm in COL_SHARDED)[0]
            if nm == "w_gate":
                self.gate = full
            elif nm == "w_up":
                p["w_gu"] = jnp.concatenate([self.gate, full], axis=1)
            elif nm == "w_down" and l > li:
                self.next_w_down = full
            else:
                p[nm] = full

    def _grad_jobs(self, li, names):
        jobs = []
        for nm in names:
            if nm == "w_in":
                jobs += [("w_in", 2 * li)] + ([("w_in", 1)] if li == 0 else [])
            elif nm == "carry":
                jobs += [("w_in", 2 * li + 3)] if li + 1 < self.depth else []
            else:
                jobs.append((nm, li))
        return jobs

    def grad_side(self, li, names, gr):
        pieces = []
        for nm, slot in self._grad_jobs(li, names):
            if nm != "w_in":
                pieces.append(_to_pieces(gr[nm], nm in COL_SHARDED))
            elif slot == 2 * li + 3:
                pieces.append(self.carry)
            else:
                pieces.append(gr["w_in"][slot - 2 * li])
        if "w_in" in names and li > 0:
            self.carry = gr["w_in"][1]
        return _rs_side(pieces) if pieces else None

    def after_grads(self, li, names, got):
        for (nm, slot), g in zip(self._grad_jobs(li, names), got):
            self.got[nm][slot] = g


def _pack(arrs, names):
    flat = jnp.concatenate([arrs[nm].reshape(-1).astype(F32) for nm in names])
    rows = -(-flat.shape[0] // (TB * LANES)) * TB
    return jnp.pad(flat, (0, rows * LANES - flat.shape[0])).reshape(rows, LANES)


def _unpack(packed, shapes, names):
    flat = packed.reshape(-1)
    out, off = {}, 0
    for nm in names:
        size = 1
        for dim in shapes[nm]:
            size *= dim
        out[nm] = flat[off:off + size].reshape(shapes[nm])
        off += size
    return out


def kernel(x, mem, norm_mix, w_in, ssd_conv_w, ssd_conv_b, dt_bias, a_log, d_skip, ssd_norm, sc_conv_w, sc_norm, w_out, mem_norm, norm_xa, w_q, w_k, w_v, w_o, norm_ffn, w_gate, w_up, w_down, norm_final, loss_target, m_norm_mix, m_w_in, m_ssd_conv_w, m_ssd_conv_b, m_dt_bias, m_a_log, m_d_skip, m_ssd_norm, m_sc_conv_w, m_sc_norm, m_w_out, m_mem_norm, m_norm_xa, m_w_q, m_w_k, m_w_v, m_w_o, m_norm_ffn, m_w_gate, m_w_up, m_w_down, m_norm_final, v_norm_mix, v_w_in, v_ssd_conv_w, v_ssd_conv_b, v_dt_bias, v_a_log, v_d_skip, v_ssd_norm, v_sc_conv_w, v_sc_norm, v_w_out, v_mem_norm, v_norm_xa, v_w_q, v_w_k, v_w_v, v_w_o, v_norm_ffn, v_w_gate, v_w_up, v_w_down, v_norm_final):
    w = dict(norm_mix=norm_mix, w_in=w_in, ssd_conv_w=ssd_conv_w, ssd_conv_b=ssd_conv_b, dt_bias=dt_bias, a_log=a_log,
             d_skip=d_skip, ssd_norm=ssd_norm, sc_conv_w=sc_conv_w, sc_norm=sc_norm, w_out=w_out, mem_norm=mem_norm,
             norm_xa=norm_xa, w_q=w_q, w_k=w_k, w_v=w_v, w_o=w_o, norm_ffn=norm_ffn, w_gate=w_gate, w_up=w_up,
             w_down=w_down, norm_final=norm_final)
    mom = dict(norm_mix=m_norm_mix, w_in=m_w_in, ssd_conv_w=m_ssd_conv_w, ssd_conv_b=m_ssd_conv_b, dt_bias=m_dt_bias,
               a_log=m_a_log, d_skip=m_d_skip, ssd_norm=m_ssd_norm, sc_conv_w=m_sc_conv_w, sc_norm=m_sc_norm,
               w_out=m_w_out, mem_norm=m_mem_norm, norm_xa=m_norm_xa, w_q=m_w_q, w_k=m_w_k, w_v=m_w_v, w_o=m_w_o,
               norm_ffn=m_norm_ffn, w_gate=m_w_gate, w_up=m_w_up, w_down=m_w_down, norm_final=m_norm_final)
    var = dict(norm_mix=v_norm_mix, w_in=v_w_in, ssd_conv_w=v_ssd_conv_w, ssd_conv_b=v_ssd_conv_b, dt_bias=v_dt_bias,
               a_log=v_a_log, d_skip=v_d_skip, ssd_norm=v_ssd_norm, sc_conv_w=v_sc_conv_w, sc_norm=v_sc_norm,
               w_out=v_w_out, mem_norm=v_mem_norm, norm_xa=v_norm_xa, w_q=v_w_q, w_k=v_w_k, w_v=v_w_v, w_o=v_w_o,
               norm_ffn=v_norm_ffn, w_gate=v_w_gate, w_up=v_w_up, w_down=v_w_down, norm_final=v_norm_final)
    depth = w_in.shape[0]
    my = 4 * lax.axis_index("x") + 2 * lax.axis_index("y") + lax.axis_index("c")

    got = _run_side(_ag_side([w_in[0:1].astype(BF16), ssd_conv_w, sc_conv_w]), name="ag_first")
    comm = _Comm(w, got[0][0])
    conv_full = {"ssd_conv_w": _assemble(got[1], True), "sc_conv_w": _assemble(got[2], True)}
    layers, gains = [], []
    for li in range(depth):
        p = _prep_small(conv_full["ssd_conv_w"][li], ssd_conv_b[li], dt_bias[li], a_log[li], d_skip[li])
        p["sc_conv_w"] = conv_full["sc_conv_w"][li]
        layers.append(p)
        gains.append({nm: w[nm][li].reshape(1, D) for nm in ("norm_mix", "ssd_norm", "sc_norm", "norm_xa", "norm_ffn")})

    loss_v, grad_x, grads, d_mem_norm, d_final = _local_step(
        x[0], mem[0], loss_target[0], layers, gains, mem_norm.reshape(1, D), norm_final.reshape(1, D), comm)
    loss = lax.psum(loss_v[0, 0], ("x", "y", "c"))

    outs = {}
    for nm in BIG:
        shp = w[nm].shape
        flat = (shp[0] * shp[1], shp[2])
        res = _adamw(comm.got[nm], w[nm].reshape(flat), mom[nm].reshape(flat), var[nm].reshape(flat), name="adamw_" + nm)
        outs[nm] = tuple(t.reshape(shp) for t in res)

    small = SMALL_REPL + SMALL_SHARDED
    gsmall = dict(mem_norm=d_mem_norm.reshape(D), norm_final=d_final.reshape(D))
    for nm in ("norm_mix", "ssd_norm", "sc_norm", "norm_xa", "norm_ffn"):
        gsmall[nm] = jnp.stack([gr[nm].reshape(D) for gr in grads])
    for nm in ("ssd_conv_b", "dt_bias", "a_log", "d_skip", "ssd_conv_w", "sc_conv_w"):
        gsmall[nm] = jnp.stack([gr[nm] for gr in grads])
    packed_g = _pack(gsmall, small)
    all_g = _run_side(_ag_side([packed_g[None]]), name="ag_small_grads")[0][0]

    def put_shard(arrs):
        loc = {nm: arrs[nm] for nm in SMALL_REPL}
        for nm in SMALL_SHARDED:
            cs = arrs[nm].shape[-1]
            loc[nm] = lax.dynamic_update_slice_in_dim(jnp.zeros(gsmall[nm].shape, F32), arrs[nm], my * cs, axis=2)
        return _pack(loc, small)

    res = _adamw([all_g], put_shard(w), put_shard(mom), put_shard(var), name="adamw_small")
    shapes = {nm: gsmall[nm].shape for nm in small}
    for idx in range(4):
        un = _unpack(res[idx], shapes, small)
        for nm in SMALL_REPL:
            outs.setdefault(nm, [None] * 4)[idx] = un[nm]
        for nm in SMALL_SHARDED:
            cs = w[nm].shape[-1]
            outs.setdefault(nm, [None] * 4)[idx] = lax.dynamic_slice_in_dim(un[nm], my * cs, cs, axis=2)

    return (loss, grad_x[None], *[outs[nm][0] for nm in WEIGHTS], *[outs[nm][1] for nm in WEIGHTS],
            *[outs[nm][2] for nm in WEIGHTS], *[outs[nm][3] for nm in WEIGHTS])
```

```python
import functools

import jax
import jax.numpy as jnp
from jax import lax
from jax.experimental import pallas as pl
from jax.experimental.pallas import tpu as pltpu

F32 = jnp.float32
BF16 = jnp.bfloat16

D = 2048
HEAD_DIM = 64
N_HEADS = 32
N_GROUPS = 4
N_STATE = 128
CHUNK = 256
D_XBC = D + 2 * N_GROUPS * N_STATE
SSD_GROUP_W = D // N_GROUPS
SC_GROUP_W = 128
XA_HEADS = 4
XA_HD = 128
D_XA = XA_HEADS * XA_HD
D_FF = 5632
NORM_EPS = 1e-5
PROJ_W = 11264
DT_W = 128
N_DEV = 8

ADAM_LR = 0.001
ADAM_B1 = 0.9
ADAM_B2 = 0.999
ADAM_EPS = 1e-08
ADAM_WD = 0.01
ADAM_STEP = 10

TB = CHUNK
HALO = 8
LANES = 128
VMEM_LIMIT = 56 * 1024 * 1024

NT = (((1,), (1,)), ((), ()))
TN = (((0,), (0,)), ((), ()))
MESH = pl.DeviceIdType.MESH


def _cp(*sem):
    return pltpu.CompilerParams(dimension_semantics=sem, vmem_limit_bytes=VMEM_LIMIT)


def _sds(shape, dtype):
    return jax.ShapeDtypeStruct(shape, dtype)


def _sigmoid(x):
    return 1.0 / (1.0 + jnp.exp(-x))


def _split3_dot(t_bf16, x):
    hi = x.astype(BF16)
    r1 = x - hi.astype(F32)
    mid = r1.astype(BF16)
    lo = (r1 - mid.astype(F32)).astype(BF16)
    out = jnp.dot(t_bf16, hi, preferred_element_type=F32)
    out = out + jnp.dot(t_bf16, mid, preferred_element_type=F32)
    return out + jnp.dot(t_bf16, lo, preferred_element_type=F32)


def _split3_dot_r(x, t_bf16):
    hi = x.astype(BF16)
    r1 = x - hi.astype(F32)
    mid = r1.astype(BF16)
    lo = (r1 - mid.astype(F32)).astype(BF16)
    out = jnp.dot(hi, t_bf16, preferred_element_type=F32)
    out = out + jnp.dot(mid, t_bf16, preferred_element_type=F32)
    return out + jnp.dot(lo, t_bf16, preferred_element_type=F32)


def _group_bcast(stat_fn, v, gw):
    pieces = []
    for g in range(v.shape[1] // gw):
        vs = v[:, g * gw:(g + 1) * gw]
        pieces.append(jnp.broadcast_to(stat_fn(vs), vs.shape))
    return jnp.concatenate(pieces, axis=1) if len(pieces) > 1 else pieces[0]


def _group_rstd(v, gw):
    return _group_bcast(lambda s: lax.rsqrt(jnp.mean(s * s, axis=1, keepdims=True) + NORM_EPS), v, gw)


def _group_mean(v, gw):
    return _group_bcast(lambda s: jnp.mean(s, axis=1, keepdims=True), v, gw)


def _pair_sum(d, first):
    s0 = jnp.sum(jnp.where(first, d, 0.0), axis=1, keepdims=True)
    s1 = jnp.sum(jnp.where(first, 0.0, d), axis=1, keepdims=True)
    return jnp.where(first, s0, s1)


class _Side:
    def __init__(self, arrays, out_shapes, n_remote, n_local, start, finish):
        self.arrays, self.out_shapes = tuple(arrays), tuple(out_shapes)
        self.n_remote, self.n_local = n_remote, n_local
        self.start, self.finish = start, finish

    def scratch(self):
        return [pltpu.SemaphoreType.DMA((self.n_remote,)), pltpu.SemaphoreType.DMA((self.n_remote,)),
                pltpu.SemaphoreType.DMA((self.n_local,))]


def _ag_side(shards):
    n = len(shards)

    def plan(x_refs, out_refs, send, recv, local, starting=False):
        x, y, c = lax.axis_index("x"), lax.axis_index("y"), lax.axis_index("c")
        me, sibling = (x, y, c), (x, y, 1 - c)
        chips = [(1 - x, y), (x, 1 - y), (1 - x, 1 - y)]
        jobs = []
        for a in range(n):
            def rows(px, py, pc, out=out_refs[a]):
                return out.at[:, 4 * px + 2 * py + pc]

            def copy(k, block, to, src=None, a=a, rows=rows):
                return pltpu.make_async_remote_copy(
                    src_ref=rows(*block) if src is None else src, dst_ref=rows(*block),
                    send_sem=send.at[7 * a + k], recv_sem=recv.at[7 * a + k], device_id=to, device_id_type=MESH)

            mine = pltpu.make_async_copy(x_refs[a], rows(*me), local.at[a])
            first = [copy(0, me, sibling, src=x_refs[a])]
            first += [copy(1 + j, me, (*chip, c), src=x_refs[a]) for j, chip in enumerate(chips)]
            if starting:
                jobs.append((mine, first))
                continue
            passed = [copy(4 + j, (*chip, c), sibling) for j, chip in enumerate(chips)]
            arrive = [copy(1 + j, (*chip, c), me) for j, chip in enumerate(chips)]
            late = [copy(0, sibling, me)] + [copy(4 + j, (*chip, 1 - c), me) for j, chip in enumerate(chips)]
            jobs.append((mine, first, passed, arrive, late))
        return jobs

    def start(*refs):
        for mine, first in plan(*refs, starting=True):
            mine.start()
            for cp in first:
                cp.start()

    def finish(*refs):
        jobs = plan(*refs)
        for j in range(3):
            for _, _, passed, arrive, _ in jobs:
                arrive[j].wait_recv()
                passed[j].start()
        for mine, first, passed, _, late in jobs:
            for cp in late:
                cp.wait_recv()
            for cp in first + passed:
                cp.wait_send()
            mine.wait()

    outs = [_sds((s.shape[0], N_DEV) + s.shape[1:], s.dtype) for s in shards]
    return _Side(shards, outs, 7 * n, n, start, finish)


def _flip(k, x, y, c):
    return (1 - x if k & 4 else x, 1 - y if k & 2 else y, 1 - c if k & 1 else c)


def _rs_side(pieces):
    n = len(pieces)

    def plan(g_refs, out_refs, send, recv, local):
        x, y, c = lax.axis_index("x"), lax.axis_index("y"), lax.axis_index("c")
        me = 4 * x + 2 * y + c
        jobs = []
        for a in range(n):
            mine = pltpu.make_async_copy(g_refs[a].at[me], out_refs[a].at[me], local.at[a])
            copies = []
            for k in range(1, N_DEV):
                px, py, pc = _flip(k, x, y, c)
                copies.append(pltpu.make_async_remote_copy(
                    src_ref=g_refs[a].at[4 * px + 2 * py + pc], dst_ref=out_refs[a].at[me],
                    send_sem=send.at[7 * a + k - 1], recv_sem=recv.at[7 * a + k - 1],
                    device_id=(px, py, pc), device_id_type=MESH))
            jobs.append((mine, copies))
        return jobs

    def start(*refs):
        for mine, copies in plan(*refs):
            mine.start()
            for cp in copies:
                cp.start()

    def finish(*refs):
        for mine, copies in plan(*refs):
            for cp in copies:
                cp.wait_recv()
            for cp in copies:
                cp.wait_send()
            mine.wait()

    return _Side(pieces, [_sds(p.shape, p.dtype) for p in pieces], 7 * n, n, start, finish)


def _run_side(side, *, name):
    n_in, n_out = len(side.arrays), len(side.out_shapes)

    def body(*refs):
        parts = (refs[:n_in], refs[n_in:n_in + n_out]) + tuple(refs[n_in + n_out:])
        side.start(*parts)
        side.finish(*parts)

    hbm = pl.BlockSpec(memory_space=pl.ANY)
    return pl.pallas_call(
        body, name=name, out_shape=side.out_shapes, in_specs=[hbm] * n_in, out_specs=tuple([hbm] * n_out),
        scratch_shapes=side.scratch(),
    )(*side.arrays)


def _mm(a, b, *, name, ta=False, tb=False, res=None, out_dtype=F32, bm=1024, bn=1024, bk=None, j_outer=False,
        side=None):
    if ta:
        k_dim, m_dim = a.shape
    else:
        m_dim, k_dim = a.shape
    if tb:
        n_dim, kb = b.shape
    else:
        kb, n_dim = b.shape
    assert k_dim == kb, (a.shape, b.shape)
    bm, bn = min(bm, m_dim), min(bn, n_dim)
    bk = k_dim if bk is None else min(bk, k_dim)
    assert m_dim % bm == 0 and n_dim % bn == 0 and k_dim % bk == 0, (name, a.shape, b.shape, bm, bn, bk)
    ni, nj, nk = m_dim // bm, n_dim // bn, k_dim // bk
    grid = (nj, ni, nk) if j_outer else (ni, nj, nk)
    dn = (((0,) if ta else (1,), (1,) if tb else (0,)), ((), ()))
    has_res = res is not None
    n_main_in = 2 + has_res
    n_side_in = len(side.arrays) if side else 0
    n_side_out = len(side.out_shapes) if side else 0

    def body(*refs):
        a_ref, b_ref = refs[0], refs[1]
        r_ref = refs[2] if has_res else None
        o_ref = refs[n_main_in + n_side_in]
        scratch = refs[n_main_in + n_side_in + 1 + n_side_out:]
        if side:
            side_refs = (refs[n_main_in:n_main_in + n_side_in],
                         refs[n_main_in + n_side_in + 1:n_main_in + n_side_in + 1 + n_side_out]) + tuple(scratch[-3:])
            step = (pl.program_id(0) * grid[1] + pl.program_id(1)) * grid[2] + pl.program_id(2)

            @pl.when(step == 0)
            def _():
                side.start(*side_refs)

        p = lax.dot_general(a_ref[...].astype(BF16), b_ref[...].astype(BF16), dn, preferred_element_type=F32)
        if nk == 1:
            if has_res:
                p = p + r_ref[...]
            o_ref[...] = p.astype(o_ref.dtype)
        else:
            acc_ref = scratch[0]
            k = pl.program_id(2)

            @pl.when(k == 0)
            def _():
                acc_ref[...] = p

            @pl.when(k > 0)
            def _():
                acc_ref[...] += p

            @pl.when(k == nk - 1)
            def _():
                r = acc_ref[...]
                if has_res:
                    r = r + r_ref[...]
                o_ref[...] = r.astype(o_ref.dtype)

        if side:
            @pl.when(step == grid[0] * grid[1] * grid[2] - 1)
            def _():
                side.finish(*side_refs)

    def spec(shape, index):
        if j_outer:
            return pl.BlockSpec(shape, lambda j, i, k: index(i, j, k))
        return pl.BlockSpec(shape, index)

    a_spec = spec((bk, bm), lambda i, j, k: (k, i)) if ta else spec((bm, bk), lambda i, j, k: (i, k))
    b_spec = spec((bn, bk), lambda i, j, k: (j, k)) if tb else spec((bk, bn), lambda i, j, k: (k, j))
    o_spec = spec((bm, bn), lambda i, j, k: (i, j))
    hbm = pl.BlockSpec(memory_space=pl.ANY)
    in_specs = [a_spec, b_spec] + ([o_spec] if has_res else []) + [hbm] * n_side_in
    args = (a, b) + ((res,) if has_res else ()) + (side.arrays if side else ())
    scratch_shapes = ([pltpu.VMEM((bm, bn), F32)] if nk > 1 else []) + (side.scratch() if side else [])
    out_main = _sds((m_dim, n_dim), out_dtype)
    if not side:
        return pl.pallas_call(
            body, name=name, grid=grid, in_specs=in_specs, out_specs=o_spec, out_shape=out_main,
            scratch_shapes=scratch_shapes, compiler_params=_cp("parallel", "parallel", "arbitrary"),
        )(*args)
    outs = pl.pallas_call(
        body, name=name, grid=grid, in_specs=in_specs, out_specs=(o_spec,) + tuple([hbm] * n_side_out),
        out_shape=(out_main,) + side.out_shapes, scratch_shapes=scratch_shapes,
        compiler_params=_cp("arbitrary", "arbitrary", "arbitrary"),
    )(*args)
    return outs[0], tuple(outs[1:])


def _norm_fwd(h, g, *, name, out_dtype=BF16):
    s, d = h.shape
    tb = min(TB, s)

    def body(h_ref, g_ref, o_ref):
        x = h_ref[...]
        r = lax.rsqrt(jnp.mean(x * x, axis=-1, keepdims=True) + NORM_EPS)
        o_ref[...] = (x * r * g_ref[...]).astype(o_ref.dtype)

    row = pl.BlockSpec((tb, d), lambda i: (i, 0))
    return pl.pallas_call(
        body, name=name, grid=(s // tb,), in_specs=[row, pl.BlockSpec((1, d), lambda i: (0, 0))], out_specs=row,
        out_shape=_sds((s, d), out_dtype), compiler_params=_cp("parallel"),
    )(h, g)


def _norm_bwd(h, g, dhn, dres, *, name):
    s, d = h.shape
    tb = min(TB, s)
    has_res = dres is not None

    def body(*refs):
        h_ref, g_ref, dhn_ref = refs[:3]
        r_ref = refs[3] if has_res else None
        dh_ref, dhb_ref, dg_ref = refs[3 + has_res:]
        x = h_ref[...]
        r = lax.rsqrt(jnp.mean(x * x, axis=-1, keepdims=True) + NORM_EPS)
        xhat = x * r
        dy = dhn_ref[...]
        gy = dy * g_ref[...]
        dx = r * (gy - xhat * jnp.mean(gy * xhat, axis=-1, keepdims=True))
        if has_res:
            dx = dx + r_ref[...]
        dh_ref[...] = dx
        dhb_ref[...] = dx.astype(BF16)
        part = jnp.sum(dy * xhat, axis=0, keepdims=True)

        @pl.when(pl.program_id(0) == 0)
        def _():
            dg_ref[...] = part

        @pl.when(pl.program_id(0) > 0)
        def _():
            dg_ref[...] += part

    row = pl.BlockSpec((tb, d), lambda i: (i, 0))
    vec = pl.BlockSpec((1, d), lambda i: (0, 0))
    return pl.pallas_call(
        body, name=name, grid=(s // tb,), in_specs=[row, vec, row] + ([row] if has_res else []),
        out_specs=(row, row, vec), out_shape=(_sds((s, d), F32), _sds((s, d), BF16), _sds((1, d), F32)),
        compiler_params=_cp("arbitrary"),
    )(*((h, g, dhn) + ((dres,) if has_res else ())))


def _loss_head(h, g, tgt, *, name):
    s, d = h.shape
    tb = min(TB, s)

    def body(h_ref, g_ref, t_ref, loss_ref, dh_ref, dhb_ref, dg_ref):
        x = h_ref[...]
        r = lax.rsqrt(jnp.mean(x * x, axis=-1, keepdims=True) + NORM_EPS)
        xhat = x * r
        gain = g_ref[...]
        err = xhat * gain - t_ref[...]
        part_loss = 0.5 * jnp.sum(jnp.mean(err * err, axis=-1, keepdims=True), axis=0, keepdims=True)
        dy = err * (1.0 / d)
        gy = dy * gain
        dx = r * (gy - xhat * jnp.mean(gy * xhat, axis=-1, keepdims=True))
        dh_ref[...] = dx
        dhb_ref[...] = dx.astype(BF16)
        part = jnp.sum(dy * xhat, axis=0, keepdims=True)
        lossv = jnp.broadcast_to(part_loss, (1, LANES))

        @pl.when(pl.program_id(0) == 0)
        def _():
            dg_ref[...] = part
            loss_ref[...] = lossv

        @pl.when(pl.program_id(0) > 0)
        def _():
            dg_ref[...] += part
            loss_ref[...] += lossv

    row = pl.BlockSpec((tb, d), lambda i: (i, 0))
    vec = pl.BlockSpec((1, d), lambda i: (0, 0))
    return pl.pallas_call(
        body, name=name, grid=(s // tb,), in_specs=[row, vec, row],
        out_specs=(pl.BlockSpec((1, LANES), lambda i: (0, 0)), row, row, vec),
        out_shape=(_sds((1, LANES), F32), _sds((s, d), F32), _sds((s, d), BF16), _sds((1, d), F32)),
        compiler_params=_cp("arbitrary"),
    )(h, g, tgt)


def _swiglu_fwd(gu, *, name):
    s = gu.shape[0]
    tb = min(TB, s)

    def body(g_ref, u_ref, o_ref):
        g = g_ref[...].astype(F32)
        o_ref[...] = (g * _sigmoid(g) * u_ref[...].astype(F32)).astype(o_ref.dtype)

    return pl.pallas_call(
        body, name=name, grid=(s // tb,),
        in_specs=[pl.BlockSpec((tb, D_FF), lambda i: (i, 0)), pl.BlockSpec((tb, D_FF), lambda i: (i, 1))],
        out_specs=pl.BlockSpec((tb, D_FF), lambda i: (i, 0)), out_shape=_sds((s, D_FF), BF16),
        compiler_params=_cp("parallel"),
    )(gu, gu)


def _swiglu_bwd(gu, dact, *, name):
    s = gu.shape[0]
    tb = min(TB, s)

    def body(g_ref, u_ref, d_ref, o_ref):
        g = g_ref[...].astype(F32)
        sg = _sigmoid(g)
        da = d_ref[...].astype(F32)
        o_ref[:, :D_FF] = (da * u_ref[...].astype(F32) * sg * (1.0 + g * (1.0 - sg))).astype(o_ref.dtype)
        o_ref[:, D_FF:] = (da * g * sg).astype(o_ref.dtype)

    return pl.pallas_call(
        body, name=name, grid=(s // tb,),
        in_specs=[pl.BlockSpec((tb, D_FF), lambda i: (i, 0)), pl.BlockSpec((tb, D_FF), lambda i: (i, 1)),
                  pl.BlockSpec((tb, D_FF), lambda i: (i, 0))],
        out_specs=pl.BlockSpec((tb, 2 * D_FF), lambda i: (i, 0)), out_shape=_sds((s, 2 * D_FF), BF16),
        compiler_params=_cp("parallel"),
    )(gu, gu, dact)


def _softmax_rows(qh, kh):
    sc = lax.dot_general(qh, kh, NT, preferred_element_type=F32) * (XA_HD ** -0.5)
    sc = sc - jnp.max(sc, axis=-1, keepdims=True)
    e = jnp.exp(sc)
    return e / jnp.sum(e, axis=-1, keepdims=True)


def _attn_fwd(q, k, v, *, name):
    s = q.shape[0]
    n_mem = k.shape[0]
    tq = min(512, s)

    def body(q_ref, k_ref, v_ref, o_ref):
        outs = []
        for h in range(XA_HEADS):
            sl = slice(h * XA_HD, (h + 1) * XA_HD)
            p = _softmax_rows(q_ref[:, sl], k_ref[:, sl])
            outs.append(jnp.dot(p.astype(BF16), v_ref[:, sl], preferred_element_type=F32))
        o_ref[...] = jnp.concatenate(outs, axis=1).astype(o_ref.dtype)

    row = pl.BlockSpec((tq, D_XA), lambda i: (i, 0))
    kv = pl.BlockSpec((n_mem, D_XA), lambda i: (0, 0))
    return pl.pallas_call(
        body, name=name, grid=(s // tq,), in_specs=[row, kv, kv], out_specs=row, out_shape=_sds((s, D_XA), BF16),
        compiler_params=_cp("parallel"),
    )(q, k, v)


def _attn_bwd(q, k, v, do, *, name):
    s = q.shape[0]
    n_mem = k.shape[0]
    tq = min(512, s)

    def body(q_ref, k_ref, v_ref, do_ref, dq_ref, dk_ref, dv_ref):
        dqs, dks, dvs = [], [], []
        for h in range(XA_HEADS):
            sl = slice(h * XA_HD, (h + 1) * XA_HD)
            qh, kh, vh, doh = q_ref[:, sl], k_ref[:, sl], v_ref[:, sl], do_ref[:, sl]
            p = _softmax_rows(qh, kh)
            dvs.append(lax.dot_general(p.astype(BF16), doh, TN, preferred_element_type=F32))
            dp = lax.dot_general(doh, vh, NT, preferred_element_type=F32)
            ds = (p * (dp - jnp.sum(dp * p, axis=-1, keepdims=True)) * (XA_HD ** -0.5)).astype(BF16)
            dqs.append(jnp.dot(ds, kh, preferred_element_type=F32))
            dks.append(lax.dot_general(ds, qh, TN, preferred_element_type=F32))
        dq_ref[...] = jnp.concatenate(dqs, axis=1).astype(dq_ref.dtype)
        dk = jnp.concatenate(dks, axis=1)
        dv = jnp.concatenate(dvs, axis=1)

        @pl.when(pl.program_id(0) == 0)
        def _():
            dk_ref[...] = dk
            dv_ref[...] = dv

        @pl.when(pl.program_id(0) > 0)
        def _():
            dk_ref[...] += dk
            dv_ref[...] += dv

    row = pl.BlockSpec((tq, D_XA), lambda i: (i, 0))
    kv = pl.BlockSpec((n_mem, D_XA), lambda i: (0, 0))
    return pl.pallas_call(
        body, name=name, grid=(s // tq,), in_specs=[row, kv, kv, row], out_specs=(row, kv, kv),
        out_shape=(_sds((s, D_XA), BF16), _sds((n_mem, D_XA), F32), _sds((n_mem, D_XA), F32)),
        compiler_params=_cp("arbitrary"),
    )(q, k, v, do)


CONV_CB = 1024
XBC_CB0 = D // CONV_CB


def _prev_rows(i, tb):
    return jnp.maximum(i * (tb // HALO) - 1, 0)


def _next_rows(i, tb, s):
    return jnp.minimum((i + 1) * (tb // HALO), s // HALO - 1)


def _taps(xcat, width):
    return [pltpu.roll(xcat, width - 1 - k, 0) for k in range(width - 1)] + [xcat]


def _conv_taps(taps, w_ref, lo):
    acc = w_ref[0:1, :] * taps[0][lo:]
    for k in range(1, len(taps)):
        acc = acc + w_ref[k:k + 1, :] * taps[k][lo:]
    return acc


def _conv_silu_fwd(proj, w, b, *, name):
    s = proj.shape[0]
    tb = min(TB, s)

    def body(x_ref, xp_ref, w_ref, b_ref, o_ref):
        i = pl.program_id(0)
        prev = jnp.where(i > 0, xp_ref[...], 0.0)
        pre = _conv_taps(_taps(jnp.concatenate([prev, x_ref[...]], axis=0), 4), w_ref, HALO) + b_ref[...]
        o_ref[...] = pre * _sigmoid(pre)

    return pl.pallas_call(
        body, name=name, grid=(s // tb, D_XBC // CONV_CB),
        in_specs=[pl.BlockSpec((tb, CONV_CB), lambda i, j: (i, XBC_CB0 + j)),
                  pl.BlockSpec((HALO, CONV_CB), lambda i, j: (_prev_rows(i, tb), XBC_CB0 + j)),
                  pl.BlockSpec((4, CONV_CB), lambda i, j: (0, j)),
                  pl.BlockSpec((1, CONV_CB), lambda i, j: (0, j))],
        out_specs=pl.BlockSpec((tb, CONV_CB), lambda i, j: (i, j)), out_shape=_sds((s, D_XBC), F32),
        compiler_params=_cp("parallel", "parallel"),
    )(proj, proj, w, b)


def _conv_silu_bwd(proj, dxs, d_b, d_c, w, b, *, name):
    s = proj.shape[0]
    tb = min(TB, s)
    n_i = s // tb
    n_xs = D // CONV_CB
    bc_w = N_GROUPS * N_STATE

    def body(x_ref, xp_ref, xn_ref, dxs_ref, dxsn_ref, dbm_ref, dbmn_ref, dcm_ref, dcmn_ref, w_ref, b_ref,
             dx_ref, dw_ref, db_ref):
        j, i = pl.program_id(0), pl.program_id(1)
        prev = jnp.where(i > 0, xp_ref[...], 0.0)
        taps = _taps(jnp.concatenate([prev, x_ref[...], xn_ref[...]], axis=0), 4)
        pre = _conv_taps(taps, w_ref, HALO) + b_ref[...]
        sg = _sigmoid(pre)
        dy = jnp.where(j < n_xs, dxs_ref[...], jnp.concatenate([dbm_ref[...], dcm_ref[...]], axis=1))
        dyn = jnp.where(j < n_xs, dxsn_ref[...], jnp.concatenate([dbmn_ref[...], dcmn_ref[...]], axis=1))
        dy_ext = jnp.concatenate([dy, jnp.where(i < n_i - 1, dyn, 0.0)], axis=0)
        dpre = dy_ext * sg * (1.0 + pre * (1.0 - sg))
        n2 = tb + HALO
        dx = w_ref[3:4, :] * dpre[:tb]
        for k in range(3):
            dx = dx + w_ref[k:k + 1, :] * pltpu.roll(dpre, n2 - (3 - k), 0)[:tb]
        dx_ref[...] = dx.astype(dx_ref.dtype)
        dpc = dpre[:tb]
        parts = [jnp.sum(dpc * taps[k][HALO:HALO + tb], axis=0, keepdims=True) for k in range(4)]
        dbp = jnp.sum(dpc, axis=0, keepdims=True)

        @pl.when(i == 0)
        def _():
            for k in range(4):
                dw_ref[k] = parts[k]
            db_ref[...] = dbp

        @pl.when(i > 0)
        def _():
            for k in range(4):
                dw_ref[k] += parts[k]
            db_ref[...] += dbp

    return pl.pallas_call(
        body, name=name, grid=(D_XBC // CONV_CB, n_i),
        in_specs=[pl.BlockSpec((tb, CONV_CB), lambda j, i: (i, XBC_CB0 + j)),
                  pl.BlockSpec((HALO, CONV_CB), lambda j, i: (_prev_rows(i, tb), XBC_CB0 + j)),
                  pl.BlockSpec((HALO, CONV_CB), lambda j, i: (_next_rows(i, tb, s), XBC_CB0 + j)),
                  pl.BlockSpec((tb, CONV_CB), lambda j, i: (jnp.where(j < n_xs, i, 0), jnp.minimum(j, n_xs - 1))),
                  pl.BlockSpec((HALO, CONV_CB),
                               lambda j, i: (jnp.where(j < n_xs, _next_rows(i, tb, s), 0), jnp.minimum(j, n_xs - 1))),
                  pl.BlockSpec((tb, bc_w), lambda j, i: (jnp.where(j < n_xs, 0, i), 0)),
                  pl.BlockSpec((HALO, bc_w), lambda j, i: (jnp.where(j < n_xs, 0, _next_rows(i, tb, s)), 0)),
                  pl.BlockSpec((tb, bc_w), lambda j, i: (jnp.where(j < n_xs, 0, i), 0)),
                  pl.BlockSpec((HALO, bc_w), lambda j, i: (jnp.where(j < n_xs, 0, _next_rows(i, tb, s)), 0)),
                  pl.BlockSpec((4, CONV_CB), lambda j, i: (0, j)),
                  pl.BlockSpec((1, CONV_CB), lambda j, i: (0, j))],
        out_specs=(pl.BlockSpec((tb, CONV_CB), lambda j, i: (i, j)),
                   pl.BlockSpec((4, 1, CONV_CB), lambda j, i: (0, 0, j)),
                   pl.BlockSpec((1, CONV_CB), lambda j, i: (0, j))),
        out_shape=(_sds((s, D_XBC), BF16), _sds((4, 1, D_XBC), F32), _sds((1, D_XBC), F32)),
        compiler_params=_cp("parallel", "arbitrary"),
    )(proj, proj, proj, dxs, dxs, d_b, d_b, d_c, d_c, w, b)


SC_U0 = (D + D_XBC) // CONV_CB
SC_B0 = SC_U0 + D // CONV_CB
SC_C0 = SC_B0 + D // CONV_CB


def _sc_fwd(proj, w, gain, cat, *, name):
    s = proj.shape[0]
    tb = min(TB, s)

    def body(u_ref, b_ref, c_ref, up_ref, cp_ref, w_ref, g_ref, cat_ref, o_ref):
        del cat_ref
        i = pl.program_id(0)
        cup = jnp.where(i > 0, cp_ref[...] * up_ref[...], 0.0)
        cat = jnp.concatenate([cup, c_ref[...] * u_ref[...]], axis=0)
        v = b_ref[...] * _conv_taps(_taps(cat, 3), w_ref, HALO)
        o_ref[...] = (v * _group_rstd(v, SC_GROUP_W) * g_ref[...]).astype(o_ref.dtype)

    def cur(c0):
        return pl.BlockSpec((tb, CONV_CB), lambda i, j: (i, c0 + j))

    def prev(c0):
        return pl.BlockSpec((HALO, CONV_CB), lambda i, j: (_prev_rows(i, tb), c0 + j))

    return pl.pallas_call(
        body, name=name, grid=(s // tb, D // CONV_CB),
        in_specs=[cur(SC_U0), cur(SC_B0), cur(SC_C0), prev(SC_U0), prev(SC_C0),
                  pl.BlockSpec((3, CONV_CB), lambda i, j: (0, j)), pl.BlockSpec((1, CONV_CB), lambda i, j: (0, j)),
                  pl.BlockSpec(memory_space=pl.ANY)],
        out_specs=pl.BlockSpec((tb, CONV_CB), lambda i, j: (i, D // CONV_CB + j)), out_shape=_sds((s, 2 * D), BF16),
        input_output_aliases={7: 0}, compiler_params=_cp("parallel", "parallel"),
    )(proj, proj, proj, proj, proj, w, gain, cat)


def _sc_bwd(proj, dcat, w, gain, *, name):
    s = proj.shape[0]
    tb = min(TB, s)
    n_i = s // tb
    dy0 = D // CONV_CB

    def body(u_ref, b_ref, c_ref, up_ref, cp_ref, un_ref, bn_ref, cn_ref, dy_ref, dyn_ref, w_ref, g_ref,
             du_ref, db_ref, dc_ref, dw_ref, dg_ref):
        i = pl.program_id(1)
        u, c = u_ref[...], c_ref[...]
        cup = jnp.where(i > 0, cp_ref[...] * up_ref[...], 0.0)
        taps = _taps(jnp.concatenate([cup, c * u, cn_ref[...] * un_ref[...]], axis=0), 3)
        conv = _conv_taps(taps, w_ref, HALO)
        b_ext = jnp.concatenate([b_ref[...], bn_ref[...]], axis=0)
        dy_ext = jnp.concatenate([dy_ref[...], jnp.where(i < n_i - 1, dyn_ref[...], 0.0)], axis=0)
        v = b_ext * conv
        r = _group_rstd(v, SC_GROUP_W)
        vhat = v * r
        dvhat = dy_ext * g_ref[...]
        dv = r * (dvhat - vhat * _group_mean(dvhat * vhat, SC_GROUP_W))
        dconv = dv * b_ext
        n2 = tb + HALO
        dcu = (w_ref[2:3, :] * dconv[:tb] + w_ref[1:2, :] * pltpu.roll(dconv, n2 - 1, 0)[:tb]
               + w_ref[0:1, :] * pltpu.roll(dconv, n2 - 2, 0)[:tb])
        du_ref[...] = (dcu * c).astype(du_ref.dtype)
        dc_ref[...] = (dcu * u).astype(dc_ref.dtype)
        db_ref[...] = (dv * conv)[:tb].astype(db_ref.dtype)
        dcc = dconv[:tb]
        parts = [jnp.sum(dcc * taps[k][HALO:HALO + tb], axis=0, keepdims=True) for k in range(3)]
        dgp = jnp.sum((dy_ext * vhat)[:tb], axis=0, keepdims=True)

        @pl.when(i == 0)
        def _():
            for k in range(3):
                dw_ref[k] = parts[k]
            dg_ref[...] = dgp

        @pl.when(i > 0)
        def _():
            for k in range(3):
                dw_ref[k] += parts[k]
            dg_ref[...] += dgp

    def cur(c0):
        return pl.BlockSpec((tb, CONV_CB), lambda j, i: (i, c0 + j))

    def prev(c0):
        return pl.BlockSpec((HALO, CONV_CB), lambda j, i: (_prev_rows(i, tb), c0 + j))

    def nxt(c0):
        return pl.BlockSpec((HALO, CONV_CB), lambda j, i: (_next_rows(i, tb, s), c0 + j))

    vec = pl.BlockSpec((1, CONV_CB), lambda j, i: (0, j))
    out_row = pl.BlockSpec((tb, CONV_CB), lambda j, i: (i, j))
    return pl.pallas_call(
        body, name=name, grid=(D // CONV_CB, n_i),
        in_specs=[cur(SC_U0), cur(SC_B0), cur(SC_C0), prev(SC_U0), prev(SC_C0), nxt(SC_U0), nxt(SC_B0), nxt(SC_C0),
                  cur(dy0), nxt(dy0), pl.BlockSpec((3, CONV_CB), lambda j, i: (0, j)), vec],
        out_specs=(out_row, out_row, out_row, pl.BlockSpec((3, 1, CONV_CB), lambda j, i: (0, 0, j)), vec),
        out_shape=(_sds((s, D), BF16), _sds((s, D), BF16), _sds((s, D), BF16), _sds((3, 1, D), F32), _sds((1, D), F32)),
        compiler_params=_cp("parallel", "arbitrary"),
    )(proj, proj, proj, proj, proj, proj, proj, proj, dcat, dcat, w, gain)


def _gated_norm_fwd(y, proj, gain, *, name):
    s = y.shape[0]
    tb = min(TB, s)

    def body(y_ref, z_ref, g_ref, o_ref):
        z = z_ref[...]
        t = y_ref[...] * z * _sigmoid(z)
        o_ref[...] = (t * _group_rstd(t, SSD_GROUP_W) * g_ref[...]).astype(o_ref.dtype)

    row = pl.BlockSpec((tb, D), lambda i: (i, 0))
    return pl.pallas_call(
        body, name=name, grid=(s // tb,), in_specs=[row, row, pl.BlockSpec((1, D), lambda i: (0, 0))], out_specs=row,
        out_shape=_sds((s, 2 * D), BF16), compiler_params=_cp("parallel"),
    )(y, proj, gain)


def _gated_norm_bwd(y, proj, dcat, gain, *, name):
    s = y.shape[0]
    tb = min(TB, s)

    def body(y_ref, z_ref, d_ref, g_ref, dy_ref, dz_ref, dg_ref):
        z, yv, dout = z_ref[...], y_ref[...], d_ref[...]
        sg = _sigmoid(z)
        sz = z * sg
        t = yv * sz
        r = _group_rstd(t, SSD_GROUP_W)
        that = t * r
        dthat = dout * g_ref[...]
        dt = r * (dthat - that * _group_mean(dthat * that, SSD_GROUP_W))
        dy_ref[...] = dt * sz
        dz_ref[...] = (dt * yv * sg * (1.0 + z * (1.0 - sg))).astype(dz_ref.dtype)
        part = jnp.sum(dout * that, axis=0, keepdims=True)

        @pl.when(pl.program_id(0) == 0)
        def _():
            dg_ref[...] = part

        @pl.when(pl.program_id(0) > 0)
        def _():
            dg_ref[...] += part

    row = pl.BlockSpec((tb, D), lambda i: (i, 0))
    vec = pl.BlockSpec((1, D), lambda i: (0, 0))
    return pl.pallas_call(
        body, name=name, grid=(s // tb,), in_specs=[row, row, row, vec], out_specs=(row, row, vec),
        out_shape=(_sds((s, D), F32), _sds((s, D), BF16), _sds((1, D), F32)), compiler_params=_cp("arbitrary"),
    )(y, proj, dcat, gain)


def _tri(lower):
    row = lax.broadcasted_iota(jnp.int32, (CHUNK, CHUNK), 0)
    col = lax.broadcasted_iota(jnp.int32, (CHUNK, CHUNK), 1)
    return jnp.where(row >= col if lower else col >= row, 1.0, 0.0).astype(BF16)


def _head_spread():
    row = lax.broadcasted_iota(jnp.int32, (DT_W, D), 0)
    col = lax.broadcasted_iota(jnp.int32, (DT_W, D), 1)
    return jnp.where(col // HEAD_DIM == row, 1.0, 0.0).astype(BF16)


def _head_pick():
    row = lax.broadcasted_iota(jnp.int32, (D, DT_W), 0)
    col = lax.broadcasted_iota(jnp.int32, (D, DT_W), 1)
    return jnp.where(row == col * HEAD_DIM, 1.0, 0.0).astype(BF16)


def _dt_fwd(dt_raw, bias, a_log, *, name):
    s = dt_raw.shape[0]
    nc = s // CHUNK

    def body(raw_ref, bias_ref, alog_ref, dt_ref, cum_ref, cumt_ref):
        x = raw_ref[...] + bias_ref[...]
        dt = jnp.maximum(x, 0.0) + jnp.log1p(jnp.exp(-jnp.abs(x)))
        cum = _split3_dot(_tri(True), dt * (-jnp.exp(alog_ref[...])))
        spread = _head_spread()
        dt_ref[...] = _split3_dot_r(dt, spread)
        cum_ref[...] = _split3_dot_r(cum, spread)
        cumt_ref[0] = cum.T

    row = pl.BlockSpec((CHUNK, DT_W), lambda i: (i, 0))
    wide = pl.BlockSpec((CHUNK, D), lambda i: (i, 0))
    vec = pl.BlockSpec((1, DT_W), lambda i: (0, 0))
    return pl.pallas_call(
        body, name=name, grid=(nc,), in_specs=[row, vec, vec],
        out_specs=(wide, wide, pl.BlockSpec((1, DT_W, CHUNK), lambda i: (i, 0, 0))),
        out_shape=(_sds((s, D), F32), _sds((s, D), F32), _sds((nc, DT_W, CHUNK), F32)),
        compiler_params=_cp("parallel"),
    )(dt_raw, bias, a_log)


def _dt_bwd(dt_raw, bias, ddt_b, *, name):
    s = dt_raw.shape[0]
    tb = min(TB, s)

    def body(raw_ref, bias_ref, d_ref, o_ref, db_ref):
        g = _split3_dot_r(d_ref[...], _head_pick()) * _sigmoid(raw_ref[...] + bias_ref[...])
        o_ref[...] = g.astype(o_ref.dtype)
        part = jnp.sum(g, axis=0, keepdims=True)

        @pl.when(pl.program_id(0) == 0)
        def _():
            db_ref[...] = part

        @pl.when(pl.program_id(0) > 0)
        def _():
            db_ref[...] += part

    row = pl.BlockSpec((tb, DT_W), lambda i: (i, 0))
    vec = pl.BlockSpec((1, DT_W), lambda i: (0, 0))
    return pl.pallas_call(
        body, name=name, grid=(s // tb,), in_specs=[row, vec, pl.BlockSpec((tb, D), lambda i: (i, 0))],
        out_specs=(row, vec), out_shape=(_sds((s, DT_W), BF16), _sds((1, DT_W), F32)),
        compiler_params=_cp("arbitrary"),
    )(dt_raw, bias, ddt_b)


N_PAIRS = N_HEADS // 2
PAIRS_PER_GROUP = N_PAIRS // N_GROUPS
GROUP_W = PAIRS_PER_GROUP * LANES
B_CB0 = D // LANES
C_CB0 = B_CB0 + N_GROUPS


def _decay(cum_col, cum_row, causal):
    return jnp.where(causal, jnp.exp(jnp.minimum(cum_col - cum_row, 0.0)), 0.0)


def _causal_mask():
    row = lax.broadcasted_iota(jnp.int32, (CHUNK, CHUNK), 0)
    col = lax.broadcasted_iota(jnp.int32, (CHUNK, CHUNK), 1)
    return row >= col


def _state_row_scale(cumt_ref, pp):
    last0 = cumt_ref[0, 2 * pp][:, CHUNK - 1:CHUNK]
    last1 = cumt_ref[0, 2 * pp + 1][:, CHUNK - 1:CHUNK]
    rown = lax.broadcasted_iota(jnp.int32, (LANES, 1), 0)
    return jnp.exp(jnp.where(rown < HEAD_DIM, last0, last1))


def _ssd_specs(nc, rev):
    def ch(c):
        return nc - 1 - c if rev else c

    wide = pl.BlockSpec((CHUNK, GROUP_W), lambda c, g: (ch(c), g))
    vec = pl.BlockSpec((1, GROUP_W), lambda c, g: (0, g))
    cumt_spec = pl.BlockSpec((1, 2 * PAIRS_PER_GROUP, 1, CHUNK), lambda c, g: (ch(c), g, 0, 0))
    hp_spec = pl.BlockSpec((1, PAIRS_PER_GROUP, LANES, N_STATE), lambda c, g: (ch(c), g, 0, 0))

    def bc(c0):
        return pl.BlockSpec((CHUNK, LANES), lambda c, g: (ch(c), c0 + g))

    return wide, vec, cumt_spec, hp_spec, bc


def _ssd_fwd(xbc_c, dt_b, cum_b, cumt4, dskip_b, *, name):
    s = xbc_c.shape[0]
    nc = s // CHUNK

    def body(xs_ref, b_ref, c_ref, dt_ref, cum_ref, cumt_ref, dsk_ref, y_ref, hp_ref, state):
        c, g = pl.program_id(0), pl.program_id(1)

        @pl.when(c == 0)
        def _():
            state[pl.ds(g * PAIRS_PER_GROUP, PAIRS_PER_GROUP)] = jnp.zeros((PAIRS_PER_GROUP, LANES, N_STATE), F32)

        bb, cbm = b_ref[...].astype(BF16), c_ref[...].astype(BF16)
        cbv = lax.dot_general(cbm, bb, NT, preferred_element_type=F32)
        first = lax.broadcasted_iota(jnp.int32, (CHUNK, LANES), 1) < HEAD_DIM
        causal = _causal_mask()
        for pp in range(PAIRS_PER_GROUP):
            sl = slice(pp * LANES, (pp + 1) * LANES)
            xs, cum = xs_ref[:, sl], cum_ref[:, sl]
            xt = xs * dt_ref[:, sl]
            xtb = xt.astype(BF16)
            hp = state[g * PAIRS_PER_GROUP + pp]
            hp_ref[0, pp] = hp
            y = jnp.exp(cum) * lax.dot_general(cbm, hp.astype(BF16), NT, preferred_element_type=F32)
            for hh in range(2):
                lm = _decay(cum[:, hh * HEAD_DIM:hh * HEAD_DIM + 1], cumt_ref[0, 2 * pp + hh], causal)
                xm = jnp.where(first if hh == 0 else jnp.logical_not(first), xtb, jnp.zeros_like(xtb))
                y = y + jnp.dot((cbv * lm).astype(BF16), xm, preferred_element_type=F32)
            decs = jnp.exp(cum[CHUNK - 1:CHUNK, :] - cum)
            st = lax.dot_general((xt * decs).astype(BF16), bb, TN, preferred_element_type=F32)
            state[g * PAIRS_PER_GROUP + pp] = _state_row_scale(cumt_ref, pp) * hp + st
            y_ref[:, sl] = y + xs * dsk_ref[:, sl]

    wide, vec, cumt_spec, hp_spec, bc = _ssd_specs(nc, False)
    return pl.pallas_call(
        body, name=name, grid=(nc, N_GROUPS),
        in_specs=[wide, bc(B_CB0), bc(C_CB0), wide, wide, cumt_spec, vec],
        out_specs=(wide, hp_spec),
        out_shape=(_sds((s, D), F32), _sds((nc, N_PAIRS, LANES, N_STATE), F32)),
        scratch_shapes=[pltpu.VMEM((N_PAIRS, LANES, N_STATE), F32)],
        compiler_params=_cp("arbitrary", "arbitrary"),
    )(xbc_c, xbc_c, xbc_c, dt_b, cum_b, cumt4, dskip_b)


def _ssd_bwd(xbc_c, dt_b, cum_b, cumt4, hprev, dy, alog_b, dskip_b, *, name):
    s = xbc_c.shape[0]
    nc = s // CHUNK

    def body(xs_ref, b_ref, c_ref, dt_ref, cum_ref, cumt_ref, hp_ref, dy_ref, alog_ref, dsk_ref,
             dxs_ref, db_ref, dc_ref, ddt_ref, dalog_ref, ddsk_ref, dstate):
        c, g = pl.program_id(0), pl.program_id(1)
        pairs = pl.ds(g * PAIRS_PER_GROUP, PAIRS_PER_GROUP)

        @pl.when(c == 0)
        def _():
            dstate[pairs] = jnp.zeros((PAIRS_PER_GROUP, LANES, N_STATE), F32)
            dalog_ref[pairs] = jnp.zeros((PAIRS_PER_GROUP, 1, LANES), F32)
            ddsk_ref[pairs] = jnp.zeros((PAIRS_PER_GROUP, 1, LANES), F32)

        bb, cbm = b_ref[...].astype(BF16), c_ref[...].astype(BF16)
        cbv = lax.dot_general(cbm, bb, NT, preferred_element_type=F32)
        first = lax.broadcasted_iota(jnp.int32, (CHUNK, LANES), 1) < HEAD_DIM
        causal = _causal_mask()
        rown = lax.broadcasted_iota(jnp.int32, (LANES, 1), 0)
        d_b = jnp.zeros((CHUNK, N_STATE), F32)
        d_c = jnp.zeros((CHUNK, N_STATE), F32)
        for pp in range(PAIRS_PER_GROUP):
            sl = slice(pp * LANES, (pp + 1) * LANES)
            idx = g * PAIRS_PER_GROUP + pp
            xs, dtb, cum, dy = xs_ref[:, sl], dt_ref[:, sl], cum_ref[:, sl], dy_ref[:, sl]
            a_b = -jnp.exp(alog_ref[:, sl])
            xt = xs * dtb
            xtb = xt.astype(BF16)
            dyb = dy.astype(BF16)
            hp = hp_ref[0, pp]
            hpb = hp.astype(BF16)
            dh = dstate[idx]
            dhb = dh.astype(BF16)
            exp_cum = jnp.exp(cum)
            decs = jnp.exp(cum[CHUNK - 1:CHUNK, :] - cum)
            row_scale = _state_row_scale(cumt_ref, pp)
            xd = (xt * decs).astype(BF16)
            dye = (dy * exp_cum).astype(BF16)
            zero_b = jnp.zeros_like(xtb)

            h_next = row_scale * hp + lax.dot_general(xd, bb, TN, preferred_element_type=F32)
            y = exp_cum * lax.dot_general(cbm, hpb, NT, preferred_element_type=F32)
            dxt = decs * lax.dot_general(bb, dhb, NT, preferred_element_type=F32)
            dcb = jnp.zeros((CHUNK, CHUNK), F32)
            for hh in range(2):
                mask = first if hh == 0 else jnp.logical_not(first)
                lm = _decay(cum[:, hh * HEAD_DIM:hh * HEAD_DIM + 1], cumt_ref[0, 2 * pp + hh], causal)
                m = (cbv * lm).astype(BF16)
                y = y + jnp.dot(m, jnp.where(mask, xtb, zero_b), preferred_element_type=F32)
                dxt = dxt + jnp.where(mask, lax.dot_general(m, dyb, TN, preferred_element_type=F32), 0.0)
                dm = lax.dot_general(jnp.where(mask, dyb, zero_b), xtb, NT, preferred_element_type=F32)
                dcb = dcb + dm * lm
            dcbb = dcb.astype(BF16)
            d_c = d_c + jnp.dot(dcbb, bb, preferred_element_type=F32) + jnp.dot(dye, hpb, preferred_element_type=F32)
            d_b = (d_b + lax.dot_general(dcbb, cbm, TN, preferred_element_type=F32)
                   + jnp.dot(xd, dhb, preferred_element_type=F32))
            dstate[idx] = row_scale * dh + lax.dot_general(dye, cbm, TN, preferred_element_type=F32)

            d_cum = _pair_sum(dyb.astype(F32) * y - dxt * xtb.astype(F32), first)
            e = jnp.sum(dh * h_next, axis=1, keepdims=True)
            t0 = jnp.sum(jnp.where(rown < HEAD_DIM, e, 0.0), axis=0, keepdims=True)
            t1 = jnp.sum(jnp.where(rown < HEAD_DIM, 0.0, e), axis=0, keepdims=True)
            d_da = _split3_dot(_tri(False), d_cum) + jnp.where(first[0:1, :], t0, t1)
            ddt_ref[:, sl] = a_b * d_da + _pair_sum(dxt * xs, first)
            dxs_ref[:, sl] = dxt * dtb + dy * dsk_ref[:, sl]
            dalog_ref[idx] += jnp.sum(d_da * dtb * a_b, axis=0, keepdims=True)
            ddsk_ref[idx] += jnp.sum(_pair_sum(dy * xs, first), axis=0, keepdims=True)
        db_ref[...] = d_b
        dc_ref[...] = d_c

    wide, vec, cumt_spec, hp_spec, bc = _ssd_specs(nc, True)
    acc = pl.BlockSpec((N_PAIRS, 1, LANES), lambda c, g: (0, 0, 0))
    return pl.pallas_call(
        body, name=name, grid=(nc, N_GROUPS),
        in_specs=[wide, bc(B_CB0), bc(C_CB0), wide, wide, cumt_spec, hp_spec, wide, vec, vec],
        out_specs=(wide, bc(0), bc(0), wide, acc, acc),
        out_shape=(_sds((s, D), F32), _sds((s, N_GROUPS * N_STATE), F32), _sds((s, N_GROUPS * N_STATE), F32),
                   _sds((s, D), F32), _sds((N_PAIRS, 1, LANES), F32), _sds((N_PAIRS, 1, LANES), F32)),
        scratch_shapes=[pltpu.VMEM((N_PAIRS, LANES, N_STATE), F32)],
        compiler_params=_cp("arbitrary", "arbitrary"),
    )(xbc_c, xbc_c, xbc_c, dt_b, cum_b, cumt4, hprev, dy, alog_b, dskip_b)


def _lane_bcast(v):
    return jnp.repeat(v, HEAD_DIM, axis=1)


DT0 = D + D_XBC
W_IN_SHARD = (DT0 + N_HEADS + 3 * D) // N_DEV
DT_PIECE = DT0 // W_IN_SHARD
DT_OFF = DT0 - DT_PIECE * W_IN_SHARD
assert DT_OFF + N_HEADS <= W_IN_SHARD


def _split_w_in(got):
    hold = got[DT_PIECE]
    cols = ([got[d] for d in range(DT_PIECE)] + [hold[:, :DT_OFF], hold[:, DT_OFF + N_HEADS:]]
            + [got[d] for d in range(DT_PIECE + 1, N_DEV)])
    return (jnp.concatenate(cols, axis=1),
            jnp.pad(hold[:, DT_OFF:DT_OFF + N_HEADS], ((0, 0), (0, DT_W - N_HEADS))))


def _w_in_pieces(dw_main, dw_dt):
    pieces = []
    for d in range(N_DEV):
        lo = d * W_IN_SHARD
        if d < DT_PIECE:
            pieces.append(dw_main[:, lo:lo + W_IN_SHARD])
        elif d == DT_PIECE:
            pieces.append(jnp.concatenate(
                [dw_main[:, lo:DT0], dw_dt[:, :N_HEADS], dw_main[:, DT0:lo + W_IN_SHARD - N_HEADS]], axis=1))
        else:
            pieces.append(dw_main[:, lo - N_HEADS:lo - N_HEADS + W_IN_SHARD])
    return jnp.stack(pieces)


def _prep_small(conv_w, conv_b, dt_bias, a_log, d_skip):
    pad = DT_W - N_HEADS
    return dict(
        conv_w=conv_w, conv_b=conv_b.reshape(1, D_XBC),
        dt_bias=jnp.pad(dt_bias.reshape(1, N_HEADS), ((0, 0), (0, pad))),
        a_log=jnp.pad(a_log.reshape(1, N_HEADS), ((0, 0), (0, pad))),
        alog_b=_lane_bcast(a_log.reshape(1, N_HEADS)), dskip_b=_lane_bcast(d_skip.reshape(1, N_HEADS)))


def _layer_fwd(h0, memn, p, g, li, comm=None):
    s = h0.shape[0]
    nc = s // CHUNK
    n = f"l{li}_"

    def hosted(what, *args, **kw):
        side = comm.ag_side(what, li) if comm is not None else None
        if side is None:
            return _mm(*args, **kw)
        out, got = _mm(*args, side=side, **kw)
        comm.after_ag(what, li, p, got)
        return out

    if comm is not None:
        comm.before_layer(li, p)
    hn = _norm_fwd(h0, g["norm_mix"], name=n + "norm_mix")
    proj = hosted("proj", hn, p["w_main"], name=n + "proj")
    dt_raw = _mm(hn, p["w_dt"], name=n + "proj_dt")
    xbc_c = _conv_silu_fwd(proj, p["conv_w"], p["conv_b"], name=n + "conv")
    dt_b, cum_b, cumt = _dt_fwd(dt_raw, p["dt_bias"], p["a_log"], name=n + "dt")
    cumt4 = cumt[:, :N_HEADS, :].reshape(nc, N_HEADS, 1, CHUNK)
    y, hprev = _ssd_fwd(xbc_c, dt_b, cum_b, cumt4, p["dskip_b"], name=n + "ssd")
    cat = _gated_norm_fwd(y, proj, g["ssd_norm"], name=n + "gnorm")
    cat = _sc_fwd(proj, p["sc_conv_w"], g["sc_norm"], cat, name=n + "sc")
    h1 = hosted("out", cat, p["w_out"], res=h0, bm=512, j_outer=True, name=n + "out")
    hx = _norm_fwd(h1, g["norm_xa"], name=n + "norm_xa")
    q = _mm(hx, p["w_q"], out_dtype=BF16, name=n + "q")
    k = _mm(memn, p["w_k"], out_dtype=BF16, name=n + "k")
    v = _mm(memn, p["w_v"], out_dtype=BF16, name=n + "v")
    o = _attn_fwd(q, k, v, name=n + "attn")
    h2 = _mm(o, p["w_o"], res=h1, name=n + "o")
    hf = _norm_fwd(h2, g["norm_ffn"], name=n + "norm_ffn")
    gu = hosted("gu", hf, p["w_gu"], out_dtype=BF16, name=n + "gu")
    act = _swiglu_fwd(gu, name=n + "swiglu")
    h3 = hosted("down", act, p["w_down"], res=h2, bm=512, j_outer=True, name=n + "down")
    saved = dict(h0=h0, hn=hn, proj=proj, dt_raw=dt_raw, xbc_c=xbc_c, dt_b=dt_b, cum_b=cum_b, cumt4=cumt4, hprev=hprev,
                 y=y, cat=cat, h1=h1, hx=hx, q=q, k=k, v=v, o=o, h2=h2, hf=hf, gu=gu, act=act)
    return h3, saved


def _layer_bwd(dh, dhb, dmemn, memn, p, g, sv, li, comm=None):
    n = f"l{li}b_"
    gr = {}

    def hosted(names, *args, **kw):
        side = comm.grad_side(li, names, gr) if comm is not None else None
        if side is None:
            return _mm(*args, **kw)
        out, got = _mm(*args, side=side, **kw)
        comm.after_grads(li, names, got)
        return out

    wide = dict(bm=512, bn=512)
    dact = _mm(dhb, p["w_down"], tb=True, bn=512, out_dtype=BF16, name=n + "dact")
    gr["w_down"] = hosted(("carry",), sv["act"], dhb, ta=True, out_dtype=BF16, name=n + "dw_down", **wide)
    dgu = _swiglu_bwd(sv["gu"], dact, name=n + "swiglu")
    dw_gu = hosted(("w_down",), sv["hf"], dgu, ta=True, out_dtype=BF16, name=n + "dw_gu", **wide)
    gr["w_gate"], gr["w_up"] = dw_gu[:, :D_FF], dw_gu[:, D_FF:]
    dhf = hosted(("w_gate",), dgu, p["w_gu"], tb=True, j_outer=True, name=n + "dhf", **wide)
    dh, dhb, gr["norm_ffn"] = _norm_bwd(sv["h2"], g["norm_ffn"], dhf, dh, name=n + "norm_ffn")
    do = _mm(dhb, p["w_o"], tb=True, out_dtype=BF16, name=n + "do")
    gr["w_o"] = _mm(sv["o"], dhb, ta=True, out_dtype=BF16, name=n + "dw_o", **wide)
    dq, dk, dv = _attn_bwd(sv["q"], sv["k"], sv["v"], do, name=n + "attn")
    gr["w_q"] = _mm(sv["hx"], dq, ta=True, out_dtype=BF16, name=n + "dw_q", **wide)
    gr["w_k"] = _mm(memn, dk, ta=True, out_dtype=BF16, name=n + "dw_k")
    gr["w_v"] = _mm(memn, dv, ta=True, out_dtype=BF16, name=n + "dw_v")
    dmemn = _mm(dk, p["w_k"], tb=True, res=dmemn, name=n + "dmem_k")
    dmemn = _mm(dv, p["w_v"], tb=True, res=dmemn, name=n + "dmem_v")
    dhx = _mm(dq, p["w_q"], tb=True, name=n + "dhx")
    dh, dhb, gr["norm_xa"] = _norm_bwd(sv["h1"], g["norm_xa"], dhx, dh, name=n + "norm_xa")
    dcat = hosted(("w_q", "w_k", "w_v", "w_o"), dhb, p["w_out"], tb=True, name=n + "dcat")
    gr["w_out"] = _mm(sv["cat"], dhb, ta=True, out_dtype=BF16, name=n + "dw_out", **wide)
    du, dgb, dgc, dsc_w, gr["sc_norm"] = _sc_bwd(sv["proj"], dcat, p["sc_conv_w"], g["sc_norm"], name=n + "sc")
    gr["sc_conv_w"] = dsc_w.reshape(3, D)
    dy, dz, gr["ssd_norm"] = _gated_norm_bwd(sv["y"], sv["proj"], dcat, g["ssd_norm"], name=n + "gnorm")
    dxs, d_b, d_c, ddt_b, dalog, ddsk = _ssd_bwd(sv["xbc_c"], sv["dt_b"], sv["cum_b"], sv["cumt4"], sv["hprev"], dy,
                                                 p["alog_b"], p["dskip_b"], name=n + "ssd")
    gr["a_log"] = dalog.reshape(N_HEADS, HEAD_DIM)[:, 0]
    gr["d_skip"] = ddsk.reshape(N_HEADS, HEAD_DIM)[:, 0]
    dxbc, dconv_w, dconv_b = _conv_silu_bwd(sv["proj"], dxs, d_b, d_c, p["conv_w"], p["conv_b"], name=n + "conv")
    gr["ssd_conv_w"] = dconv_w.reshape(4, D_XBC)
    gr["ssd_conv_b"] = dconv_b.reshape(D_XBC)
    ddt_raw, ddt_bias = _dt_bwd(sv["dt_raw"], p["dt_bias"], ddt_b, name=n + "dt")
    gr["dt_bias"] = ddt_bias[0, :N_HEADS]
    dproj = jnp.concatenate([dz, dxbc, du, dgb, dgc], axis=1)
    dw_main = hosted(("w_up", "w_out"), sv["hn"], dproj, ta=True, out_dtype=BF16, name=n + "dw_main", **wide)
    dw_dt = _mm(sv["hn"], ddt_raw, ta=True, out_dtype=BF16, name=n + "dw_dt", **wide)
    gr["w_in"] = [_w_in_pieces(dw_main[rows], dw_dt[rows]) for rows in (slice(0, D // 2), slice(D // 2, D))]
    dhn = hosted(("w_in",), dproj, p["w_main"], tb=True, j_outer=True, name=n + "dhn", **wide)
    dhn = _mm(ddt_raw, p["w_dt"], tb=True, res=dhn, name=n + "dhn_dt")
    dh, dhb, gr["norm_mix"] = _norm_bwd(sv["h0"], g["norm_mix"], dhn, dh, name=n + "norm_mix")
    return dh, dhb, dmemn, gr


def _local_step(x, mem, tgt, layers, gains, mem_norm, norm_final, comm=None):
    depth = len(layers)
    memn_f = _norm_fwd(mem, mem_norm, out_dtype=F32, name="mem_norm")
    memn = memn_f.astype(BF16)
    h = x
    saved = []
    for li in range(depth):
        h, sv = _layer_fwd(h, memn, layers[li], gains[li], li, comm)
        saved.append(sv)
    loss, dh, dhb, d_final = _loss_head(h, norm_final, tgt, name="loss_head")
    dmemn = jnp.zeros(mem.shape, F32)
    grads = [None] * depth
    for li in reversed(range(depth)):
        dh, dhb, dmemn, grads[li] = _layer_bwd(dh, dhb, dmemn, memn, layers[li], gains[li], saved[li], li, comm)
    _, _, d_mem_norm = _norm_bwd(mem, mem_norm, dmemn, None, name="mem_norm_b")
    return loss, dh, grads, d_mem_norm, d_final


ADAMW_ROWS = (64, 32, 16, 8)


def _adamw(parts, w, m, v, *, name):
    n_parts = len(parts)
    r, c_dim = parts[0].shape[1:]
    per_layer = n_parts // w.shape[0]
    assert w.shape == (n_parts // per_layer, per_layer * r, c_dim), (name, w.shape, parts[0].shape)
    tr = next(t for t in ADAMW_ROWS if r % t == 0)
    nb = r // tr

    def body(*refs):
        p_refs = refs[:n_parts]
        w_ref, m_ref, v_ref, g_ref, d_ref, nm_ref, nv_ref = refs[n_parts:]
        for l in range(n_parts):
            @pl.when(pl.program_id(0) == l)
            def _(l=l):
                g = p_refs[l][0].astype(F32)
                for s in range(1, N_DEV):
                    g = g + p_refs[l][s].astype(F32)
                g_ref[...] = g

        g = g_ref[...]
        m2 = ADAM_B1 * m_ref[...] + (1.0 - ADAM_B1) * g
        v2 = ADAM_B2 * v_ref[...] + (1.0 - ADAM_B2) * (g * g)
        m_hat = m2 / (1.0 - ADAM_B1 ** ADAM_STEP)
        v_hat = v2 / (1.0 - ADAM_B2 ** ADAM_STEP)
        d_ref[...] = -ADAM_LR * (m_hat / (jnp.sqrt(v_hat) + ADAM_EPS) + ADAM_WD * w_ref[...])
        nm_ref[...] = m2
        nv_ref[...] = v2

    def part_spec(l):
        return pl.BlockSpec((N_DEV, tr, c_dim),
                            lambda lay, i: (0, jnp.where(lay == l, i, jnp.where(lay < l, 0, nb - 1)), 0))

    row = pl.BlockSpec((None, tr, c_dim), lambda lay, i: (lay // per_layer, (lay % per_layer) * nb + i, 0))
    out = _sds(w.shape, F32)
    return pl.pallas_call(
        body, name=name, grid=(n_parts, nb), in_specs=[part_spec(l) for l in range(n_parts)] + [row, row, row],
        out_specs=(row, row, row, row), out_shape=(out, out, out, out), compiler_params=_cp("arbitrary", "arbitrary"),
    )(*parts, w, m, v)


BIG = ("w_in", "w_out", "w_q", "w_k", "w_v", "w_o", "w_gate", "w_up", "w_down")
COL_SHARDED = ("w_in", "w_o", "w_gate", "w_up")
PROJ_HOSTED = ("w_out", "w_q", "w_k", "w_v", "w_o", "w_gate")
SMALL_REPL = ("norm_mix", "ssd_conv_b", "dt_bias", "a_log", "d_skip", "ssd_norm", "sc_norm", "mem_norm", "norm_xa",
              "norm_ffn", "norm_final")
SMALL_SHARDED = ("ssd_conv_w", "sc_conv_w")
WEIGHTS = ("norm_mix", "w_in", "ssd_conv_w", "ssd_conv_b", "dt_bias", "a_log", "d_skip", "ssd_norm", "sc_conv_w",
           "sc_norm", "w_out", "mem_norm", "norm_xa", "w_q", "w_k", "w_v", "w_o", "norm_ffn", "w_gate", "w_up",
           "w_down", "norm_final")


def _assemble(got, col_sharded):
    l, _, r, c_dim = got.shape
    if col_sharded:
        return jnp.transpose(got, (0, 2, 1, 3)).reshape(l, r, N_DEV * c_dim)
    return got.reshape(l, N_DEV * r, c_dim)


def _to_pieces(full, col_sharded):
    rr, cc = full.shape
    if col_sharded:
        return jnp.transpose(full.reshape(rr, N_DEV, cc // N_DEV), (1, 0, 2))
    return full.reshape(N_DEV, rr // N_DEV, cc)


class _Comm:
    def __init__(self, w, first_w_in):
        self.w = w
        self.depth = w["w_in"].shape[0]
        self.next_w_in = first_w_in
        self.next_w_down = None
        self.gate = None
        self.carry = None
        self.got = {nm: [None] * self.depth for nm in BIG}
        self.got["w_in"] = [None] * (2 * self.depth)

    def _wanted(self, what, li):
        more = li + 1 < self.depth
        if what == "proj":
            return [(nm, li) for nm in PROJ_HOSTED]
        if what == "out":
            return [("w_up", li)]
        if what == "gu":
            return ([("w_in", li + 1)] if more else []) + ([("w_down", 0)] if li == 0 else [])
        return [("w_down", li + 1)] if more else []

    def before_layer(self, li, p):
        p["w_main"], p["w_dt"] = _split_w_in(self.next_w_in)
        if li > 0:
            p["w_down"] = self.next_w_down

    def ag_side(self, what, li):
        wanted = self._wanted(what, li)
        return _ag_side([self.w[nm][l:l + 1].astype(BF16) for nm, l in wanted]) if wanted else None

    def after_ag(self, what, li, p, got):
        for (nm, l), g in zip(self._wanted(what, li), got):
            if nm == "w_in":
                self.next_w_in = g[0]
                continue
            full = _assemble(g, nm in COL_SHARDED)[0]
            if nm == "w_gate":
                self.gate = full
            elif nm == "w_up":
                p["w_gu"] = jnp.concatenate([self.gate, full], axis=1)
            elif nm == "w_down" and l > li:
                self.next_w_down = full
            else:
                p[nm] = full

    def _grad_jobs(self, li, names):
        jobs = []
        for nm in names:
            if nm == "w_in":
                jobs += [("w_in", 2 * li)] + ([("w_in", 1)] if li == 0 else [])
            elif nm == "carry":
                jobs += [("w_in", 2 * li + 3)] if li + 1 < self.depth else []
            else:
                jobs.append((nm, li))
        return jobs

    def grad_side(self, li, names, gr):
        pieces = []
        for nm, slot in self._grad_jobs(li, names):
            if nm != "w_in":
                pieces.append(_to_pieces(gr[nm], nm in COL_SHARDED))
            elif slot == 2 * li + 3:
                pieces.append(self.carry)
            else:
                pieces.append(gr["w_in"][slot - 2 * li])
        if "w_in" in names and li > 0:
            self.carry = gr["w_in"][1]
        return _rs_side(pieces) if pieces else None

    def after_grads(self, li, names, got):
        for (nm, slot), g in zip(self._grad_jobs(li, names), got):
            self.got[nm][slot] = g


def _pack(arrs, names):
    flat = jnp.concatenate([arrs[nm].reshape(-1).astype(F32) for nm in names])
    rows = -(-flat.shape[0] // (TB * LANES)) * TB
    return jnp.pad(flat, (0, rows * LANES - flat.shape[0])).reshape(rows, LANES)


def _unpack(packed, shapes, names):
    flat = packed.reshape(-1)
    out, off = {}, 0
    for nm in names:
        size = 1
        for dim in shapes[nm]:
            size *= dim
        out[nm] = flat[off:off + size].reshape(shapes[nm])
        off += size
    return out


def kernel(x, mem, norm_mix, w_in, ssd_conv_w, ssd_conv_b, dt_bias, a_log, d_skip, ssd_norm, sc_conv_w, sc_norm, w_out, mem_norm, norm_xa, w_q, w_k, w_v, w_o, norm_ffn, w_gate, w_up, w_down, norm_final, loss_target, m_norm_mix, m_w_in, m_ssd_conv_w, m_ssd_conv_b, m_dt_bias, m_a_log, m_d_skip, m_ssd_norm, m_sc_conv_w, m_sc_norm, m_w_out, m_mem_norm, m_norm_xa, m_w_q, m_w_k, m_w_v, m_w_o, m_norm_ffn, m_w_gate, m_w_up, m_w_down, m_norm_final, v_norm_mix, v_w_in, v_ssd_conv_w, v_ssd_conv_b, v_dt_bias, v_a_log, v_d_skip, v_ssd_norm, v_sc_conv_w, v_sc_norm, v_w_out, v_mem_norm, v_norm_xa, v_w_q, v_w_k, v_w_v, v_w_o, v_norm_ffn, v_w_gate, v_w_up, v_w_down, v_norm_final):
    w = dict(norm_mix=norm_mix, w_in=w_in, ssd_conv_w=ssd_conv_w, ssd_conv_b=ssd_conv_b, dt_bias=dt_bias, a_log=a_log,
             d_skip=d_skip, ssd_norm=ssd_norm, sc_conv_w=sc_conv_w, sc_norm=sc_norm, w_out=w_out, mem_norm=mem_norm,
             norm_xa=norm_xa, w_q=w_q, w_k=w_k, w_v=w_v, w_o=w_o, norm_ffn=norm_ffn, w_gate=w_gate, w_up=w_up,
             w_down=w_down, norm_final=norm_final)
    mom = dict(norm_mix=m_norm_mix, w_in=m_w_in, ssd_conv_w=m_ssd_conv_w, ssd_conv_b=m_ssd_conv_b, dt_bias=m_dt_bias,
               a_log=m_a_log, d_skip=m_d_skip, ssd_norm=m_ssd_norm, sc_conv_w=m_sc_conv_w, sc_norm=m_sc_norm,
               w_out=m_w_out, mem_norm=m_mem_norm, norm_xa=m_norm_xa, w_q=m_w_q, w_k=m_w_k, w_v=m_w_v, w_o=m_w_o,
               norm_ffn=m_norm_ffn, w_gate=m_w_gate, w_up=m_w_up, w_down=m_w_down, norm_final=m_norm_final)
    var = dict(norm_mix=v_norm_mix, w_in=v_w_in, ssd_conv_w=v_ssd_conv_w, ssd_conv_b=v_ssd_conv_b, dt_bias=v_dt_bias,
               a_log=v_a_log, d_skip=v_d_skip, ssd_norm=v_ssd_norm, sc_conv_w=v_sc_conv_w, sc_norm=v_sc_norm,
               w_out=v_w_out, mem_norm=v_mem_norm, norm_xa=v_norm_xa, w_q=v_w_q, w_k=v_w_k, w_v=v_w_v, w_o=v_w_o,
               norm_ffn=v_norm_ffn, w_gate=v_w_gate, w_up=v_w_up, w_down=v_w_down, norm_final=v_norm_final)
    depth = w_in.shape[0]
    my = 4 * lax.axis_index("x") + 2 * lax.axis_index("y") + lax.axis_index("c")

    got = _run_side(_ag_side([w_in[0:1].astype(BF16), ssd_conv_w, sc_conv_w]), name="ag_first")
    comm = _Comm(w, got[0][0])
    conv_full = {"ssd_conv_w": _assemble(got[1], True), "sc_conv_w": _assemble(got[2], True)}
    layers, gains = [], []
    for li in range(depth):
        p = _prep_small(conv_full["ssd_conv_w"][li], ssd_conv_b[li], dt_bias[li], a_log[li], d_skip[li])
        p["sc_conv_w"] = conv_full["sc_conv_w"][li]
        layers.append(p)
        gains.append({nm: w[nm][li].reshape(1, D) for nm in ("norm_mix", "ssd_norm", "sc_norm", "norm_xa", "norm_ffn")})

    loss_v, grad_x, grads, d_mem_norm, d_final = _local_step(
        x[0], mem[0], loss_target[0], layers, gains, mem_norm.reshape(1, D), norm_final.reshape(1, D), comm)
    loss = lax.psum(loss_v[0, 0], ("x", "y", "c"))

    outs = {}
    for nm in BIG:
        outs[nm] = _adamw(comm.got[nm], w[nm], mom[nm], var[nm], name="adamw_" + nm)

    small = SMALL_REPL + SMALL_SHARDED
    gsmall = dict(mem_norm=d_mem_norm.reshape(D), norm_final=d_final.reshape(D))
    for nm in ("norm_mix", "ssd_norm", "sc_norm", "norm_xa", "norm_ffn"):
        gsmall[nm] = jnp.stack([gr[nm].reshape(D) for gr in grads])
    for nm in ("ssd_conv_b", "dt_bias", "a_log", "d_skip", "ssd_conv_w", "sc_conv_w"):
        gsmall[nm] = jnp.stack([gr[nm] for gr in grads])
    packed_g = _pack(gsmall, small)
    all_g = _run_side(_ag_side([packed_g[None]]), name="ag_small_grads")[0][0]

    def put_shard(arrs):
        loc = {nm: arrs[nm] for nm in SMALL_REPL}
        for nm in SMALL_SHARDED:
            cs = arrs[nm].shape[-1]
            loc[nm] = lax.dynamic_update_slice_in_dim(jnp.zeros(gsmall[nm].shape, F32), arrs[nm], my * cs, axis=2)
        return _pack(loc, small)

    res = _adamw([all_g], put_shard(w)[None], put_shard(mom)[None], put_shard(var)[None], name="adamw_small")
    shapes = {nm: gsmall[nm].shape for nm in small}
    for idx in range(4):
        un = _unpack(res[idx], shapes, small)
        for nm in SMALL_REPL:
            outs.setdefault(nm, [None] * 4)[idx] = un[nm]
        for nm in SMALL_SHARDED:
            cs = w[nm].shape[-1]
            outs.setdefault(nm, [None] * 4)[idx] = lax.dynamic_slice_in_dim(un[nm], my * cs, cs, axis=2)

    return (loss, grad_x[None], *[outs[nm][0] for nm in WEIGHTS], *[outs[nm][1] for nm in WEIGHTS],
            *[outs[nm][2] for nm in WEIGHTS], *[outs[nm][3] for nm in WEIGHTS])
```

```python
import functools

import jax
import jax.numpy as jnp
from jax import lax
from jax.experimental import pallas as pl
from jax.experimental.pallas import tpu as pltpu

F32 = jnp.float32
BF16 = jnp.bfloat16

D = 2048
HEAD_DIM = 64
N_HEADS = 32
N_GROUPS = 4
N_STATE = 128
CHUNK = 256
D_XBC = D + 2 * N_GROUPS * N_STATE
SSD_GROUP_W = D // N_GROUPS
SC_GROUP_W = 128
XA_HEADS = 4
XA_HD = 128
D_XA = XA_HEADS * XA_HD
D_FF = 5632
NORM_EPS = 1e-5
PROJ_W = 11264
DT_W = 128
N_DEV = 8

ADAM_LR = 0.001
ADAM_B1 = 0.9
ADAM_B2 = 0.999
ADAM_EPS = 1e-08
ADAM_WD = 0.01
ADAM_STEP = 10

TB = CHUNK
HALO = 8
LANES = 128
VMEM_LIMIT = 56 * 1024 * 1024

NT = (((1,), (1,)), ((), ()))
TN = (((0,), (0,)), ((), ()))
MESH = pl.DeviceIdType.MESH


def _cp(*sem):
    return pltpu.CompilerParams(dimension_semantics=sem, vmem_limit_bytes=VMEM_LIMIT)


def _sds(shape, dtype):
    return jax.ShapeDtypeStruct(shape, dtype)


def _sigmoid(x):
    return 1.0 / (1.0 + jnp.exp(-x))


def _split3_dot(t_bf16, x):
    hi = x.astype(BF16)
    r1 = x - hi.astype(F32)
    mid = r1.astype(BF16)
    lo = (r1 - mid.astype(F32)).astype(BF16)
    out = jnp.dot(t_bf16, hi, preferred_element_type=F32)
    out = out + jnp.dot(t_bf16, mid, preferred_element_type=F32)
    return out + jnp.dot(t_bf16, lo, preferred_element_type=F32)


def _split3_dot_r(x, t_bf16):
    hi = x.astype(BF16)
    r1 = x - hi.astype(F32)
    mid = r1.astype(BF16)
    lo = (r1 - mid.astype(F32)).astype(BF16)
    out = jnp.dot(hi, t_bf16, preferred_element_type=F32)
    out = out + jnp.dot(mid, t_bf16, preferred_element_type=F32)
    return out + jnp.dot(lo, t_bf16, preferred_element_type=F32)


def _group_bcast(stat_fn, v, gw):
    pieces = []
    for g in range(v.shape[1] // gw):
        vs = v[:, g * gw:(g + 1) * gw]
        pieces.append(jnp.broadcast_to(stat_fn(vs), vs.shape))
    return jnp.concatenate(pieces, axis=1) if len(pieces) > 1 else pieces[0]


def _group_rstd(v, gw):
    return _group_bcast(lambda s: lax.rsqrt(jnp.mean(s * s, axis=1, keepdims=True) + NORM_EPS), v, gw)


def _group_mean(v, gw):
    return _group_bcast(lambda s: jnp.mean(s, axis=1, keepdims=True), v, gw)


def _pair_sum(d, first):
    s0 = jnp.sum(jnp.where(first, d, 0.0), axis=1, keepdims=True)
    s1 = jnp.sum(jnp.where(first, 0.0, d), axis=1, keepdims=True)
    return jnp.where(first, s0, s1)


class _Side:
    def __init__(self, arrays, out_shapes, n_remote, n_local, start, finish):
        self.arrays, self.out_shapes = tuple(arrays), tuple(out_shapes)
        self.n_remote, self.n_local = n_remote, n_local
        self.start, self.finish = start, finish

    def scratch(self):
        return [pltpu.SemaphoreType.DMA((self.n_remote,)), pltpu.SemaphoreType.DMA((self.n_remote,)),
                pltpu.SemaphoreType.DMA((self.n_local,))]


def _ag_side(shards):
    n = len(shards)

    def plan(x_refs, out_refs, send, recv, local, starting=False):
        x, y, c = lax.axis_index("x"), lax.axis_index("y"), lax.axis_index("c")
        me, sibling = (x, y, c), (x, y, 1 - c)
        chips = [(1 - x, y), (x, 1 - y), (1 - x, 1 - y)]
        jobs = []
        for a in range(n):
            def rows(px, py, pc, out=out_refs[a]):
                return out.at[:, 4 * px + 2 * py + pc]

            def copy(k, block, to, src=None, a=a, rows=rows):
                return pltpu.make_async_remote_copy(
                    src_ref=rows(*block) if src is None else src, dst_ref=rows(*block),
                    send_sem=send.at[7 * a + k], recv_sem=recv.at[7 * a + k], device_id=to, device_id_type=MESH)

            mine = pltpu.make_async_copy(x_refs[a], rows(*me), local.at[a])
            first = [copy(0, me, sibling, src=x_refs[a])]
            first += [copy(1 + j, me, (*chip, c), src=x_refs[a]) for j, chip in enumerate(chips)]
            if starting:
                jobs.append((mine, first))
                continue
            passed = [copy(4 + j, (*chip, c), sibling) for j, chip in enumerate(chips)]
            arrive = [copy(1 + j, (*chip, c), me) for j, chip in enumerate(chips)]
            late = [copy(0, sibling, me)] + [copy(4 + j, (*chip, 1 - c), me) for j, chip in enumerate(chips)]
            jobs.append((mine, first, passed, arrive, late))
        return jobs

    def start(*refs):
        for mine, first in plan(*refs, starting=True):
            mine.start()
            for cp in first:
                cp.start()

    def finish(*refs):
        jobs = plan(*refs)
        for j in range(3):
            for _, _, passed, arrive, _ in jobs:
                arrive[j].wait_recv()
                passed[j].start()
        for mine, first, passed, _, late in jobs:
            for cp in late:
                cp.wait_recv()
            for cp in first + passed:
                cp.wait_send()
            mine.wait()

    outs = [_sds((s.shape[0], N_DEV) + s.shape[1:], s.dtype) for s in shards]
    return _Side(shards, outs, 7 * n, n, start, finish)


def _flip(k, x, y, c):
    return (1 - x if k & 4 else x, 1 - y if k & 2 else y, 1 - c if k & 1 else c)


def _rs_side(pieces):
    n = len(pieces)

    def plan(g_refs, out_refs, send, recv, local):
        x, y, c = lax.axis_index("x"), lax.axis_index("y"), lax.axis_index("c")
        me = 4 * x + 2 * y + c
        jobs = []
        for a in range(n):
            mine = pltpu.make_async_copy(g_refs[a].at[me], out_refs[a].at[me], local.at[a])
            copies = []
            for k in range(1, N_DEV):
                px, py, pc = _flip(k, x, y, c)
                copies.append(pltpu.make_async_remote_copy(
                    src_ref=g_refs[a].at[4 * px + 2 * py + pc], dst_ref=out_refs[a].at[me],
                    send_sem=send.at[7 * a + k - 1], recv_sem=recv.at[7 * a + k - 1],
                    device_id=(px, py, pc), device_id_type=MESH))
            jobs.append((mine, copies))
        return jobs

    def start(*refs):
        for mine, copies in plan(*refs):
            mine.start()
            for cp in copies:
                cp.start()

    def finish(*refs):
        for mine, copies in plan(*refs):
            for cp in copies:
                cp.wait_recv()
            for cp in copies:
                cp.wait_send()
            mine.wait()

    return _Side(pieces, [_sds(p.shape, p.dtype) for p in pieces], 7 * n, n, start, finish)


def _run_side(side, *, name):
    n_in, n_out = len(side.arrays), len(side.out_shapes)

    def body(*refs):
        parts = (refs[:n_in], refs[n_in:n_in + n_out]) + tuple(refs[n_in + n_out:])
        side.start(*parts)
        side.finish(*parts)

    hbm = pl.BlockSpec(memory_space=pl.ANY)
    return pl.pallas_call(
        body, name=name, out_shape=side.out_shapes, in_specs=[hbm] * n_in, out_specs=tuple([hbm] * n_out),
        scratch_shapes=side.scratch(),
    )(*side.arrays)


def _mm(a, b, *, name, ta=False, tb=False, res=None, out_dtype=F32, bm=1024, bn=1024, bk=None, j_outer=False,
        side=None):
    if ta:
        k_dim, m_dim = a.shape
    else:
        m_dim, k_dim = a.shape
    if tb:
        n_dim, kb = b.shape
    else:
        kb, n_dim = b.shape
    assert k_dim == kb, (a.shape, b.shape)
    bm, bn = min(bm, m_dim), min(bn, n_dim)
    bk = k_dim if bk is None else min(bk, k_dim)
    assert m_dim % bm == 0 and n_dim % bn == 0 and k_dim % bk == 0, (name, a.shape, b.shape, bm, bn, bk)
    ni, nj, nk = m_dim // bm, n_dim // bn, k_dim // bk
    grid = (nj, ni, nk) if j_outer else (ni, nj, nk)
    dn = (((0,) if ta else (1,), (1,) if tb else (0,)), ((), ()))
    has_res = res is not None
    n_main_in = 2 + has_res
    n_side_in = len(side.arrays) if side else 0
    n_side_out = len(side.out_shapes) if side else 0

    def body(*refs):
        a_ref, b_ref = refs[0], refs[1]
        r_ref = refs[2] if has_res else None
        o_ref = refs[n_main_in + n_side_in]
        scratch = refs[n_main_in + n_side_in + 1 + n_side_out:]
        if side:
            side_refs = (refs[n_main_in:n_main_in + n_side_in],
                         refs[n_main_in + n_side_in + 1:n_main_in + n_side_in + 1 + n_side_out]) + tuple(scratch[-3:])
            step = (pl.program_id(0) * grid[1] + pl.program_id(1)) * grid[2] + pl.program_id(2)

            @pl.when(step == 0)
            def _():
                side.start(*side_refs)

        p = lax.dot_general(a_ref[...].astype(BF16), b_ref[...].astype(BF16), dn, preferred_element_type=F32)
        if nk == 1:
            if has_res:
                p = p + r_ref[...]
            o_ref[...] = p.astype(o_ref.dtype)
        else:
            acc_ref = scratch[0]
            k = pl.program_id(2)

            @pl.when(k == 0)
            def _():
                acc_ref[...] = p

            @pl.when(k > 0)
            def _():
                acc_ref[...] += p

            @pl.when(k == nk - 1)
            def _():
                r = acc_ref[...]
                if has_res:
                    r = r + r_ref[...]
                o_ref[...] = r.astype(o_ref.dtype)

        if side:
            @pl.when(step == grid[0] * grid[1] * grid[2] - 1)
            def _():
                side.finish(*side_refs)

    def spec(shape, index):
        if j_outer:
            return pl.BlockSpec(shape, lambda j, i, k: index(i, j, k))
        return pl.BlockSpec(shape, index)

    a_spec = spec((bk, bm), lambda i, j, k: (k, i)) if ta else spec((bm, bk), lambda i, j, k: (i, k))
    b_spec = spec((bn, bk), lambda i, j, k: (j, k)) if tb else spec((bk, bn), lambda i, j, k: (k, j))
    o_spec = spec((bm, bn), lambda i, j, k: (i, j))
    hbm = pl.BlockSpec(memory_space=pl.ANY)
    in_specs = [a_spec, b_spec] + ([o_spec] if has_res else []) + [hbm] * n_side_in
    args = (a, b) + ((res,) if has_res else ()) + (side.arrays if side else ())
    scratch_shapes = ([pltpu.VMEM((bm, bn), F32)] if nk > 1 else []) + (side.scratch() if side else [])
    out_main = _sds((m_dim, n_dim), out_dtype)
    if not side:
        return pl.pallas_call(
            body, name=name, grid=grid, in_specs=in_specs, out_specs=o_spec, out_shape=out_main,
            scratch_shapes=scratch_shapes, compiler_params=_cp("parallel", "parallel", "arbitrary"),
        )(*args)
    outs = pl.pallas_call(
        body, name=name, grid=grid, in_specs=in_specs, out_specs=(o_spec,) + tuple([hbm] * n_side_out),
        out_shape=(out_main,) + side.out_shapes, scratch_shapes=scratch_shapes,
        compiler_params=_cp("arbitrary", "arbitrary", "arbitrary"),
    )(*args)
    return outs[0], tuple(outs[1:])


def _norm_fwd(h, g, *, name, out_dtype=BF16):
    s, d = h.shape
    tb = min(TB, s)

    def body(h_ref, g_ref, o_ref):
        x = h_ref[...]
        r = lax.rsqrt(jnp.mean(x * x, axis=-1, keepdims=True) + NORM_EPS)
        o_ref[...] = (x * r * g_ref[...]).astype(o_ref.dtype)

    row = pl.BlockSpec((tb, d), lambda i: (i, 0))
    return pl.pallas_call(
        body, name=name, grid=(s // tb,), in_specs=[row, pl.BlockSpec((1, d), lambda i: (0, 0))], out_specs=row,
        out_shape=_sds((s, d), out_dtype), compiler_params=_cp("parallel"),
    )(h, g)


def _norm_bwd(h, g, dhn, dres, *, name):
    s, d = h.shape
    tb = min(TB, s)
    has_res = dres is not None

    def body(*refs):
        h_ref, g_ref, dhn_ref = refs[:3]
        r_ref = refs[3] if has_res else None
        dh_ref, dhb_ref, dg_ref = refs[3 + has_res:]
        x = h_ref[...]
        r = lax.rsqrt(jnp.mean(x * x, axis=-1, keepdims=True) + NORM_EPS)
        xhat = x * r
        dy = dhn_ref[...].astype(F32)
        gy = dy * g_ref[...]
        dx = r * (gy - xhat * jnp.mean(gy * xhat, axis=-1, keepdims=True))
        if has_res:
            dx = dx + r_ref[...]
        dh_ref[...] = dx
        dhb_ref[...] = dx.astype(BF16)
        part = jnp.sum(dy * xhat, axis=0, keepdims=True)

        @pl.when(pl.program_id(0) == 0)
        def _():
            dg_ref[...] = part

        @pl.when(pl.program_id(0) > 0)
        def _():
            dg_ref[...] += part

    row = pl.BlockSpec((tb, d), lambda i: (i, 0))
    vec = pl.BlockSpec((1, d), lambda i: (0, 0))
    return pl.pallas_call(
        body, name=name, grid=(s // tb,), in_specs=[row, vec, row] + ([row] if has_res else []),
        out_specs=(row, row, vec), out_shape=(_sds((s, d), F32), _sds((s, d), BF16), _sds((1, d), F32)),
        compiler_params=_cp("arbitrary"),
    )(*((h, g, dhn) + ((dres,) if has_res else ())))


def _loss_head(h, g, tgt, *, name):
    s, d = h.shape
    tb = min(TB, s)

    def body(h_ref, g_ref, t_ref, loss_ref, dh_ref, dhb_ref, dg_ref):
        x = h_ref[...]
        r = lax.rsqrt(jnp.mean(x * x, axis=-1, keepdims=True) + NORM_EPS)
        xhat = x * r
        gain = g_ref[...]
        err = xhat * gain - t_ref[...]
        part_loss = 0.5 * jnp.sum(jnp.mean(err * err, axis=-1, keepdims=True), axis=0, keepdims=True)
        dy = err * (1.0 / d)
        gy = dy * gain
        dx = r * (gy - xhat * jnp.mean(gy * xhat, axis=-1, keepdims=True))
        dh_ref[...] = dx
        dhb_ref[...] = dx.astype(BF16)
        part = jnp.sum(dy * xhat, axis=0, keepdims=True)
        lossv = jnp.broadcast_to(part_loss, (1, LANES))

        @pl.when(pl.program_id(0) == 0)
        def _():
            dg_ref[...] = part
            loss_ref[...] = lossv

        @pl.when(pl.program_id(0) > 0)
        def _():
            dg_ref[...] += part
            loss_ref[...] += lossv

    row = pl.BlockSpec((tb, d), lambda i: (i, 0))
    vec = pl.BlockSpec((1, d), lambda i: (0, 0))
    return pl.pallas_call(
        body, name=name, grid=(s // tb,), in_specs=[row, vec, row],
        out_specs=(pl.BlockSpec((1, LANES), lambda i: (0, 0)), row, row, vec),
        out_shape=(_sds((1, LANES), F32), _sds((s, d), F32), _sds((s, d), BF16), _sds((1, d), F32)),
        compiler_params=_cp("arbitrary"),
    )(h, g, tgt)


def _swiglu_fwd(gu, *, name):
    s = gu.shape[0]
    tb = min(TB, s)

    def body(g_ref, u_ref, o_ref):
        g = g_ref[...].astype(F32)
        o_ref[...] = (g * _sigmoid(g) * u_ref[...].astype(F32)).astype(o_ref.dtype)

    return pl.pallas_call(
        body, name=name, grid=(s // tb,),
        in_specs=[pl.BlockSpec((tb, D_FF), lambda i: (i, 0)), pl.BlockSpec((tb, D_FF), lambda i: (i, 1))],
        out_specs=pl.BlockSpec((tb, D_FF), lambda i: (i, 0)), out_shape=_sds((s, D_FF), BF16),
        compiler_params=_cp("parallel"),
    )(gu, gu)


def _swiglu_bwd(gu, dact, *, name):
    s = gu.shape[0]
    tb = min(TB, s)

    def body(g_ref, u_ref, d_ref, o_ref):
        g = g_ref[...].astype(F32)
        sg = _sigmoid(g)
        da = d_ref[...].astype(F32)
        o_ref[:, :D_FF] = (da * u_ref[...].astype(F32) * sg * (1.0 + g * (1.0 - sg))).astype(o_ref.dtype)
        o_ref[:, D_FF:] = (da * g * sg).astype(o_ref.dtype)

    return pl.pallas_call(
        body, name=name, grid=(s // tb,),
        in_specs=[pl.BlockSpec((tb, D_FF), lambda i: (i, 0)), pl.BlockSpec((tb, D_FF), lambda i: (i, 1)),
                  pl.BlockSpec((tb, D_FF), lambda i: (i, 0))],
        out_specs=pl.BlockSpec((tb, 2 * D_FF), lambda i: (i, 0)), out_shape=_sds((s, 2 * D_FF), BF16),
        compiler_params=_cp("parallel"),
    )(gu, gu, dact)


def _softmax_rows(qh, kh):
    sc = lax.dot_general(qh, kh, NT, preferred_element_type=F32) * (XA_HD ** -0.5)
    sc = sc - jnp.max(sc, axis=-1, keepdims=True)
    e = jnp.exp(sc)
    return e / jnp.sum(e, axis=-1, keepdims=True)


def _attn_fwd(q, k, v, *, name):
    s = q.shape[0]
    n_mem = k.shape[0]
    tq = min(512, s)

    def body(q_ref, k_ref, v_ref, o_ref):
        outs = []
        for h in range(XA_HEADS):
            sl = slice(h * XA_HD, (h + 1) * XA_HD)
            p = _softmax_rows(q_ref[:, sl], k_ref[:, sl])
            outs.append(jnp.dot(p.astype(BF16), v_ref[:, sl], preferred_element_type=F32))
        o_ref[...] = jnp.concatenate(outs, axis=1).astype(o_ref.dtype)

    row = pl.BlockSpec((tq, D_XA), lambda i: (i, 0))
    kv = pl.BlockSpec((n_mem, D_XA), lambda i: (0, 0))
    return pl.pallas_call(
        body, name=name, grid=(s // tq,), in_specs=[row, kv, kv], out_specs=row, out_shape=_sds((s, D_XA), BF16),
        compiler_params=_cp("parallel"),
    )(q, k, v)


def _attn_bwd(q, k, v, do, *, name):
    s = q.shape[0]
    n_mem = k.shape[0]
    tq = min(512, s)

    def body(q_ref, k_ref, v_ref, do_ref, dq_ref, dk_ref, dv_ref):
        dqs, dks, dvs = [], [], []
        for h in range(XA_HEADS):
            sl = slice(h * XA_HD, (h + 1) * XA_HD)
            qh, kh, vh, doh = q_ref[:, sl], k_ref[:, sl], v_ref[:, sl], do_ref[:, sl]
            p = _softmax_rows(qh, kh)
            dvs.append(lax.dot_general(p.astype(BF16), doh, TN, preferred_element_type=F32))
            dp = lax.dot_general(doh, vh, NT, preferred_element_type=F32)
            ds = (p * (dp - jnp.sum(dp * p, axis=-1, keepdims=True)) * (XA_HD ** -0.5)).astype(BF16)
            dqs.append(jnp.dot(ds, kh, preferred_element_type=F32))
            dks.append(lax.dot_general(ds, qh, TN, preferred_element_type=F32))
        dq_ref[...] = jnp.concatenate(dqs, axis=1).astype(dq_ref.dtype)
        dk = jnp.concatenate(dks, axis=1)
        dv = jnp.concatenate(dvs, axis=1)

        @pl.when(pl.program_id(0) == 0)
        def _():
            dk_ref[...] = dk
            dv_ref[...] = dv

        @pl.when(pl.program_id(0) > 0)
        def _():
            dk_ref[...] += dk
            dv_ref[...] += dv

    row = pl.BlockSpec((tq, D_XA), lambda i: (i, 0))
    kv = pl.BlockSpec((n_mem, D_XA), lambda i: (0, 0))
    return pl.pallas_call(
        body, name=name, grid=(s // tq,), in_specs=[row, kv, kv, row], out_specs=(row, kv, kv),
        out_shape=(_sds((s, D_XA), BF16), _sds((n_mem, D_XA), F32), _sds((n_mem, D_XA), F32)),
        compiler_params=_cp("arbitrary"),
    )(q, k, v, do)


CONV_CB = 1024
XBC_CB0 = D // CONV_CB


def _prev_rows(i, tb):
    return jnp.maximum(i * (tb // HALO) - 1, 0)


def _next_rows(i, tb, s):
    return jnp.minimum((i + 1) * (tb // HALO), s // HALO - 1)


def _taps(xcat, width):
    return [pltpu.roll(xcat, width - 1 - k, 0) for k in range(width - 1)] + [xcat]


def _conv_taps(taps, w_ref, lo):
    acc = w_ref[0:1, :] * taps[0][lo:]
    for k in range(1, len(taps)):
        acc = acc + w_ref[k:k + 1, :] * taps[k][lo:]
    return acc


def _conv_silu_fwd(proj, w, b, *, name):
    s = proj.shape[0]
    tb = min(TB, s)

    def body(x_ref, xp_ref, w_ref, b_ref, o_ref):
        i = pl.program_id(0)
        prev = jnp.where(i > 0, xp_ref[...], 0.0)
        pre = _conv_taps(_taps(jnp.concatenate([prev, x_ref[...]], axis=0), 4), w_ref, HALO) + b_ref[...]
        o_ref[...] = pre * _sigmoid(pre)

    return pl.pallas_call(
        body, name=name, grid=(s // tb, D_XBC // CONV_CB),
        in_specs=[pl.BlockSpec((tb, CONV_CB), lambda i, j: (i, XBC_CB0 + j)),
                  pl.BlockSpec((HALO, CONV_CB), lambda i, j: (_prev_rows(i, tb), XBC_CB0 + j)),
                  pl.BlockSpec((4, CONV_CB), lambda i, j: (0, j)),
                  pl.BlockSpec((1, CONV_CB), lambda i, j: (0, j))],
        out_specs=pl.BlockSpec((tb, CONV_CB), lambda i, j: (i, j)), out_shape=_sds((s, D_XBC), F32),
        compiler_params=_cp("parallel", "parallel"),
    )(proj, proj, w, b)


def _conv_silu_bwd(proj, dxs, d_b, d_c, w, b, *, name):
    s = proj.shape[0]
    tb = min(TB, s)
    n_i = s // tb
    n_xs = D // CONV_CB
    bc_w = N_GROUPS * N_STATE

    def body(x_ref, xp_ref, xn_ref, dxs_ref, dxsn_ref, dbm_ref, dbmn_ref, dcm_ref, dcmn_ref, w_ref, b_ref,
             dx_ref, dw_ref, db_ref):
        j, i = pl.program_id(0), pl.program_id(1)
        prev = jnp.where(i > 0, xp_ref[...], 0.0)
        taps = _taps(jnp.concatenate([prev, x_ref[...], xn_ref[...]], axis=0), 4)
        pre = _conv_taps(taps, w_ref, HALO) + b_ref[...]
        sg = _sigmoid(pre)
        dy = jnp.where(j < n_xs, dxs_ref[...], jnp.concatenate([dbm_ref[...], dcm_ref[...]], axis=1))
        dyn = jnp.where(j < n_xs, dxsn_ref[...], jnp.concatenate([dbmn_ref[...], dcmn_ref[...]], axis=1))
        dy_ext = jnp.concatenate([dy, jnp.where(i < n_i - 1, dyn, 0.0)], axis=0)
        dpre = dy_ext * sg * (1.0 + pre * (1.0 - sg))
        n2 = tb + HALO
        dx = w_ref[3:4, :] * dpre[:tb]
        for k in range(3):
            dx = dx + w_ref[k:k + 1, :] * pltpu.roll(dpre, n2 - (3 - k), 0)[:tb]
        dx_ref[...] = dx.astype(dx_ref.dtype)
        dpc = dpre[:tb]
        parts = [jnp.sum(dpc * taps[k][HALO:HALO + tb], axis=0, keepdims=True) for k in range(4)]
        dbp = jnp.sum(dpc, axis=0, keepdims=True)

        @pl.when(i == 0)
        def _():
            for k in range(4):
                dw_ref[k] = parts[k]
            db_ref[...] = dbp

        @pl.when(i > 0)
        def _():
            for k in range(4):
                dw_ref[k] += parts[k]
            db_ref[...] += dbp

    return pl.pallas_call(
        body, name=name, grid=(D_XBC // CONV_CB, n_i),
        in_specs=[pl.BlockSpec((tb, CONV_CB), lambda j, i: (i, XBC_CB0 + j)),
                  pl.BlockSpec((HALO, CONV_CB), lambda j, i: (_prev_rows(i, tb), XBC_CB0 + j)),
                  pl.BlockSpec((HALO, CONV_CB), lambda j, i: (_next_rows(i, tb, s), XBC_CB0 + j)),
                  pl.BlockSpec((tb, CONV_CB), lambda j, i: (jnp.where(j < n_xs, i, 0), jnp.minimum(j, n_xs - 1))),
                  pl.BlockSpec((HALO, CONV_CB),
                               lambda j, i: (jnp.where(j < n_xs, _next_rows(i, tb, s), 0), jnp.minimum(j, n_xs - 1))),
                  pl.BlockSpec((tb, bc_w), lambda j, i: (jnp.where(j < n_xs, 0, i), 0)),
                  pl.BlockSpec((HALO, bc_w), lambda j, i: (jnp.where(j < n_xs, 0, _next_rows(i, tb, s)), 0)),
                  pl.BlockSpec((tb, bc_w), lambda j, i: (jnp.where(j < n_xs, 0, i), 0)),
                  pl.BlockSpec((HALO, bc_w), lambda j, i: (jnp.where(j < n_xs, 0, _next_rows(i, tb, s)), 0)),
                  pl.BlockSpec((4, CONV_CB), lambda j, i: (0, j)),
                  pl.BlockSpec((1, CONV_CB), lambda j, i: (0, j))],
        out_specs=(pl.BlockSpec((tb, CONV_CB), lambda j, i: (i, j)),
                   pl.BlockSpec((4, 1, CONV_CB), lambda j, i: (0, 0, j)),
                   pl.BlockSpec((1, CONV_CB), lambda j, i: (0, j))),
        out_shape=(_sds((s, D_XBC), BF16), _sds((4, 1, D_XBC), F32), _sds((1, D_XBC), F32)),
        compiler_params=_cp("parallel", "arbitrary"),
    )(proj, proj, proj, dxs, dxs, d_b, d_b, d_c, d_c, w, b)


SC_U0 = (D + D_XBC) // CONV_CB
SC_B0 = SC_U0 + D // CONV_CB
SC_C0 = SC_B0 + D // CONV_CB


def _sc_fwd(proj, w, gain, *, name):
    s = proj.shape[0]
    tb = min(TB, s)

    def body(u_ref, b_ref, c_ref, up_ref, cp_ref, w_ref, g_ref, o_ref):
        i = pl.program_id(0)
        cup = jnp.where(i > 0, cp_ref[...] * up_ref[...], 0.0)
        cat = jnp.concatenate([cup, c_ref[...] * u_ref[...]], axis=0)
        v = b_ref[...] * _conv_taps(_taps(cat, 3), w_ref, HALO)
        o_ref[...] = (v * _group_rstd(v, SC_GROUP_W) * g_ref[...]).astype(o_ref.dtype)

    def cur(c0):
        return pl.BlockSpec((tb, CONV_CB), lambda i, j: (i, c0 + j))

    def prev(c0):
        return pl.BlockSpec((HALO, CONV_CB), lambda i, j: (_prev_rows(i, tb), c0 + j))

    return pl.pallas_call(
        body, name=name, grid=(s // tb, D // CONV_CB),
        in_specs=[cur(SC_U0), cur(SC_B0), cur(SC_C0), prev(SC_U0), prev(SC_C0),
                  pl.BlockSpec((3, CONV_CB), lambda i, j: (0, j)), pl.BlockSpec((1, CONV_CB), lambda i, j: (0, j))],
        out_specs=pl.BlockSpec((tb, CONV_CB), lambda i, j: (i, j)), out_shape=_sds((s, D), BF16),
        compiler_params=_cp("parallel", "parallel"),
    )(proj, proj, proj, proj, proj, w, gain)


def _sc_bwd(proj, dcat, w, gain, *, name):
    s = proj.shape[0]
    tb = min(TB, s)
    n_i = s // tb
    dy0 = D // CONV_CB

    def body(u_ref, b_ref, c_ref, up_ref, cp_ref, un_ref, bn_ref, cn_ref, dy_ref, dyn_ref, w_ref, g_ref,
             du_ref, db_ref, dc_ref, dw_ref, dg_ref):
        i = pl.program_id(1)
        u, c = u_ref[...], c_ref[...]
        cup = jnp.where(i > 0, cp_ref[...] * up_ref[...], 0.0)
        taps = _taps(jnp.concatenate([cup, c * u, cn_ref[...] * un_ref[...]], axis=0), 3)
        conv = _conv_taps(taps, w_ref, HALO)
        b_ext = jnp.concatenate([b_ref[...], bn_ref[...]], axis=0)
        dy_ext = jnp.concatenate([dy_ref[...].astype(F32), jnp.where(i < n_i - 1, dyn_ref[...].astype(F32), 0.0)],
                                 axis=0)
        v = b_ext * conv
        r = _group_rstd(v, SC_GROUP_W)
        vhat = v * r
        dvhat = dy_ext * g_ref[...]
        dv = r * (dvhat - vhat * _group_mean(dvhat * vhat, SC_GROUP_W))
        dconv = dv * b_ext
        n2 = tb + HALO
        dcu = (w_ref[2:3, :] * dconv[:tb] + w_ref[1:2, :] * pltpu.roll(dconv, n2 - 1, 0)[:tb]
               + w_ref[0:1, :] * pltpu.roll(dconv, n2 - 2, 0)[:tb])
        du_ref[...] = (dcu * c).astype(du_ref.dtype)
        dc_ref[...] = (dcu * u).astype(dc_ref.dtype)
        db_ref[...] = (dv * conv)[:tb].astype(db_ref.dtype)
        dcc = dconv[:tb]
        parts = [jnp.sum(dcc * taps[k][HALO:HALO + tb], axis=0, keepdims=True) for k in range(3)]
        dgp = jnp.sum((dy_ext * vhat)[:tb], axis=0, keepdims=True)

        @pl.when(i == 0)
        def _():
            for k in range(3):
                dw_ref[k] = parts[k]
            dg_ref[...] = dgp

        @pl.when(i > 0)
        def _():
            for k in range(3):
                dw_ref[k] += parts[k]
            dg_ref[...] += dgp

    def cur(c0):
        return pl.BlockSpec((tb, CONV_CB), lambda j, i: (i, c0 + j))

    def prev(c0):
        return pl.BlockSpec((HALO, CONV_CB), lambda j, i: (_prev_rows(i, tb), c0 + j))

    def nxt(c0):
        return pl.BlockSpec((HALO, CONV_CB), lambda j, i: (_next_rows(i, tb, s), c0 + j))

    vec = pl.BlockSpec((1, CONV_CB), lambda j, i: (0, j))
    out_row = pl.BlockSpec((tb, CONV_CB), lambda j, i: (i, j))
    return pl.pallas_call(
        body, name=name, grid=(D // CONV_CB, n_i),
        in_specs=[cur(SC_U0), cur(SC_B0), cur(SC_C0), prev(SC_U0), prev(SC_C0), nxt(SC_U0), nxt(SC_B0), nxt(SC_C0),
                  cur(dy0), nxt(dy0), pl.BlockSpec((3, CONV_CB), lambda j, i: (0, j)), vec],
        out_specs=(out_row, out_row, out_row, pl.BlockSpec((3, 1, CONV_CB), lambda j, i: (0, 0, j)), vec),
        out_shape=(_sds((s, D), BF16), _sds((s, D), BF16), _sds((s, D), BF16), _sds((3, 1, D), F32), _sds((1, D), F32)),
        compiler_params=_cp("parallel", "arbitrary"),
    )(proj, proj, proj, proj, proj, proj, proj, proj, dcat, dcat, w, gain)


def _gated_norm_fwd(y, proj, gain, *, name):
    s = y.shape[0]
    tb = min(TB, s)

    def body(y_ref, z_ref, g_ref, o_ref):
        z = z_ref[...]
        t = y_ref[...] * z * _sigmoid(z)
        o_ref[...] = (t * _group_rstd(t, SSD_GROUP_W) * g_ref[...]).astype(o_ref.dtype)

    row = pl.BlockSpec((tb, D), lambda i: (i, 0))
    return pl.pallas_call(
        body, name=name, grid=(s // tb,), in_specs=[row, row, pl.BlockSpec((1, D), lambda i: (0, 0))], out_specs=row,
        out_shape=_sds((s, D), BF16), compiler_params=_cp("parallel"),
    )(y, proj, gain)


def _gated_norm_bwd(y, proj, dcat, gain, *, name):
    s = y.shape[0]
    tb = min(TB, s)

    def body(y_ref, z_ref, d_ref, g_ref, dy_ref, dz_ref, dg_ref):
        z, yv, dout = z_ref[...], y_ref[...], d_ref[...].astype(F32)
        sg = _sigmoid(z)
        sz = z * sg
        t = yv * sz
        r = _group_rstd(t, SSD_GROUP_W)
        that = t * r
        dthat = dout * g_ref[...]
        dt = r * (dthat - that * _group_mean(dthat * that, SSD_GROUP_W))
        dy_ref[...] = dt * sz
        dz_ref[...] = (dt * yv * sg * (1.0 + z * (1.0 - sg))).astype(dz_ref.dtype)
        part = jnp.sum(dout * that, axis=0, keepdims=True)

        @pl.when(pl.program_id(0) == 0)
        def _():
            dg_ref[...] = part

        @pl.when(pl.program_id(0) > 0)
        def _():
            dg_ref[...] += part

    row = pl.BlockSpec((tb, D), lambda i: (i, 0))
    vec = pl.BlockSpec((1, D), lambda i: (0, 0))
    return pl.pallas_call(
        body, name=name, grid=(s // tb,), in_specs=[row, row, row, vec], out_specs=(row, row, vec),
        out_shape=(_sds((s, D), F32), _sds((s, D), BF16), _sds((1, D), F32)), compiler_params=_cp("arbitrary"),
    )(y, proj, dcat, gain)


def _tri(lower):
    row = lax.broadcasted_iota(jnp.int32, (CHUNK, CHUNK), 0)
    col = lax.broadcasted_iota(jnp.int32, (CHUNK, CHUNK), 1)
    return jnp.where(row >= col if lower else col >= row, 1.0, 0.0).astype(BF16)


def _head_spread():
    row = lax.broadcasted_iota(jnp.int32, (DT_W, D), 0)
    col = lax.broadcasted_iota(jnp.int32, (DT_W, D), 1)
    return jnp.where(col // HEAD_DIM == row, 1.0, 0.0).astype(BF16)


def _head_pick():
    row = lax.broadcasted_iota(jnp.int32, (D, DT_W), 0)
    col = lax.broadcasted_iota(jnp.int32, (D, DT_W), 1)
    return jnp.where(row == col * HEAD_DIM, 1.0, 0.0).astype(BF16)


def _dt_fwd(dt_raw, bias, a_log, *, name):
    s = dt_raw.shape[0]
    nc = s // CHUNK

    def body(raw_ref, bias_ref, alog_ref, dt_ref, cum_ref, cumt_ref):
        x = raw_ref[...] + bias_ref[...]
        dt = jnp.maximum(x, 0.0) + jnp.log1p(jnp.exp(-jnp.abs(x)))
        cum = _split3_dot(_tri(True), dt * (-jnp.exp(alog_ref[...])))
        spread = _head_spread()
        dt_ref[...] = _split3_dot_r(dt, spread)
        cum_ref[...] = _split3_dot_r(cum, spread)
        cumt_ref[0] = cum.T

    row = pl.BlockSpec((CHUNK, DT_W), lambda i: (i, 0))
    wide = pl.BlockSpec((CHUNK, D), lambda i: (i, 0))
    vec = pl.BlockSpec((1, DT_W), lambda i: (0, 0))
    return pl.pallas_call(
        body, name=name, grid=(nc,), in_specs=[row, vec, vec],
        out_specs=(wide, wide, pl.BlockSpec((1, DT_W, CHUNK), lambda i: (i, 0, 0))),
        out_shape=(_sds((s, D), F32), _sds((s, D), F32), _sds((nc, DT_W, CHUNK), F32)),
        compiler_params=_cp("parallel"),
    )(dt_raw, bias, a_log)


def _dt_bwd(dt_raw, bias, ddt_b, *, name):
    s = dt_raw.shape[0]
    tb = min(TB, s)

    def body(raw_ref, bias_ref, d_ref, o_ref, db_ref):
        g = _split3_dot_r(d_ref[...], _head_pick()) * _sigmoid(raw_ref[...] + bias_ref[...])
        o_ref[...] = g.astype(o_ref.dtype)
        part = jnp.sum(g, axis=0, keepdims=True)

        @pl.when(pl.program_id(0) == 0)
        def _():
            db_ref[...] = part

        @pl.when(pl.program_id(0) > 0)
        def _():
            db_ref[...] += part

    row = pl.BlockSpec((tb, DT_W), lambda i: (i, 0))
    vec = pl.BlockSpec((1, DT_W), lambda i: (0, 0))
    return pl.pallas_call(
        body, name=name, grid=(s // tb,), in_specs=[row, vec, pl.BlockSpec((tb, D), lambda i: (i, 0))],
        out_specs=(row, vec), out_shape=(_sds((s, DT_W), BF16), _sds((1, DT_W), F32)),
        compiler_params=_cp("arbitrary"),
    )(dt_raw, bias, ddt_b)


N_PAIRS = N_HEADS // 2
PAIRS_PER_GROUP = N_PAIRS // N_GROUPS
GROUP_W = PAIRS_PER_GROUP * LANES
B_CB0 = D // LANES
C_CB0 = B_CB0 + N_GROUPS


def _decay(cum_col, cum_row, causal):
    return jnp.where(causal, jnp.exp(jnp.minimum(cum_col - cum_row, 0.0)), 0.0)


def _causal_mask():
    row = lax.broadcasted_iota(jnp.int32, (CHUNK, CHUNK), 0)
    col = lax.broadcasted_iota(jnp.int32, (CHUNK, CHUNK), 1)
    return row >= col


def _state_row_scale(cumt_ref, pp):
    last0 = cumt_ref[0, 2 * pp][:, CHUNK - 1:CHUNK]
    last1 = cumt_ref[0, 2 * pp + 1][:, CHUNK - 1:CHUNK]
    rown = lax.broadcasted_iota(jnp.int32, (LANES, 1), 0)
    return jnp.exp(jnp.where(rown < HEAD_DIM, last0, last1))


def _ssd_specs(nc, rev):
    def ch(c):
        return nc - 1 - c if rev else c

    wide = pl.BlockSpec((CHUNK, GROUP_W), lambda c, g: (ch(c), g))
    vec = pl.BlockSpec((1, GROUP_W), lambda c, g: (0, g))
    cumt_spec = pl.BlockSpec((1, 2 * PAIRS_PER_GROUP, 1, CHUNK), lambda c, g: (ch(c), g, 0, 0))
    hp_spec = pl.BlockSpec((1, PAIRS_PER_GROUP, LANES, N_STATE), lambda c, g: (ch(c), g, 0, 0))

    def bc(c0):
        return pl.BlockSpec((CHUNK, LANES), lambda c, g: (ch(c), c0 + g))

    return wide, vec, cumt_spec, hp_spec, bc


def _ssd_fwd(xbc_c, dt_b, cum_b, cumt4, dskip_b, *, name):
    s = xbc_c.shape[0]
    nc = s // CHUNK

    def body(xs_ref, b_ref, c_ref, dt_ref, cum_ref, cumt_ref, dsk_ref, y_ref, hp_ref, state):
        c, g = pl.program_id(0), pl.program_id(1)

        @pl.when(c == 0)
        def _():
            state[pl.ds(g * PAIRS_PER_GROUP, PAIRS_PER_GROUP)] = jnp.zeros((PAIRS_PER_GROUP, LANES, N_STATE), F32)

        bb, cbm = b_ref[...].astype(BF16), c_ref[...].astype(BF16)
        cbv = lax.dot_general(cbm, bb, NT, preferred_element_type=F32)
        first = lax.broadcasted_iota(jnp.int32, (CHUNK, LANES), 1) < HEAD_DIM
        causal = _causal_mask()
        for pp in range(PAIRS_PER_GROUP):
            sl = slice(pp * LANES, (pp + 1) * LANES)
            xs, cum = xs_ref[:, sl], cum_ref[:, sl]
            xt = xs * dt_ref[:, sl]
            xtb = xt.astype(BF16)
            hp = state[g * PAIRS_PER_GROUP + pp]
            hp_ref[0, pp] = hp
            y = jnp.exp(cum) * lax.dot_general(cbm, hp.astype(BF16), NT, preferred_element_type=F32)
            for hh in range(2):
                lm = _decay(cum[:, hh * HEAD_DIM:hh * HEAD_DIM + 1], cumt_ref[0, 2 * pp + hh], causal)
                xm = jnp.where(first if hh == 0 else jnp.logical_not(first), xtb, jnp.zeros_like(xtb))
                y = y + jnp.dot((cbv * lm).astype(BF16), xm, preferred_element_type=F32)
            decs = jnp.exp(cum[CHUNK - 1:CHUNK, :] - cum)
            st = lax.dot_general((xt * decs).astype(BF16), bb, TN, preferred_element_type=F32)
            state[g * PAIRS_PER_GROUP + pp] = _state_row_scale(cumt_ref, pp) * hp + st
            y_ref[:, sl] = y + xs * dsk_ref[:, sl]

    wide, vec, cumt_spec, hp_spec, bc = _ssd_specs(nc, False)
    return pl.pallas_call(
        body, name=name, grid=(nc, N_GROUPS),
        in_specs=[wide, bc(B_CB0), bc(C_CB0), wide, wide, cumt_spec, vec],
        out_specs=(wide, hp_spec),
        out_shape=(_sds((s, D), F32), _sds((nc, N_PAIRS, LANES, N_STATE), F32)),
        scratch_shapes=[pltpu.VMEM((N_PAIRS, LANES, N_STATE), F32)],
        compiler_params=_cp("arbitrary", "arbitrary"),
    )(xbc_c, xbc_c, xbc_c, dt_b, cum_b, cumt4, dskip_b)


def _ssd_bwd(xbc_c, dt_b, cum_b, cumt4, hprev, dy, alog_b, dskip_b, *, name):
    s = xbc_c.shape[0]
    nc = s // CHUNK

    def body(xs_ref, b_ref, c_ref, dt_ref, cum_ref, cumt_ref, hp_ref, dy_ref, alog_ref, dsk_ref,
             dxs_ref, db_ref, dc_ref, ddt_ref, dalog_ref, ddsk_ref, dstate):
        c, g = pl.program_id(0), pl.program_id(1)
        pairs = pl.ds(g * PAIRS_PER_GROUP, PAIRS_PER_GROUP)

        @pl.when(c == 0)
        def _():
            dstate[pairs] = jnp.zeros((PAIRS_PER_GROUP, LANES, N_STATE), F32)
            dalog_ref[pairs] = jnp.zeros((PAIRS_PER_GROUP, 1, LANES), F32)
            ddsk_ref[pairs] = jnp.zeros((PAIRS_PER_GROUP, 1, LANES), F32)

        bb, cbm = b_ref[...].astype(BF16), c_ref[...].astype(BF16)
        cbv = lax.dot_general(cbm, bb, NT, preferred_element_type=F32)
        first = lax.broadcasted_iota(jnp.int32, (CHUNK, LANES), 1) < HEAD_DIM
        causal = _causal_mask()
        rown = lax.broadcasted_iota(jnp.int32, (LANES, 1), 0)
        d_b = jnp.zeros((CHUNK, N_STATE), F32)
        d_c = jnp.zeros((CHUNK, N_STATE), F32)
        for pp in range(PAIRS_PER_GROUP):
            sl = slice(pp * LANES, (pp + 1) * LANES)
            idx = g * PAIRS_PER_GROUP + pp
            xs, dtb, cum, dy = xs_ref[:, sl], dt_ref[:, sl], cum_ref[:, sl], dy_ref[:, sl]
            a_b = -jnp.exp(alog_ref[:, sl])
            xt = xs * dtb
            xtb = xt.astype(BF16)
            dyb = dy.astype(BF16)
            hp = hp_ref[0, pp]
            hpb = hp.astype(BF16)
            dh = dstate[idx]
            dhb = dh.astype(BF16)
            exp_cum = jnp.exp(cum)
            decs = jnp.exp(cum[CHUNK - 1:CHUNK, :] - cum)
            row_scale = _state_row_scale(cumt_ref, pp)
            xd = (xt * decs).astype(BF16)
            dye = (dy * exp_cum).astype(BF16)
            zero_b = jnp.zeros_like(xtb)

            h_next = row_scale * hp + lax.dot_general(xd, bb, TN, preferred_element_type=F32)
            y = exp_cum * lax.dot_general(cbm, hpb, NT, preferred_element_type=F32)
            dxt = decs * lax.dot_general(bb, dhb, NT, preferred_element_type=F32)
            dcb = jnp.zeros((CHUNK, CHUNK), F32)
            for hh in range(2):
                mask = first if hh == 0 else jnp.logical_not(first)
                lm = _decay(cum[:, hh * HEAD_DIM:hh * HEAD_DIM + 1], cumt_ref[0, 2 * pp + hh], causal)
                m = (cbv * lm).astype(BF16)
                y = y + jnp.dot(m, jnp.where(mask, xtb, zero_b), preferred_element_type=F32)
                dxt = dxt + jnp.where(mask, lax.dot_general(m, dyb, TN, preferred_element_type=F32), 0.0)
                dm = lax.dot_general(jnp.where(mask, dyb, zero_b), xtb, NT, preferred_element_type=F32)
                dcb = dcb + dm * lm
            dcbb = dcb.astype(BF16)
            d_c = d_c + jnp.dot(dcbb, bb, preferred_element_type=F32) + jnp.dot(dye, hpb, preferred_element_type=F32)
            d_b = (d_b + lax.dot_general(dcbb, cbm, TN, preferred_element_type=F32)
                   + jnp.dot(xd, dhb, preferred_element_type=F32))
            dstate[idx] = row_scale * dh + lax.dot_general(dye, cbm, TN, preferred_element_type=F32)

            d_cum = _pair_sum(dyb.astype(F32) * y - dxt * xtb.astype(F32), first)
            e = jnp.sum(dh * h_next, axis=1, keepdims=True)
            t0 = jnp.sum(jnp.where(rown < HEAD_DIM, e, 0.0), axis=0, keepdims=True)
            t1 = jnp.sum(jnp.where(rown < HEAD_DIM, 0.0, e), axis=0, keepdims=True)
            d_da = _split3_dot(_tri(False), d_cum) + jnp.where(first[0:1, :], t0, t1)
            ddt_ref[:, sl] = a_b * d_da + _pair_sum(dxt * xs, first)
            dxs_ref[:, sl] = dxt * dtb + dy * dsk_ref[:, sl]
            dalog_ref[idx] += jnp.sum(d_da * dtb * a_b, axis=0, keepdims=True)
            ddsk_ref[idx] += jnp.sum(_pair_sum(dy * xs, first), axis=0, keepdims=True)
        db_ref[...] = d_b
        dc_ref[...] = d_c

    wide, vec, cumt_spec, hp_spec, bc = _ssd_specs(nc, True)
    acc = pl.BlockSpec((N_PAIRS, 1, LANES), lambda c, g: (0, 0, 0))
    return pl.pallas_call(
        body, name=name, grid=(nc, N_GROUPS),
        in_specs=[wide, bc(B_CB0), bc(C_CB0), wide, wide, cumt_spec, hp_spec, wide, vec, vec],
        out_specs=(wide, bc(0), bc(0), wide, acc, acc),
        out_shape=(_sds((s, D), F32), _sds((s, N_GROUPS * N_STATE), F32), _sds((s, N_GROUPS * N_STATE), F32),
                   _sds((s, D), F32), _sds((N_PAIRS, 1, LANES), F32), _sds((N_PAIRS, 1, LANES), F32)),
        scratch_shapes=[pltpu.VMEM((N_PAIRS, LANES, N_STATE), F32)],
        compiler_params=_cp("arbitrary", "arbitrary"),
    )(xbc_c, xbc_c, xbc_c, dt_b, cum_b, cumt4, hprev, dy, alog_b, dskip_b)


def _lane_bcast(v):
    return jnp.repeat(v, HEAD_DIM, axis=1)


DT0 = D + D_XBC
W_IN_SHARD = (DT0 + N_HEADS + 3 * D) // N_DEV
DT_PIECE = DT0 // W_IN_SHARD
DT_OFF = DT0 - DT_PIECE * W_IN_SHARD
assert DT_OFF + N_HEADS <= W_IN_SHARD


def _split_w_in(got):
    hold = got[DT_PIECE]
    cols = ([got[d] for d in range(DT_PIECE)] + [hold[:, :DT_OFF], hold[:, DT_OFF + N_HEADS:]]
            + [got[d] for d in range(DT_PIECE + 1, N_DEV)])
    return (jnp.concatenate(cols, axis=1),
            jnp.pad(hold[:, DT_OFF:DT_OFF + N_HEADS], ((0, 0), (0, DT_W - N_HEADS))))


def _w_in_pieces(dw_main, dw_dt):
    pieces = []
    for d in range(N_DEV):
        lo = d * W_IN_SHARD
        if d < DT_PIECE:
            pieces.append(dw_main[:, lo:lo + W_IN_SHARD])
        elif d == DT_PIECE:
            pieces.append(jnp.concatenate(
                [dw_main[:, lo:DT0], dw_dt[:, :N_HEADS], dw_main[:, DT0:lo + W_IN_SHARD - N_HEADS]], axis=1))
        else:
            pieces.append(dw_main[:, lo - N_HEADS:lo - N_HEADS + W_IN_SHARD])
    return jnp.stack(pieces)


def _prep_small(conv_w, conv_b, dt_bias, a_log, d_skip):
    pad = DT_W - N_HEADS
    return dict(
        conv_w=conv_w, conv_b=conv_b.reshape(1, D_XBC),
        dt_bias=jnp.pad(dt_bias.reshape(1, N_HEADS), ((0, 0), (0, pad))),
        a_log=jnp.pad(a_log.reshape(1, N_HEADS), ((0, 0), (0, pad))),
        alog_b=_lane_bcast(a_log.reshape(1, N_HEADS)), dskip_b=_lane_bcast(d_skip.reshape(1, N_HEADS)))


def _layer_fwd(h0, memn, p, g, li, comm=None):
    s = h0.shape[0]
    nc = s // CHUNK
    n = f"l{li}_"

    def hosted(what, *args, **kw):
        side = comm.ag_side(what, li) if comm is not None else None
        if side is None:
            return _mm(*args, **kw)
        out, got = _mm(*args, side=side, **kw)
        comm.after_ag(what, li, p, got)
        return out

    if comm is not None:
        comm.before_layer(li, p)
    hn = _norm_fwd(h0, g["norm_mix"], name=n + "norm_mix")
    proj = hosted("proj", hn, p["w_main"], name=n + "proj")
    dt_raw = _mm(hn, p["w_dt"], name=n + "proj_dt")
    xbc_c = _conv_silu_fwd(proj, p["conv_w"], p["conv_b"], name=n + "conv")
    dt_b, cum_b, cumt = _dt_fwd(dt_raw, p["dt_bias"], p["a_log"], name=n + "dt")
    cumt4 = cumt[:, :N_HEADS, :].reshape(nc, N_HEADS, 1, CHUNK)
    y, hprev = _ssd_fwd(xbc_c, dt_b, cum_b, cumt4, p["dskip_b"], name=n + "ssd")
    y_ssd = _gated_norm_fwd(y, proj, g["ssd_norm"], name=n + "gnorm")
    y_sc = _sc_fwd(proj, p["sc_conv_w"], g["sc_norm"], name=n + "sc")
    cat = jnp.concatenate([y_ssd, y_sc], axis=1)
    h1 = hosted("out", cat, p["w_out"], res=h0, bm=512, j_outer=True, name=n + "out")
    hx = _norm_fwd(h1, g["norm_xa"], name=n + "norm_xa")
    q = _mm(hx, p["w_q"], out_dtype=BF16, name=n + "q")
    k = _mm(memn, p["w_k"], out_dtype=BF16, name=n + "k")
    v = _mm(memn, p["w_v"], out_dtype=BF16, name=n + "v")
    o = _attn_fwd(q, k, v, name=n + "attn")
    h2 = _mm(o, p["w_o"], res=h1, name=n + "o")
    hf = _norm_fwd(h2, g["norm_ffn"], name=n + "norm_ffn")
    gu = hosted("gu", hf, p["w_gu"], out_dtype=BF16, name=n + "gu")
    act = _swiglu_fwd(gu, name=n + "swiglu")
    h3 = hosted("down", act, p["w_down"], res=h2, bm=512, j_outer=True, name=n + "down")
    saved = dict(h0=h0, hn=hn, proj=proj, dt_raw=dt_raw, xbc_c=xbc_c, dt_b=dt_b, cum_b=cum_b, cumt4=cumt4, hprev=hprev,
                 y=y, cat=cat, h1=h1, hx=hx, q=q, k=k, v=v, o=o, h2=h2, hf=hf, gu=gu, act=act)
    return h3, saved


def _layer_bwd(dh, dhb, dmemn, memn, p, g, sv, li, comm=None):
    n = f"l{li}b_"
    gr = {}

    def hosted(names, *args, **kw):
        side = comm.grad_side(li, names, gr) if comm is not None else None
        if side is None:
            return _mm(*args, **kw)
        out, got = _mm(*args, side=side, **kw)
        comm.after_grads(li, names, got)
        return out

    wide = dict(bm=512, bn=512)
    dact = _mm(dhb, p["w_down"], tb=True, bn=512, out_dtype=BF16, name=n + "dact")
    gr["w_down"] = hosted(("carry",), sv["act"], dhb, ta=True, out_dtype=BF16, name=n + "dw_down", **wide)
    dgu = _swiglu_bwd(sv["gu"], dact, name=n + "swiglu")
    dw_gu = hosted(("w_down",), sv["hf"], dgu, ta=True, out_dtype=BF16, name=n + "dw_gu", **wide)
    gr["w_gate"], gr["w_up"] = dw_gu[:, :D_FF], dw_gu[:, D_FF:]
    dhf = hosted(("w_gate",), dgu, p["w_gu"], tb=True, j_outer=True, out_dtype=BF16, name=n + "dhf", **wide)
    dh, dhb, gr["norm_ffn"] = _norm_bwd(sv["h2"], g["norm_ffn"], dhf, dh, name=n + "norm_ffn")
    do = _mm(dhb, p["w_o"], tb=True, out_dtype=BF16, name=n + "do")
    gr["w_o"] = _mm(sv["o"], dhb, ta=True, out_dtype=BF16, name=n + "dw_o", **wide)
    dq, dk, dv = _attn_bwd(sv["q"], sv["k"], sv["v"], do, name=n + "attn")
    gr["w_q"] = _mm(sv["hx"], dq, ta=True, out_dtype=BF16, name=n + "dw_q", **wide)
    gr["w_k"] = _mm(memn, dk, ta=True, out_dtype=BF16, name=n + "dw_k")
    gr["w_v"] = _mm(memn, dv, ta=True, out_dtype=BF16, name=n + "dw_v")
    dmemn = _mm(dk, p["w_k"], tb=True, res=dmemn, name=n + "dmem_k")
    dmemn = _mm(dv, p["w_v"], tb=True, res=dmemn, name=n + "dmem_v")
    dhx = _mm(dq, p["w_q"], tb=True, out_dtype=BF16, name=n + "dhx")
    dh, dhb, gr["norm_xa"] = _norm_bwd(sv["h1"], g["norm_xa"], dhx, dh, name=n + "norm_xa")
    dcat = hosted(("w_q", "w_k", "w_v", "w_o"), dhb, p["w_out"], tb=True, out_dtype=BF16, name=n + "dcat")
    gr["w_out"] = _mm(sv["cat"], dhb, ta=True, out_dtype=BF16, name=n + "dw_out", **wide)
    du, dgb, dgc, dsc_w, gr["sc_norm"] = _sc_bwd(sv["proj"], dcat, p["sc_conv_w"], g["sc_norm"], name=n + "sc")
    gr["sc_conv_w"] = dsc_w.reshape(3, D)
    dy, dz, gr["ssd_norm"] = _gated_norm_bwd(sv["y"], sv["proj"], dcat, g["ssd_norm"], name=n + "gnorm")
    dxs, d_b, d_c, ddt_b, dalog, ddsk = _ssd_bwd(sv["xbc_c"], sv["dt_b"], sv["cum_b"], sv["cumt4"], sv["hprev"], dy,
                                                 p["alog_b"], p["dskip_b"], name=n + "ssd")
    gr["a_log"] = dalog.reshape(N_HEADS, HEAD_DIM)[:, 0]
    gr["d_skip"] = ddsk.reshape(N_HEADS, HEAD_DIM)[:, 0]
    dxbc, dconv_w, dconv_b = _conv_silu_bwd(sv["proj"], dxs, d_b, d_c, p["conv_w"], p["conv_b"], name=n + "conv")
    gr["ssd_conv_w"] = dconv_w.reshape(4, D_XBC)
    gr["ssd_conv_b"] = dconv_b.reshape(D_XBC)
    ddt_raw, ddt_bias = _dt_bwd(sv["dt_raw"], p["dt_bias"], ddt_b, name=n + "dt")
    gr["dt_bias"] = ddt_bias[0, :N_HEADS]
    dproj = jnp.concatenate([dz, dxbc, du, dgb, dgc], axis=1)
    dw_main = hosted(("w_up", "w_out"), sv["hn"], dproj, ta=True, out_dtype=BF16, name=n + "dw_main", **wide)
    dw_dt = _mm(sv["hn"], ddt_raw, ta=True, out_dtype=BF16, name=n + "dw_dt", **wide)
    gr["w_in"] = [_w_in_pieces(dw_main[rows], dw_dt[rows]) for rows in (slice(0, D // 2), slice(D // 2, D))]
    dhn = hosted(("w_in",), dproj, p["w_main"], tb=True, j_outer=True, name=n + "dhn", **wide)
    dhn = _mm(ddt_raw, p["w_dt"], tb=True, res=dhn, name=n + "dhn_dt")
    dh, dhb, gr["norm_mix"] = _norm_bwd(sv["h0"], g["norm_mix"], dhn, dh, name=n + "norm_mix")
    return dh, dhb, dmemn, gr


def _local_step(x, mem, tgt, layers, gains, mem_norm, norm_final, comm=None):
    depth = len(layers)
    memn_f = _norm_fwd(mem, mem_norm, out_dtype=F32, name="mem_norm")
    memn = memn_f.astype(BF16)
    h = x
    saved = []
    for li in range(depth):
        h, sv = _layer_fwd(h, memn, layers[li], gains[li], li, comm)
        saved.append(sv)
    loss, dh, dhb, d_final = _loss_head(h, norm_final, tgt, name="loss_head")
    dmemn = jnp.zeros(mem.shape, F32)
    grads = [None] * depth
    for li in reversed(range(depth)):
        dh, dhb, dmemn, grads[li] = _layer_bwd(dh, dhb, dmemn, memn, layers[li], gains[li], saved[li], li, comm)
    _, _, d_mem_norm = _norm_bwd(mem, mem_norm, dmemn, None, name="mem_norm_b")
    return loss, dh, grads, d_mem_norm, d_final


ADAMW_ROWS = (64, 32, 16, 8)


def _adamw(parts, w, m, v, *, name):
    n_parts = len(parts)
    r, c_dim = parts[0].shape[1:]
    per_layer = n_parts // w.shape[0]
    assert w.shape == (n_parts // per_layer, per_layer * r, c_dim), (name, w.shape, parts[0].shape)
    tr = next(t for t in ADAMW_ROWS if r % t == 0)
    nb = r // tr

    def body(*refs):
        p_refs = refs[:n_parts]
        w_ref, m_ref, v_ref, g_ref, d_ref, nm_ref, nv_ref = refs[n_parts:]
        for l in range(n_parts):
            @pl.when(pl.program_id(0) == l)
            def _(l=l):
                g = p_refs[l][0].astype(F32)
                for s in range(1, N_DEV):
                    g = g + p_refs[l][s].astype(F32)
                g_ref[...] = g

        g = g_ref[...]
        m2 = ADAM_B1 * m_ref[...] + (1.0 - ADAM_B1) * g
        v2 = ADAM_B2 * v_ref[...] + (1.0 - ADAM_B2) * (g * g)
        m_hat = m2 / (1.0 - ADAM_B1 ** ADAM_STEP)
        v_hat = v2 / (1.0 - ADAM_B2 ** ADAM_STEP)
        d_ref[...] = -ADAM_LR * (m_hat / (jnp.sqrt(v_hat) + ADAM_EPS) + ADAM_WD * w_ref[...])
        nm_ref[...] = m2
        nv_ref[...] = v2

    def part_spec(l):
        return pl.BlockSpec((N_DEV, tr, c_dim),
                            lambda lay, i: (0, jnp.where(lay == l, i, jnp.where(lay < l, 0, nb - 1)), 0))

    row = pl.BlockSpec((None, tr, c_dim), lambda lay, i: (lay // per_layer, (lay % per_layer) * nb + i, 0))
    out = _sds(w.shape, F32)
    return pl.pallas_call(
        body, name=name, grid=(n_parts, nb), in_specs=[part_spec(l) for l in range(n_parts)] + [row, row, row],
        out_specs=(row, row, row, row), out_shape=(out, out, out, out), compiler_params=_cp("arbitrary", "arbitrary"),
    )(*parts, w, m, v)


BIG = ("w_in", "w_out", "w_q", "w_k", "w_v", "w_o", "w_gate", "w_up", "w_down")
COL_SHARDED = ("w_in", "w_o", "w_gate", "w_up")
PROJ_HOSTED = ("w_out", "w_q", "w_k", "w_v", "w_o", "w_gate")
SMALL_REPL = ("norm_mix", "ssd_conv_b", "dt_bias", "a_log", "d_skip", "ssd_norm", "sc_norm", "mem_norm", "norm_xa",
              "norm_ffn", "norm_final")
SMALL_SHARDED = ("ssd_conv_w", "sc_conv_w")
WEIGHTS = ("norm_mix", "w_in", "ssd_conv_w", "ssd_conv_b", "dt_bias", "a_log", "d_skip", "ssd_norm", "sc_conv_w",
           "sc_norm", "w_out", "mem_norm", "norm_xa", "w_q", "w_k", "w_v", "w_o", "norm_ffn", "w_gate", "w_up",
           "w_down", "norm_final")


def _assemble(got, col_sharded):
    l, _, r, c_dim = got.shape
    if col_sharded:
        return jnp.transpose(got, (0, 2, 1, 3)).reshape(l, r, N_DEV * c_dim)
    return got.reshape(l, N_DEV * r, c_dim)


def _to_pieces(full, col_sharded):
    rr, cc = full.shape
    if col_sharded:
        return jnp.transpose(full.reshape(rr, N_DEV, cc // N_DEV), (1, 0, 2))
    return full.reshape(N_DEV, rr // N_DEV, cc)


class _Comm:
    def __init__(self, w, first_w_in):
        self.w = w
        self.depth = w["w_in"].shape[0]
        self.next_w_in = first_w_in
        self.next_w_down = None
        self.gate = None
        self.carry = None
        self.got = {nm: [None] * self.depth for nm in BIG}
        self.got["w_in"] = [None] * (2 * self.depth)

    def _wanted(self, what, li):
        more = li + 1 < self.depth
        if what == "proj":
            return [(nm, li) for nm in PROJ_HOSTED]
        if what == "out":
            return [("w_up", li)]
        if what == "gu":
            return ([("w_in", li + 1)] if more else []) + ([("w_down", 0)] if li == 0 else [])
        return [("w_down", li + 1)] if more else []

    def before_layer(self, li, p):
        p["w_main"], p["w_dt"] = _split_w_in(self.next_w_in)
        if li > 0:
            p["w_down"] = self.next_w_down

    def ag_side(self, what, li):
        wanted = self._wanted(what, li)
        return _ag_side([self.w[nm][l:l + 1].astype(BF16) for nm, l in wanted]) if wanted else None

    def after_ag(self, what, li, p, got):
        for (nm, l), g in zip(self._wanted(what, li), got):
            if nm == "w_in":
                self.next_w_in = g[0]
                continue
            full = _assemble(g, nm in COL_SHARDED)[0]
            if nm == "w_gate":
                self.gate = full
            elif nm == "w_up":
                p["w_gu"] = jnp.concatenate([self.gate, full], axis=1)
            elif nm == "w_down" and l > li:
                self.next_w_down = full
            else:
                p[nm] = full

    def _grad_jobs(self, li, names):
        jobs = []
        for nm in names:
            if nm == "w_in":
                jobs += [("w_in", 2 * li)] + ([("w_in", 1)] if li == 0 else [])
            elif nm == "carry":
                jobs += [("w_in", 2 * li + 3)] if li + 1 < self.depth else []
            else:
                jobs.append((nm, li))
        return jobs

    def grad_side(self, li, names, gr):
        pieces = []
        for nm, slot in self._grad_jobs(li, names):
            if nm != "w_in":
                pieces.append(_to_pieces(gr[nm], nm in COL_SHARDED))
            elif slot == 2 * li + 3:
                pieces.append(self.carry)
            else:
                pieces.append(gr["w_in"][slot - 2 * li])
        if "w_in" in names and li > 0:
            self.carry = gr["w_in"][1]
        return _rs_side(pieces) if pieces else None

    def after_grads(self, li, names, got):
        for (nm, slot), g in zip(self._grad_jobs(li, names), got):
            self.got[nm][slot] = g


def _pack(arrs, names):
    flat = jnp.concatenate([arrs[nm].reshape(-1).astype(F32) for nm in names])
    rows = -(-flat.shape[0] // (TB * LANES)) * TB
    return jnp.pad(flat, (0, rows * LANES - flat.shape[0])).reshape(rows, LANES)


def _unpack(packed, shapes, names):
    flat = packed.reshape(-1)
    out, off = {}, 0
    for nm in names:
        size = 1
        for dim in shapes[nm]:
            size *= dim
        out[nm] = flat[off:off + size].reshape(shapes[nm])
        off += size
    return out


def kernel(x, mem, norm_mix, w_in, ssd_conv_w, ssd_conv_b, dt_bias, a_log, d_skip, ssd_norm, sc_conv_w, sc_norm, w_out, mem_norm, norm_xa, w_q, w_k, w_v, w_o, norm_ffn, w_gate, w_up, w_down, norm_final, loss_target, m_norm_mix, m_w_in, m_ssd_conv_w, m_ssd_conv_b, m_dt_bias, m_a_log, m_d_skip, m_ssd_norm, m_sc_conv_w, m_sc_norm, m_w_out, m_mem_norm, m_norm_xa, m_w_q, m_w_k, m_w_v, m_w_o, m_norm_ffn, m_w_gate, m_w_up, m_w_down, m_norm_final, v_norm_mix, v_w_in, v_ssd_conv_w, v_ssd_conv_b, v_dt_bias, v_a_log, v_d_skip, v_ssd_norm, v_sc_conv_w, v_sc_norm, v_w_out, v_mem_norm, v_norm_xa, v_w_q, v_w_k, v_w_v, v_w_o, v_norm_ffn, v_w_gate, v_w_up, v_w_down, v_norm_final):
    w = dict(norm_mix=norm_mix, w_in=w_in, ssd_conv_w=ssd_conv_w, ssd_conv_b=ssd_conv_b, dt_bias=dt_bias, a_log=a_log,
             d_skip=d_skip, ssd_norm=ssd_norm, sc_conv_w=sc_conv_w, sc_norm=sc_norm, w_out=w_out, mem_norm=mem_norm,
             norm_xa=norm_xa, w_q=w_q, w_k=w_k, w_v=w_v, w_o=w_o, norm_ffn=norm_ffn, w_gate=w_gate, w_up=w_up,
             w_down=w_down, norm_final=norm_final)
    mom = dict(norm_mix=m_norm_mix, w_in=m_w_in, ssd_conv_w=m_ssd_conv_w, ssd_conv_b=m_ssd_conv_b, dt_bias=m_dt_bias,
               a_log=m_a_log, d_skip=m_d_skip, ssd_norm=m_ssd_norm, sc_conv_w=m_sc_conv_w, sc_norm=m_sc_norm,
               w_out=m_w_out, mem_norm=m_mem_norm, norm_xa=m_norm_xa, w_q=m_w_q, w_k=m_w_k, w_v=m_w_v, w_o=m_w_o,
               norm_ffn=m_norm_ffn, w_gate=m_w_gate, w_up=m_w_up, w_down=m_w_down, norm_final=m_norm_final)
    var = dict(norm_mix=v_norm_mix, w_in=v_w_in, ssd_conv_w=v_ssd_conv_w, ssd_conv_b=v_ssd_conv_b, dt_bias=v_dt_bias,
               a_log=v_a_log, d_skip=v_d_skip, ssd_norm=v_ssd_norm, sc_conv_w=v_sc_conv_w, sc_norm=v_sc_norm,
               w_out=v_w_out, mem_norm=v_mem_norm, norm_xa=v_norm_xa, w_q=v_w_q, w_k=v_w_k, w_v=v_w_v, w_o=v_w_o,
               norm_ffn=v_norm_ffn, w_gate=v_w_gate, w_up=v_w_up, w_down=v_w_down, norm_final=v_norm_final)
    depth = w_in.shape[0]
    my = 4 * lax.axis_index("x") + 2 * lax.axis_index("y") + lax.axis_index("c")

    got = _run_side(_ag_side([w_in[0:1].astype(BF16), ssd_conv_w, sc_conv_w]), name="ag_first")
    comm = _Comm(w, got[0][0])
    conv_full = {"ssd_conv_w": _assemble(got[1], True), "sc_conv_w": _assemble(got[2], True)}
    layers, gains = [], []
    for li in range(depth):
        p = _prep_small(conv_full["ssd_conv_w"][li], ssd_conv_b[li], dt_bias[li], a_log[li], d_skip[li])
        p["sc_conv_w"] = conv_full["sc_conv_w"][li]
        layers.append(p)
        gains.append({nm: w[nm][li].reshape(1, D) for nm in ("norm_mix", "ssd_norm", "sc_norm", "norm_xa", "norm_ffn")})

    loss_v, grad_x, grads, d_mem_norm, d_final = _local_step(
        x[0], mem[0], loss_target[0], layers, gains, mem_norm.reshape(1, D), norm_final.reshape(1, D), comm)
    loss = lax.psum(loss_v[0, 0], ("x", "y", "c"))

    outs = {}
    for nm in BIG:
        outs[nm] = _adamw(comm.got[nm], w[nm], mom[nm], var[nm], name="adamw_" + nm)

    small = SMALL_REPL + SMALL_SHARDED
    gsmall = dict(mem_norm=d_mem_norm.reshape(D), norm_final=d_final.reshape(D))
    for nm in ("norm_mix", "ssd_norm", "sc_norm", "norm_xa", "norm_ffn"):
        gsmall[nm] = jnp.stack([gr[nm].reshape(D) for gr in grads])
    for nm in ("ssd_conv_b", "dt_bias", "a_log", "d_skip", "ssd_conv_w", "sc_conv_w"):
        gsmall[nm] = jnp.stack([gr[nm] for gr in grads])
    packed_g = _pack(gsmall, small)
    all_g = _run_side(_ag_side([packed_g[None]]), name="ag_small_grads")[0][0]

    def put_shard(arrs):
        loc = {nm: arrs[nm] for nm in SMALL_REPL}
        for nm in SMALL_SHARDED:
            cs = arrs[nm].shape[-1]
            loc[nm] = lax.dynamic_update_slice_in_dim(jnp.zeros(gsmall[nm].shape, F32), arrs[nm], my * cs, axis=2)
        return _pack(loc, small)

    res = _adamw([all_g], put_shard(w)[None], put_shard(mom)[None], put_shard(var)[None], name="adamw_small")
    shapes = {nm: gsmall[nm].shape for nm in small}
    for idx in range(4):
        un = _unpack(res[idx], shapes, small)
        for nm in SMALL_REPL:
            outs.setdefault(nm, [None] * 4)[idx] = un[nm]
        for nm in SMALL_SHARDED:
            cs = w[nm].shape[-1]
            outs.setdefault(nm, [None] * 4)[idx] = lax.dynamic_slice_in_dim(un[nm], my * cs, cs, axis=2)

    return (loss, grad_x[None], *[outs[nm][0] for nm in WEIGHTS], *[outs[nm][1] for nm in WEIGHTS],
            *[outs[nm][2] for nm in WEIGHTS], *[outs[nm][3] for nm in WEIGHTS])
```

```python
import functools

import jax
import jax.numpy as jnp
from jax import lax
from jax.experimental import pallas as pl
from jax.experimental.pallas import tpu as pltpu

F32 = jnp.float32
BF16 = jnp.bfloat16

D = 2048
HEAD_DIM = 64
N_HEADS = 32
N_GROUPS = 4
N_STATE = 128
CHUNK = 256
D_XBC = D + 2 * N_GROUPS * N_STATE
SSD_GROUP_W = D // N_GROUPS
SC_GROUP_W = 128
XA_HEADS = 4
XA_HD = 128
D_XA = XA_HEADS * XA_HD
D_FF = 5632
NORM_EPS = 1e-5
PROJ_W = 11264
DT_W = 128
N_DEV = 8

ADAM_LR = 0.001
ADAM_B1 = 0.9
ADAM_B2 = 0.999
ADAM_EPS = 1e-08
ADAM_WD = 0.01
ADAM_STEP = 10

TB = CHUNK
HALO = 8
LANES = 128
VMEM_LIMIT = 56 * 1024 * 1024

NT = (((1,), (1,)), ((), ()))
TN = (((0,), (0,)), ((), ()))
MESH = pl.DeviceIdType.MESH


def _cp(*sem):
    return pltpu.CompilerParams(dimension_semantics=sem, vmem_limit_bytes=VMEM_LIMIT)


def _sds(shape, dtype):
    return jax.ShapeDtypeStruct(shape, dtype)


def _sigmoid(x):
    return pl.reciprocal(1.0 + jnp.exp(-x), approx=True)


def _split3_dot(t_bf16, x):
    hi = x.astype(BF16)
    r1 = x - hi.astype(F32)
    mid = r1.astype(BF16)
    lo = (r1 - mid.astype(F32)).astype(BF16)
    out = jnp.dot(t_bf16, hi, preferred_element_type=F32)
    out = out + jnp.dot(t_bf16, mid, preferred_element_type=F32)
    return out + jnp.dot(t_bf16, lo, preferred_element_type=F32)


def _split3_dot_r(x, t_bf16):
    hi = x.astype(BF16)
    r1 = x - hi.astype(F32)
    mid = r1.astype(BF16)
    lo = (r1 - mid.astype(F32)).astype(BF16)
    out = jnp.dot(hi, t_bf16, preferred_element_type=F32)
    out = out + jnp.dot(mid, t_bf16, preferred_element_type=F32)
    return out + jnp.dot(lo, t_bf16, preferred_element_type=F32)


def _group_bcast(stat_fn, v, gw):
    pieces = []
    for g in range(v.shape[1] // gw):
        vs = v[:, g * gw:(g + 1) * gw]
        pieces.append(jnp.broadcast_to(stat_fn(vs), vs.shape))
    return jnp.concatenate(pieces, axis=1) if len(pieces) > 1 else pieces[0]


def _group_rstd(v, gw):
    return _group_bcast(lambda s: lax.rsqrt(jnp.mean(s * s, axis=1, keepdims=True) + NORM_EPS), v, gw)


def _group_mean(v, gw):
    return _group_bcast(lambda s: jnp.mean(s, axis=1, keepdims=True), v, gw)


def _pair_sum(d, first):
    s0 = jnp.sum(jnp.where(first, d, 0.0), axis=1, keepdims=True)
    s1 = jnp.sum(jnp.where(first, 0.0, d), axis=1, keepdims=True)
    return jnp.where(first, s0, s1)


class _Side:
    def __init__(self, arrays, out_shapes, n_remote, n_local, start, finish):
        self.arrays, self.out_shapes = tuple(arrays), tuple(out_shapes)
        self.n_remote, self.n_local = n_remote, n_local
        self.start, self.finish = start, finish

    def scratch(self):
        return [pltpu.SemaphoreType.DMA((self.n_remote,)), pltpu.SemaphoreType.DMA((self.n_remote,)),
                pltpu.SemaphoreType.DMA((self.n_local,))]


def _ag_side(shards):
    n = len(shards)

    def plan(x_refs, out_refs, send, recv, local, starting=False):
        x, y, c = lax.axis_index("x"), lax.axis_index("y"), lax.axis_index("c")
        me, sibling = (x, y, c), (x, y, 1 - c)
        chips = [(1 - x, y), (x, 1 - y), (1 - x, 1 - y)]
        jobs = []
        for a in range(n):
            def rows(px, py, pc, out=out_refs[a]):
                return out.at[:, 4 * px + 2 * py + pc]

            def copy(k, block, to, src=None, a=a, rows=rows):
                return pltpu.make_async_remote_copy(
                    src_ref=rows(*block) if src is None else src, dst_ref=rows(*block),
                    send_sem=send.at[7 * a + k], recv_sem=recv.at[7 * a + k], device_id=to, device_id_type=MESH)

            mine = pltpu.make_async_copy(x_refs[a], rows(*me), local.at[a])
            first = [copy(0, me, sibling, src=x_refs[a])]
            first += [copy(1 + j, me, (*chip, c), src=x_refs[a]) for j, chip in enumerate(chips)]
            if starting:
                jobs.append((mine, first))
                continue
            passed = [copy(4 + j, (*chip, c), sibling) for j, chip in enumerate(chips)]
            arrive = [copy(1 + j, (*chip, c), me) for j, chip in enumerate(chips)]
            late = [copy(0, sibling, me)] + [copy(4 + j, (*chip, 1 - c), me) for j, chip in enumerate(chips)]
            jobs.append((mine, first, passed, arrive, late))
        return jobs

    def start(*refs):
        for mine, first in plan(*refs, starting=True):
            mine.start()
            for cp in first:
                cp.start()

    def finish(*refs):
        jobs = plan(*refs)
        for j in range(3):
            for _, _, passed, arrive, _ in jobs:
                arrive[j].wait_recv()
                passed[j].start()
        for mine, first, passed, _, late in jobs:
            for cp in late:
                cp.wait_recv()
            for cp in first + passed:
                cp.wait_send()
            mine.wait()

    outs = [_sds((s.shape[0], N_DEV) + s.shape[1:], s.dtype) for s in shards]
    return _Side(shards, outs, 7 * n, n, start, finish)


def _flip(k, x, y, c):
    return (1 - x if k & 4 else x, 1 - y if k & 2 else y, 1 - c if k & 1 else c)


def _rs_side(pieces):
    n = len(pieces)

    def plan(g_refs, out_refs, send, recv, local):
        x, y, c = lax.axis_index("x"), lax.axis_index("y"), lax.axis_index("c")
        me = 4 * x + 2 * y + c
        jobs = []
        for a in range(n):
            mine = pltpu.make_async_copy(g_refs[a].at[me], out_refs[a].at[me], local.at[a])
            copies = []
            for k in range(1, N_DEV):
                px, py, pc = _flip(k, x, y, c)
                copies.append(pltpu.make_async_remote_copy(
                    src_ref=g_refs[a].at[4 * px + 2 * py + pc], dst_ref=out_refs[a].at[me],
                    send_sem=send.at[7 * a + k - 1], recv_sem=recv.at[7 * a + k - 1],
                    device_id=(px, py, pc), device_id_type=MESH))
            jobs.append((mine, copies))
        return jobs

    def start(*refs):
        for mine, copies in plan(*refs):
            mine.start()
            for cp in copies:
                cp.start()

    def finish(*refs):
        for mine, copies in plan(*refs):
            for cp in copies:
                cp.wait_recv()
            for cp in copies:
                cp.wait_send()
            mine.wait()

    return _Side(pieces, [_sds(p.shape, p.dtype) for p in pieces], 7 * n, n, start, finish)


def _run_side(side, *, name):
    n_in, n_out = len(side.arrays), len(side.out_shapes)

    def body(*refs):
        parts = (refs[:n_in], refs[n_in:n_in + n_out]) + tuple(refs[n_in + n_out:])
        side.start(*parts)
        side.finish(*parts)

    hbm = pl.BlockSpec(memory_space=pl.ANY)
    return pl.pallas_call(
        body, name=name, out_shape=side.out_shapes, in_specs=[hbm] * n_in, out_specs=tuple([hbm] * n_out),
        scratch_shapes=side.scratch(),
    )(*side.arrays)


def _mm(a, b, *, name, ta=False, tb=False, res=None, out_dtype=F32, bm=1024, bn=1024, bk=None, j_outer=False,
        side=None):
    if ta:
        k_dim, m_dim = a.shape
    else:
        m_dim, k_dim = a.shape
    if tb:
        n_dim, kb = b.shape
    else:
        kb, n_dim = b.shape
    assert k_dim == kb, (a.shape, b.shape)
    bm, bn = min(bm, m_dim), min(bn, n_dim)
    bk = k_dim if bk is None else min(bk, k_dim)
    assert m_dim % bm == 0 and n_dim % bn == 0 and k_dim % bk == 0, (name, a.shape, b.shape, bm, bn, bk)
    ni, nj, nk = m_dim // bm, n_dim // bn, k_dim // bk
    grid = (nj, ni, nk) if j_outer else (ni, nj, nk)
    dn = (((0,) if ta else (1,), (1,) if tb else (0,)), ((), ()))
    has_res = res is not None
    n_main_in = 2 + has_res
    n_side_in = len(side.arrays) if side else 0
    n_side_out = len(side.out_shapes) if side else 0

    def body(*refs):
        a_ref, b_ref = refs[0], refs[1]
        r_ref = refs[2] if has_res else None
        o_ref = refs[n_main_in + n_side_in]
        scratch = refs[n_main_in + n_side_in + 1 + n_side_out:]
        if side:
            side_refs = (refs[n_main_in:n_main_in + n_side_in],
                         refs[n_main_in + n_side_in + 1:n_main_in + n_side_in + 1 + n_side_out]) + tuple(scratch[-3:])
            step = (pl.program_id(0) * grid[1] + pl.program_id(1)) * grid[2] + pl.program_id(2)

            @pl.when(step == 0)
            def _():
                side.start(*side_refs)

        p = lax.dot_general(a_ref[...].astype(BF16), b_ref[...].astype(BF16), dn, preferred_element_type=F32)
        if nk == 1:
            if has_res:
                p = p + r_ref[...]
            o_ref[...] = p.astype(o_ref.dtype)
        else:
            acc_ref = scratch[0]
            k = pl.program_id(2)

            @pl.when(k == 0)
            def _():
                acc_ref[...] = p

            @pl.when(k > 0)
            def _():
                acc_ref[...] += p

            @pl.when(k == nk - 1)
            def _():
                r = acc_ref[...]
                if has_res:
                    r = r + r_ref[...]
                o_ref[...] = r.astype(o_ref.dtype)

        if side:
            @pl.when(step == grid[0] * grid[1] * grid[2] - 1)
            def _():
                side.finish(*side_refs)

    def spec(shape, index):
        if j_outer:
            return pl.BlockSpec(shape, lambda j, i, k: index(i, j, k))
        return pl.BlockSpec(shape, index)

    a_spec = spec((bk, bm), lambda i, j, k: (k, i)) if ta else spec((bm, bk), lambda i, j, k: (i, k))
    b_spec = spec((bn, bk), lambda i, j, k: (j, k)) if tb else spec((bk, bn), lambda i, j, k: (k, j))
    o_spec = spec((bm, bn), lambda i, j, k: (i, j))
    hbm = pl.BlockSpec(memory_space=pl.ANY)
    in_specs = [a_spec, b_spec] + ([o_spec] if has_res else []) + [hbm] * n_side_in
    args = (a, b) + ((res,) if has_res else ()) + (side.arrays if side else ())
    scratch_shapes = ([pltpu.VMEM((bm, bn), F32)] if nk > 1 else []) + (side.scratch() if side else [])
    out_main = _sds((m_dim, n_dim), out_dtype)
    if not side:
        return pl.pallas_call(
            body, name=name, grid=grid, in_specs=in_specs, out_specs=o_spec, out_shape=out_main,
            scratch_shapes=scratch_shapes, compiler_params=_cp("parallel", "parallel", "arbitrary"),
        )(*args)
    outs = pl.pallas_call(
        body, name=name, grid=grid, in_specs=in_specs, out_specs=(o_spec,) + tuple([hbm] * n_side_out),
        out_shape=(out_main,) + side.out_shapes, scratch_shapes=scratch_shapes,
        compiler_params=_cp("arbitrary", "arbitrary", "arbitrary"),
    )(*args)
    return outs[0], tuple(outs[1:])


def _norm_fwd(h, g, *, name, out_dtype=BF16):
    s, d = h.shape
    tb = min(TB, s)

    def body(h_ref, g_ref, o_ref):
        x = h_ref[...]
        r = lax.rsqrt(jnp.mean(x * x, axis=-1, keepdims=True) + NORM_EPS)
        o_ref[...] = (x * r * g_ref[...]).astype(o_ref.dtype)

    row = pl.BlockSpec((tb, d), lambda i: (i, 0))
    return pl.pallas_call(
        body, name=name, grid=(s // tb,), in_specs=[row, pl.BlockSpec((1, d), lambda i: (0, 0))], out_specs=row,
        out_shape=_sds((s, d), out_dtype), compiler_params=_cp("parallel"),
    )(h, g)


def _norm_bwd(h, g, dhn, dres, *, name):
    s, d = h.shape
    tb = min(TB, s)
    has_res = dres is not None

    def body(*refs):
        h_ref, g_ref, dhn_ref = refs[:3]
        r_ref = refs[3] if has_res else None
        dh_ref, dhb_ref, dg_ref = refs[3 + has_res:]
        x = h_ref[...]
        r = lax.rsqrt(jnp.mean(x * x, axis=-1, keepdims=True) + NORM_EPS)
        xhat = x * r
        dy = dhn_ref[...].astype(F32)
        gy = dy * g_ref[...]
        dx = r * (gy - xhat * jnp.mean(gy * xhat, axis=-1, keepdims=True))
        if has_res:
            dx = dx + r_ref[...]
        dh_ref[...] = dx
        dhb_ref[...] = dx.astype(BF16)
        part = jnp.sum(dy * xhat, axis=0, keepdims=True)

        @pl.when(pl.program_id(0) == 0)
        def _():
            dg_ref[...] = part

        @pl.when(pl.program_id(0) > 0)
        def _():
            dg_ref[...] += part

    row = pl.BlockSpec((tb, d), lambda i: (i, 0))
    vec = pl.BlockSpec((1, d), lambda i: (0, 0))
    return pl.pallas_call(
        body, name=name, grid=(s // tb,), in_specs=[row, vec, row] + ([row] if has_res else []),
        out_specs=(row, row, vec), out_shape=(_sds((s, d), F32), _sds((s, d), BF16), _sds((1, d), F32)),
        compiler_params=_cp("arbitrary"),
    )(*((h, g, dhn) + ((dres,) if has_res else ())))


def _loss_head(h, g, tgt, *, name):
    s, d = h.shape
    tb = min(TB, s)

    def body(h_ref, g_ref, t_ref, loss_ref, dh_ref, dhb_ref, dg_ref):
        x = h_ref[...]
        r = lax.rsqrt(jnp.mean(x * x, axis=-1, keepdims=True) + NORM_EPS)
        xhat = x * r
        gain = g_ref[...]
        err = xhat * gain - t_ref[...]
        part_loss = 0.5 * jnp.sum(jnp.mean(err * err, axis=-1, keepdims=True), axis=0, keepdims=True)
        dy = err * (1.0 / d)
        gy = dy * gain
        dx = r * (gy - xhat * jnp.mean(gy * xhat, axis=-1, keepdims=True))
        dh_ref[...] = dx
        dhb_ref[...] = dx.astype(BF16)
        part = jnp.sum(dy * xhat, axis=0, keepdims=True)
        lossv = jnp.broadcast_to(part_loss, (1, LANES))

        @pl.when(pl.program_id(0) == 0)
        def _():
            dg_ref[...] = part
            loss_ref[...] = lossv

        @pl.when(pl.program_id(0) > 0)
        def _():
            dg_ref[...] += part
            loss_ref[...] += lossv

    row = pl.BlockSpec((tb, d), lambda i: (i, 0))
    vec = pl.BlockSpec((1, d), lambda i: (0, 0))
    return pl.pallas_call(
        body, name=name, grid=(s // tb,), in_specs=[row, vec, row],
        out_specs=(pl.BlockSpec((1, LANES), lambda i: (0, 0)), row, row, vec),
        out_shape=(_sds((1, LANES), F32), _sds((s, d), F32), _sds((s, d), BF16), _sds((1, d), F32)),
        compiler_params=_cp("arbitrary"),
    )(h, g, tgt)


def _swiglu_fwd(gu, *, name):
    s = gu.shape[0]
    tb = min(TB, s)

    def body(g_ref, u_ref, o_ref):
        g = g_ref[...].astype(F32)
        o_ref[...] = (g * _sigmoid(g) * u_ref[...].astype(F32)).astype(o_ref.dtype)

    return pl.pallas_call(
        body, name=name, grid=(s // tb,),
        in_specs=[pl.BlockSpec((tb, D_FF), lambda i: (i, 0)), pl.BlockSpec((tb, D_FF), lambda i: (i, 1))],
        out_specs=pl.BlockSpec((tb, D_FF), lambda i: (i, 0)), out_shape=_sds((s, D_FF), BF16),
        compiler_params=_cp("parallel"),
    )(gu, gu)


def _swiglu_bwd(gu, dact, *, name):
    s = gu.shape[0]
    tb = min(TB, s)

    def body(g_ref, u_ref, d_ref, o_ref):
        g = g_ref[...].astype(F32)
        sg = _sigmoid(g)
        da = d_ref[...].astype(F32)
        o_ref[:, :D_FF] = (da * u_ref[...].astype(F32) * sg * (1.0 + g * (1.0 - sg))).astype(o_ref.dtype)
        o_ref[:, D_FF:] = (da * g * sg).astype(o_ref.dtype)

    return pl.pallas_call(
        body, name=name, grid=(s // tb,),
        in_specs=[pl.BlockSpec((tb, D_FF), lambda i: (i, 0)), pl.BlockSpec((tb, D_FF), lambda i: (i, 1)),
                  pl.BlockSpec((tb, D_FF), lambda i: (i, 0))],
        out_specs=pl.BlockSpec((tb, 2 * D_FF), lambda i: (i, 0)), out_shape=_sds((s, 2 * D_FF), BF16),
        compiler_params=_cp("parallel"),
    )(gu, gu, dact)


def _softmax_rows(qh, kh):
    sc = lax.dot_general(qh, kh, NT, preferred_element_type=F32) * (XA_HD ** -0.5)
    sc = sc - jnp.max(sc, axis=-1, keepdims=True)
    e = jnp.exp(sc)
    return e / jnp.sum(e, axis=-1, keepdims=True)


def _attn_fwd(q, k, v, *, name):
    s = q.shape[0]
    n_mem = k.shape[0]
    tq = min(512, s)

    def body(q_ref, k_ref, v_ref, o_ref):
        outs = []
        for h in range(XA_HEADS):
            sl = slice(h * XA_HD, (h + 1) * XA_HD)
            p = _softmax_rows(q_ref[:, sl], k_ref[:, sl])
            outs.append(jnp.dot(p.astype(BF16), v_ref[:, sl], preferred_element_type=F32))
        o_ref[...] = jnp.concatenate(outs, axis=1).astype(o_ref.dtype)

    row = pl.BlockSpec((tq, D_XA), lambda i: (i, 0))
    kv = pl.BlockSpec((n_mem, D_XA), lambda i: (0, 0))
    return pl.pallas_call(
        body, name=name, grid=(s // tq,), in_specs=[row, kv, kv], out_specs=row, out_shape=_sds((s, D_XA), BF16),
        compiler_params=_cp("parallel"),
    )(q, k, v)


def _attn_bwd(q, k, v, do, *, name):
    s = q.shape[0]
    n_mem = k.shape[0]
    tq = min(512, s)

    def body(q_ref, k_ref, v_ref, do_ref, dq_ref, dk_ref, dv_ref):
        dqs, dks, dvs = [], [], []
        for h in range(XA_HEADS):
            sl = slice(h * XA_HD, (h + 1) * XA_HD)
            qh, kh, vh, doh = q_ref[:, sl], k_ref[:, sl], v_ref[:, sl], do_ref[:, sl]
            p = _softmax_rows(qh, kh)
            dvs.append(lax.dot_general(p.astype(BF16), doh, TN, preferred_element_type=F32))
            dp = lax.dot_general(doh, vh, NT, preferred_element_type=F32)
            ds = (p * (dp - jnp.sum(dp * p, axis=-1, keepdims=True)) * (XA_HD ** -0.5)).astype(BF16)
            dqs.append(jnp.dot(ds, kh, preferred_element_type=F32))
            dks.append(lax.dot_general(ds, qh, TN, preferred_element_type=F32))
        dq_ref[...] = jnp.concatenate(dqs, axis=1).astype(dq_ref.dtype)
        dk = jnp.concatenate(dks, axis=1)
        dv = jnp.concatenate(dvs, axis=1)

        @pl.when(pl.program_id(0) == 0)
        def _():
            dk_ref[...] = dk
            dv_ref[...] = dv

        @pl.when(pl.program_id(0) > 0)
        def _():
            dk_ref[...] += dk
            dv_ref[...] += dv

    row = pl.BlockSpec((tq, D_XA), lambda i: (i, 0))
    kv = pl.BlockSpec((n_mem, D_XA), lambda i: (0, 0))
    return pl.pallas_call(
        body, name=name, grid=(s // tq,), in_specs=[row, kv, kv, row], out_specs=(row, kv, kv),
        out_shape=(_sds((s, D_XA), BF16), _sds((n_mem, D_XA), F32), _sds((n_mem, D_XA), F32)),
        compiler_params=_cp("arbitrary"),
    )(q, k, v, do)


CONV_CB = 1024
XBC_CB0 = D // CONV_CB


def _prev_rows(i, tb):
    return jnp.maximum(i * (tb // HALO) - 1, 0)


def _next_rows(i, tb, s):
    return jnp.minimum((i + 1) * (tb // HALO), s // HALO - 1)


def _taps(xcat, width):
    return [pltpu.roll(xcat, width - 1 - k, 0) for k in range(width - 1)] + [xcat]


def _conv_taps(taps, w_ref, lo):
    acc = w_ref[0:1, :] * taps[0][lo:]
    for k in range(1, len(taps)):
        acc = acc + w_ref[k:k + 1, :] * taps[k][lo:]
    return acc


def _conv_silu_fwd(proj, w, b, *, name):
    s = proj.shape[0]
    tb = min(TB, s)

    def body(x_ref, xp_ref, w_ref, b_ref, o_ref):
        i = pl.program_id(0)
        prev = jnp.where(i > 0, xp_ref[...], 0.0)
        pre = _conv_taps(_taps(jnp.concatenate([prev, x_ref[...]], axis=0), 4), w_ref, HALO) + b_ref[...]
        o_ref[...] = pre * _sigmoid(pre)

    return pl.pallas_call(
        body, name=name, grid=(s // tb, D_XBC // CONV_CB),
        in_specs=[pl.BlockSpec((tb, CONV_CB), lambda i, j: (i, XBC_CB0 + j)),
                  pl.BlockSpec((HALO, CONV_CB), lambda i, j: (_prev_rows(i, tb), XBC_CB0 + j)),
                  pl.BlockSpec((4, CONV_CB), lambda i, j: (0, j)),
                  pl.BlockSpec((1, CONV_CB), lambda i, j: (0, j))],
        out_specs=pl.BlockSpec((tb, CONV_CB), lambda i, j: (i, j)), out_shape=_sds((s, D_XBC), F32),
        compiler_params=_cp("parallel", "parallel"),
    )(proj, proj, w, b)


def _conv_silu_bwd(proj, dxs, d_b, d_c, w, b, *, name):
    s = proj.shape[0]
    tb = min(TB, s)
    n_i = s // tb
    n_xs = D // CONV_CB
    bc_w = N_GROUPS * N_STATE

    def body(x_ref, xp_ref, xn_ref, dxs_ref, dxsn_ref, dbm_ref, dbmn_ref, dcm_ref, dcmn_ref, w_ref, b_ref,
             dx_ref, dw_ref, db_ref):
        j, i = pl.program_id(0), pl.program_id(1)
        prev = jnp.where(i > 0, xp_ref[...], 0.0)
        taps = _taps(jnp.concatenate([prev, x_ref[...], xn_ref[...]], axis=0), 4)
        pre = _conv_taps(taps, w_ref, HALO) + b_ref[...]
        sg = _sigmoid(pre)
        dy = jnp.where(j < n_xs, dxs_ref[...], jnp.concatenate([dbm_ref[...], dcm_ref[...]], axis=1))
        dyn = jnp.where(j < n_xs, dxsn_ref[...], jnp.concatenate([dbmn_ref[...], dcmn_ref[...]], axis=1))
        dy_ext = jnp.concatenate([dy, jnp.where(i < n_i - 1, dyn, 0.0)], axis=0)
        dpre = dy_ext * sg * (1.0 + pre * (1.0 - sg))
        n2 = tb + HALO
        dx = w_ref[3:4, :] * dpre[:tb]
        for k in range(3):
            dx = dx + w_ref[k:k + 1, :] * pltpu.roll(dpre, n2 - (3 - k), 0)[:tb]
        dx_ref[...] = dx.astype(dx_ref.dtype)
        dpc = dpre[:tb]
        parts = [jnp.sum(dpc * taps[k][HALO:HALO + tb], axis=0, keepdims=True) for k in range(4)]
        dbp = jnp.sum(dpc, axis=0, keepdims=True)

        @pl.when(i == 0)
        def _():
            for k in range(4):
                dw_ref[k] = parts[k]
            db_ref[...] = dbp

        @pl.when(i > 0)
        def _():
            for k in range(4):
                dw_ref[k] += parts[k]
            db_ref[...] += dbp

    return pl.pallas_call(
        body, name=name, grid=(D_XBC // CONV_CB, n_i),
        in_specs=[pl.BlockSpec((tb, CONV_CB), lambda j, i: (i, XBC_CB0 + j)),
                  pl.BlockSpec((HALO, CONV_CB), lambda j, i: (_prev_rows(i, tb), XBC_CB0 + j)),
                  pl.BlockSpec((HALO, CONV_CB), lambda j, i: (_next_rows(i, tb, s), XBC_CB0 + j)),
                  pl.BlockSpec((tb, CONV_CB), lambda j, i: (jnp.where(j < n_xs, i, 0), jnp.minimum(j, n_xs - 1))),
                  pl.BlockSpec((HALO, CONV_CB),
                               lambda j, i: (jnp.where(j < n_xs, _next_rows(i, tb, s), 0), jnp.minimum(j, n_xs - 1))),
                  pl.BlockSpec((tb, bc_w), lambda j, i: (jnp.where(j < n_xs, 0, i), 0)),
                  pl.BlockSpec((HALO, bc_w), lambda j, i: (jnp.where(j < n_xs, 0, _next_rows(i, tb, s)), 0)),
                  pl.BlockSpec((tb, bc_w), lambda j, i: (jnp.where(j < n_xs, 0, i), 0)),
                  pl.BlockSpec((HALO, bc_w), lambda j, i: (jnp.where(j < n_xs, 0, _next_rows(i, tb, s)), 0)),
                  pl.BlockSpec((4, CONV_CB), lambda j, i: (0, j)),
                  pl.BlockSpec((1, CONV_CB), lambda j, i: (0, j))],
        out_specs=(pl.BlockSpec((tb, CONV_CB), lambda j, i: (i, j)),
                   pl.BlockSpec((4, 1, CONV_CB), lambda j, i: (0, 0, j)),
                   pl.BlockSpec((1, CONV_CB), lambda j, i: (0, j))),
        out_shape=(_sds((s, D_XBC), BF16), _sds((4, 1, D_XBC), F32), _sds((1, D_XBC), F32)),
        compiler_params=_cp("parallel", "arbitrary"),
    )(proj, proj, proj, dxs, dxs, d_b, d_b, d_c, d_c, w, b)


SC_U0 = (D + D_XBC) // CONV_CB
SC_B0 = SC_U0 + D // CONV_CB
SC_C0 = SC_B0 + D // CONV_CB


def _sc_fwd(proj, w, gain, *, name):
    s = proj.shape[0]
    tb = min(TB, s)

    def body(u_ref, b_ref, c_ref, up_ref, cp_ref, w_ref, g_ref, o_ref):
        i = pl.program_id(0)
        cup = jnp.where(i > 0, cp_ref[...] * up_ref[...], 0.0)
        cat = jnp.concatenate([cup, c_ref[...] * u_ref[...]], axis=0)
        v = b_ref[...] * _conv_taps(_taps(cat, 3), w_ref, HALO)
        o_ref[...] = (v * _group_rstd(v, SC_GROUP_W) * g_ref[...]).astype(o_ref.dtype)

    def cur(c0):
        return pl.BlockSpec((tb, CONV_CB), lambda i, j: (i, c0 + j))

    def prev(c0):
        return pl.BlockSpec((HALO, CONV_CB), lambda i, j: (_prev_rows(i, tb), c0 + j))

    return pl.pallas_call(
        body, name=name, grid=(s // tb, D // CONV_CB),
        in_specs=[cur(SC_U0), cur(SC_B0), cur(SC_C0), prev(SC_U0), prev(SC_C0),
                  pl.BlockSpec((3, CONV_CB), lambda i, j: (0, j)), pl.BlockSpec((1, CONV_CB), lambda i, j: (0, j))],
        out_specs=pl.BlockSpec((tb, CONV_CB), lambda i, j: (i, j)), out_shape=_sds((s, D), BF16),
        compiler_params=_cp("parallel", "parallel"),
    )(proj, proj, proj, proj, proj, w, gain)


def _sc_bwd(proj, dcat, w, gain, *, name):
    s = proj.shape[0]
    tb = min(TB, s)
    n_i = s // tb
    dy0 = D // CONV_CB

    def body(u_ref, b_ref, c_ref, up_ref, cp_ref, un_ref, bn_ref, cn_ref, dy_ref, dyn_ref, w_ref, g_ref,
             du_ref, db_ref, dc_ref, dw_ref, dg_ref):
        i = pl.program_id(1)
        u, c = u_ref[...], c_ref[...]
        cup = jnp.where(i > 0, cp_ref[...] * up_ref[...], 0.0)
        taps = _taps(jnp.concatenate([cup, c * u, cn_ref[...] * un_ref[...]], axis=0), 3)
        conv = _conv_taps(taps, w_ref, HALO)
        b_ext = jnp.concatenate([b_ref[...], bn_ref[...]], axis=0)
        dy_ext = jnp.concatenate([dy_ref[...].astype(F32), jnp.where(i < n_i - 1, dyn_ref[...].astype(F32), 0.0)],
                                 axis=0)
        v = b_ext * conv
        r = _group_rstd(v, SC_GROUP_W)
        vhat = v * r
        dvhat = dy_ext * g_ref[...]
        dv = r * (dvhat - vhat * _group_mean(dvhat * vhat, SC_GROUP_W))
        dconv = dv * b_ext
        n2 = tb + HALO
        dcu = (w_ref[2:3, :] * dconv[:tb] + w_ref[1:2, :] * pltpu.roll(dconv, n2 - 1, 0)[:tb]
               + w_ref[0:1, :] * pltpu.roll(dconv, n2 - 2, 0)[:tb])
        du_ref[...] = (dcu * c).astype(du_ref.dtype)
        dc_ref[...] = (dcu * u).astype(dc_ref.dtype)
        db_ref[...] = (dv * conv)[:tb].astype(db_ref.dtype)
        dcc = dconv[:tb]
        parts = [jnp.sum(dcc * taps[k][HALO:HALO + tb], axis=0, keepdims=True) for k in range(3)]
        dgp = jnp.sum((dy_ext * vhat)[:tb], axis=0, keepdims=True)

        @pl.when(i == 0)
        def _():
            for k in range(3):
                dw_ref[k] = parts[k]
            dg_ref[...] = dgp

        @pl.when(i > 0)
        def _():
            for k in range(3):
                dw_ref[k] += parts[k]
            dg_ref[...] += dgp

    def cur(c0):
        return pl.BlockSpec((tb, CONV_CB), lambda j, i: (i, c0 + j))

    def prev(c0):
        return pl.BlockSpec((HALO, CONV_CB), lambda j, i: (_prev_rows(i, tb), c0 + j))

    def nxt(c0):
        return pl.BlockSpec((HALO, CONV_CB), lambda j, i: (_next_rows(i, tb, s), c0 + j))

    vec = pl.BlockSpec((1, CONV_CB), lambda j, i: (0, j))
    out_row = pl.BlockSpec((tb, CONV_CB), lambda j, i: (i, j))
    return pl.pallas_call(
        body, name=name, grid=(D // CONV_CB, n_i),
        in_specs=[cur(SC_U0), cur(SC_B0), cur(SC_C0), prev(SC_U0), prev(SC_C0), nxt(SC_U0), nxt(SC_B0), nxt(SC_C0),
                  cur(dy0), nxt(dy0), pl.BlockSpec((3, CONV_CB), lambda j, i: (0, j)), vec],
        out_specs=(out_row, out_row, out_row, pl.BlockSpec((3, 1, CONV_CB), lambda j, i: (0, 0, j)), vec),
        out_shape=(_sds((s, D), BF16), _sds((s, D), BF16), _sds((s, D), BF16), _sds((3, 1, D), F32), _sds((1, D), F32)),
        compiler_params=_cp("parallel", "arbitrary"),
    )(proj, proj, proj, proj, proj, proj, proj, proj, dcat, dcat, w, gain)


def _gated_norm_fwd(y, proj, gain, *, name):
    s = y.shape[0]
    tb = min(TB, s)

    def body(y_ref, z_ref, g_ref, o_ref):
        z = z_ref[...]
        t = y_ref[...] * z * _sigmoid(z)
        o_ref[...] = (t * _group_rstd(t, SSD_GROUP_W) * g_ref[...]).astype(o_ref.dtype)

    row = pl.BlockSpec((tb, D), lambda i: (i, 0))
    return pl.pallas_call(
        body, name=name, grid=(s // tb,), in_specs=[row, row, pl.BlockSpec((1, D), lambda i: (0, 0))], out_specs=row,
        out_shape=_sds((s, D), BF16), compiler_params=_cp("parallel"),
    )(y, proj, gain)


def _gated_norm_bwd(y, proj, dcat, gain, *, name):
    s = y.shape[0]
    tb = min(TB, s)

    def body(y_ref, z_ref, d_ref, g_ref, dy_ref, dz_ref, dg_ref):
        z, yv, dout = z_ref[...], y_ref[...], d_ref[...].astype(F32)
        sg = _sigmoid(z)
        sz = z * sg
        t = yv * sz
        r = _group_rstd(t, SSD_GROUP_W)
        that = t * r
        dthat = dout * g_ref[...]
        dt = r * (dthat - that * _group_mean(dthat * that, SSD_GROUP_W))
        dy_ref[...] = dt * sz
        dz_ref[...] = (dt * yv * sg * (1.0 + z * (1.0 - sg))).astype(dz_ref.dtype)
        part = jnp.sum(dout * that, axis=0, keepdims=True)

        @pl.when(pl.program_id(0) == 0)
        def _():
            dg_ref[...] = part

        @pl.when(pl.program_id(0) > 0)
        def _():
            dg_ref[...] += part

    row = pl.BlockSpec((tb, D), lambda i: (i, 0))
    vec = pl.BlockSpec((1, D), lambda i: (0, 0))
    return pl.pallas_call(
        body, name=name, grid=(s // tb,), in_specs=[row, row, row, vec], out_specs=(row, row, vec),
        out_shape=(_sds((s, D), F32), _sds((s, D), BF16), _sds((1, D), F32)), compiler_params=_cp("arbitrary"),
    )(y, proj, dcat, gain)


def _tri(lower):
    row = lax.broadcasted_iota(jnp.int32, (CHUNK, CHUNK), 0)
    col = lax.broadcasted_iota(jnp.int32, (CHUNK, CHUNK), 1)
    return jnp.where(row >= col if lower else col >= row, 1.0, 0.0).astype(BF16)


def _head_spread():
    row = lax.broadcasted_iota(jnp.int32, (DT_W, D), 0)
    col = lax.broadcasted_iota(jnp.int32, (DT_W, D), 1)
    return jnp.where(col // HEAD_DIM == row, 1.0, 0.0).astype(BF16)


def _head_pick():
    row = lax.broadcasted_iota(jnp.int32, (D, DT_W), 0)
    col = lax.broadcasted_iota(jnp.int32, (D, DT_W), 1)
    return jnp.where(row == col * HEAD_DIM, 1.0, 0.0).astype(BF16)


def _dt_fwd(dt_raw, bias, a_log, *, name):
    s = dt_raw.shape[0]
    nc = s // CHUNK

    def body(raw_ref, bias_ref, alog_ref, dt_ref, cum_ref, cumt_ref):
        x = raw_ref[...] + bias_ref[...]
        dt = jnp.maximum(x, 0.0) + jnp.log1p(jnp.exp(-jnp.abs(x)))
        cum = _split3_dot(_tri(True), dt * (-jnp.exp(alog_ref[...])))
        spread = _head_spread()
        dt_ref[...] = _split3_dot_r(dt, spread)
        cum_ref[...] = _split3_dot_r(cum, spread)
        cumt_ref[0] = cum.T

    row = pl.BlockSpec((CHUNK, DT_W), lambda i: (i, 0))
    wide = pl.BlockSpec((CHUNK, D), lambda i: (i, 0))
    vec = pl.BlockSpec((1, DT_W), lambda i: (0, 0))
    return pl.pallas_call(
        body, name=name, grid=(nc,), in_specs=[row, vec, vec],
        out_specs=(wide, wide, pl.BlockSpec((1, DT_W, CHUNK), lambda i: (i, 0, 0))),
        out_shape=(_sds((s, D), F32), _sds((s, D), F32), _sds((nc, DT_W, CHUNK), F32)),
        compiler_params=_cp("parallel"),
    )(dt_raw, bias, a_log)


def _dt_bwd(dt_raw, bias, ddt_b, *, name):
    s = dt_raw.shape[0]
    tb = min(TB, s)

    def body(raw_ref, bias_ref, d_ref, o_ref, db_ref):
        g = _split3_dot_r(d_ref[...], _head_pick()) * _sigmoid(raw_ref[...] + bias_ref[...])
        o_ref[...] = g.astype(o_ref.dtype)
        part = jnp.sum(g, axis=0, keepdims=True)

        @pl.when(pl.program_id(0) == 0)
        def _():
            db_ref[...] = part

        @pl.when(pl.program_id(0) > 0)
        def _():
            db_ref[...] += part

    row = pl.BlockSpec((tb, DT_W), lambda i: (i, 0))
    vec = pl.BlockSpec((1, DT_W), lambda i: (0, 0))
    return pl.pallas_call(
        body, name=name, grid=(s // tb,), in_specs=[row, vec, pl.BlockSpec((tb, D), lambda i: (i, 0))],
        out_specs=(row, vec), out_shape=(_sds((s, DT_W), BF16), _sds((1, DT_W), F32)),
        compiler_params=_cp("arbitrary"),
    )(dt_raw, bias, ddt_b)


N_PAIRS = N_HEADS // 2
PAIRS_PER_GROUP = N_PAIRS // N_GROUPS
GROUP_W = PAIRS_PER_GROUP * LANES
B_CB0 = D // LANES
C_CB0 = B_CB0 + N_GROUPS


def _decay(cum_col, cum_row, causal):
    return jnp.where(causal, jnp.exp(jnp.minimum(cum_col - cum_row, 0.0)), 0.0)


def _causal_mask():
    row = lax.broadcasted_iota(jnp.int32, (CHUNK, CHUNK), 0)
    col = lax.broadcasted_iota(jnp.int32, (CHUNK, CHUNK), 1)
    return row >= col


def _state_row_scale(cumt_ref, pp):
    last0 = cumt_ref[0, 2 * pp][:, CHUNK - 1:CHUNK]
    last1 = cumt_ref[0, 2 * pp + 1][:, CHUNK - 1:CHUNK]
    rown = lax.broadcasted_iota(jnp.int32, (LANES, 1), 0)
    return jnp.exp(jnp.where(rown < HEAD_DIM, last0, last1))


def _ssd_specs(nc, rev):
    def ch(c):
        return nc - 1 - c if rev else c

    wide = pl.BlockSpec((CHUNK, GROUP_W), lambda c, g: (ch(c), g))
    vec = pl.BlockSpec((1, GROUP_W), lambda c, g: (0, g))
    cumt_spec = pl.BlockSpec((1, 2 * PAIRS_PER_GROUP, 1, CHUNK), lambda c, g: (ch(c), g, 0, 0))
    hp_spec = pl.BlockSpec((1, PAIRS_PER_GROUP, LANES, N_STATE), lambda c, g: (ch(c), g, 0, 0))

    def bc(c0):
        return pl.BlockSpec((CHUNK, LANES), lambda c, g: (ch(c), c0 + g))

    return wide, vec, cumt_spec, hp_spec, bc


def _ssd_fwd(xbc_c, dt_b, cum_b, cumt4, dskip_b, *, name):
    s = xbc_c.shape[0]
    nc = s // CHUNK

    def body(xs_ref, b_ref, c_ref, dt_ref, cum_ref, cumt_ref, dsk_ref, y_ref, hp_ref, state):
        c, g = pl.program_id(0), pl.program_id(1)

        @pl.when(c == 0)
        def _():
            state[pl.ds(g * PAIRS_PER_GROUP, PAIRS_PER_GROUP)] = jnp.zeros((PAIRS_PER_GROUP, LANES, N_STATE), F32)

        bb, cbm = b_ref[...].astype(BF16), c_ref[...].astype(BF16)
        cbv = lax.dot_general(cbm, bb, NT, preferred_element_type=F32)
        first = lax.broadcasted_iota(jnp.int32, (CHUNK, LANES), 1) < HEAD_DIM
        causal = _causal_mask()
        for pp in range(PAIRS_PER_GROUP):
            sl = slice(pp * LANES, (pp + 1) * LANES)
            xs, cum = xs_ref[:, sl], cum_ref[:, sl]
            xt = xs * dt_ref[:, sl]
            xtb = xt.astype(BF16)
            hp = state[g * PAIRS_PER_GROUP + pp]
            hp_ref[0, pp] = hp
            y = jnp.exp(cum) * lax.dot_general(cbm, hp.astype(BF16), NT, preferred_element_type=F32)
            for hh in range(2):
                lm = _decay(cum[:, hh * HEAD_DIM:hh * HEAD_DIM + 1], cumt_ref[0, 2 * pp + hh], causal)
                xm = jnp.where(first if hh == 0 else jnp.logical_not(first), xtb, jnp.zeros_like(xtb))
                y = y + jnp.dot((cbv * lm).astype(BF16), xm, preferred_element_type=F32)
            decs = jnp.exp(cum[CHUNK - 1:CHUNK, :] - cum)
            st = lax.dot_general((xt * decs).astype(BF16), bb, TN, preferred_element_type=F32)
            state[g * PAIRS_PER_GROUP + pp] = _state_row_scale(cumt_ref, pp) * hp + st
            y_ref[:, sl] = y + xs * dsk_ref[:, sl]

    wide, vec, cumt_spec, hp_spec, bc = _ssd_specs(nc, False)
    return pl.pallas_call(
        body, name=name, grid=(nc, N_GROUPS),
        in_specs=[wide, bc(B_CB0), bc(C_CB0), wide, wide, cumt_spec, vec],
        out_specs=(wide, hp_spec),
        out_shape=(_sds((s, D), F32), _sds((nc, N_PAIRS, LANES, N_STATE), F32)),
        scratch_shapes=[pltpu.VMEM((N_PAIRS, LANES, N_STATE), F32)],
        compiler_params=_cp("arbitrary", "arbitrary"),
    )(xbc_c, xbc_c, xbc_c, dt_b, cum_b, cumt4, dskip_b)


def _ssd_bwd(xbc_c, dt_b, cum_b, cumt4, hprev, dy, alog_b, dskip_b, *, name):
    s = xbc_c.shape[0]
    nc = s // CHUNK

    def body(xs_ref, b_ref, c_ref, dt_ref, cum_ref, cumt_ref, hp_ref, dy_ref, alog_ref, dsk_ref,
             dxs_ref, db_ref, dc_ref, ddt_ref, dalog_ref, ddsk_ref, dstate):
        c, g = pl.program_id(0), pl.program_id(1)
        pairs = pl.ds(g * PAIRS_PER_GROUP, PAIRS_PER_GROUP)

        @pl.when(c == 0)
        def _():
            dstate[pairs] = jnp.zeros((PAIRS_PER_GROUP, LANES, N_STATE), F32)
            dalog_ref[pairs] = jnp.zeros((PAIRS_PER_GROUP, 1, LANES), F32)
            ddsk_ref[pairs] = jnp.zeros((PAIRS_PER_GROUP, 1, LANES), F32)

        bb, cbm = b_ref[...].astype(BF16), c_ref[...].astype(BF16)
        cbv = lax.dot_general(cbm, bb, NT, preferred_element_type=F32)
        first = lax.broadcasted_iota(jnp.int32, (CHUNK, LANES), 1) < HEAD_DIM
        causal = _causal_mask()
        rown = lax.broadcasted_iota(jnp.int32, (LANES, 1), 0)
        d_b = jnp.zeros((CHUNK, N_STATE), F32)
        d_c = jnp.zeros((CHUNK, N_STATE), F32)
        for pp in range(PAIRS_PER_GROUP):
            sl = slice(pp * LANES, (pp + 1) * LANES)
            idx = g * PAIRS_PER_GROUP + pp
            xs, dtb, cum, dy = xs_ref[:, sl], dt_ref[:, sl], cum_ref[:, sl], dy_ref[:, sl]
            a_b = -jnp.exp(alog_ref[:, sl])
            xt = xs * dtb
            xtb = xt.astype(BF16)
            dyb = dy.astype(BF16)
            hp = hp_ref[0, pp]
            hpb = hp.astype(BF16)
            dh = dstate[idx]
            dhb = dh.astype(BF16)
            exp_cum = jnp.exp(cum)
            decs = jnp.exp(cum[CHUNK - 1:CHUNK, :] - cum)
            row_scale = _state_row_scale(cumt_ref, pp)
            xd = (xt * decs).astype(BF16)
            dye = (dy * exp_cum).astype(BF16)
            zero_b = jnp.zeros_like(xtb)

            h_next = row_scale * hp + lax.dot_general(xd, bb, TN, preferred_element_type=F32)
            y = exp_cum * lax.dot_general(cbm, hpb, NT, preferred_element_type=F32)
            dxt = decs * lax.dot_general(bb, dhb, NT, preferred_element_type=F32)
            dcb = jnp.zeros((CHUNK, CHUNK), F32)
            for hh in range(2):
                mask = first if hh == 0 else jnp.logical_not(first)
                lm = _decay(cum[:, hh * HEAD_DIM:hh * HEAD_DIM + 1], cumt_ref[0, 2 * pp + hh], causal)
                m = (cbv * lm).astype(BF16)
                y = y + jnp.dot(m, jnp.where(mask, xtb, zero_b), preferred_element_type=F32)
                dxt = dxt + jnp.where(mask, lax.dot_general(m, dyb, TN, preferred_element_type=F32), 0.0)
                dm = lax.dot_general(jnp.where(mask, dyb, zero_b), xtb, NT, preferred_element_type=F32)
                dcb = dcb + dm * lm
            dcbb = dcb.astype(BF16)
            d_c = d_c + jnp.dot(dcbb, bb, preferred_element_type=F32) + jnp.dot(dye, hpb, preferred_element_type=F32)
            d_b = (d_b + lax.dot_general(dcbb, cbm, TN, preferred_element_type=F32)
                   + jnp.dot(xd, dhb, preferred_element_type=F32))
            dstate[idx] = row_scale * dh + lax.dot_general(dye, cbm, TN, preferred_element_type=F32)

            d_cum = _pair_sum(dyb.astype(F32) * y - dxt * xtb.astype(F32), first)
            e = jnp.sum(dh * h_next, axis=1, keepdims=True)
            t0 = jnp.sum(jnp.where(rown < HEAD_DIM, e, 0.0), axis=0, keepdims=True)
            t1 = jnp.sum(jnp.where(rown < HEAD_DIM, 0.0, e), axis=0, keepdims=True)
            d_da = _split3_dot(_tri(False), d_cum) + jnp.where(first[0:1, :], t0, t1)
            ddt_ref[:, sl] = a_b * d_da + _pair_sum(dxt * xs, first)
            dxs_ref[:, sl] = dxt * dtb + dy * dsk_ref[:, sl]
            dalog_ref[idx] += jnp.sum(d_da * dtb * a_b, axis=0, keepdims=True)
            ddsk_ref[idx] += jnp.sum(_pair_sum(dy * xs, first), axis=0, keepdims=True)
        db_ref[...] = d_b
        dc_ref[...] = d_c

    wide, vec, cumt_spec, hp_spec, bc = _ssd_specs(nc, True)
    acc = pl.BlockSpec((N_PAIRS, 1, LANES), lambda c, g: (0, 0, 0))
    return pl.pallas_call(
        body, name=name, grid=(nc, N_GROUPS),
        in_specs=[wide, bc(B_CB0), bc(C_CB0), wide, wide, cumt_spec, hp_spec, wide, vec, vec],
        out_specs=(wide, bc(0), bc(0), wide, acc, acc),
        out_shape=(_sds((s, D), F32), _sds((s, N_GROUPS * N_STATE), F32), _sds((s, N_GROUPS * N_STATE), F32),
                   _sds((s, D), F32), _sds((N_PAIRS, 1, LANES), F32), _sds((N_PAIRS, 1, LANES), F32)),
        scratch_shapes=[pltpu.VMEM((N_PAIRS, LANES, N_STATE), F32)],
        compiler_params=_cp("arbitrary", "arbitrary"),
    )(xbc_c, xbc_c, xbc_c, dt_b, cum_b, cumt4, hprev, dy, alog_b, dskip_b)


def _lane_bcast(v):
    return jnp.repeat(v, HEAD_DIM, axis=1)


DT0 = D + D_XBC
W_IN_SHARD = (DT0 + N_HEADS + 3 * D) // N_DEV
DT_PIECE = DT0 // W_IN_SHARD
DT_OFF = DT0 - DT_PIECE * W_IN_SHARD
assert DT_OFF + N_HEADS <= W_IN_SHARD


def _split_w_in(got):
    hold = got[DT_PIECE]
    cols = ([got[d] for d in range(DT_PIECE)] + [hold[:, :DT_OFF], hold[:, DT_OFF + N_HEADS:]]
            + [got[d] for d in range(DT_PIECE + 1, N_DEV)])
    return (jnp.concatenate(cols, axis=1),
            jnp.pad(hold[:, DT_OFF:DT_OFF + N_HEADS], ((0, 0), (0, DT_W - N_HEADS))))


def _w_in_pieces(dw_main, dw_dt):
    pieces = []
    for d in range(N_DEV):
        lo = d * W_IN_SHARD
        if d < DT_PIECE:
            pieces.append(dw_main[:, lo:lo + W_IN_SHARD])
        elif d == DT_PIECE:
            pieces.append(jnp.concatenate(
                [dw_main[:, lo:DT0], dw_dt[:, :N_HEADS], dw_main[:, DT0:lo + W_IN_SHARD - N_HEADS]], axis=1))
        else:
            pieces.append(dw_main[:, lo - N_HEADS:lo - N_HEADS + W_IN_SHARD])
    return jnp.stack(pieces)


def _prep_small(conv_w, conv_b, dt_bias, a_log, d_skip):
    pad = DT_W - N_HEADS
    return dict(
        conv_w=conv_w, conv_b=conv_b.reshape(1, D_XBC),
        dt_bias=jnp.pad(dt_bias.reshape(1, N_HEADS), ((0, 0), (0, pad))),
        a_log=jnp.pad(a_log.reshape(1, N_HEADS), ((0, 0), (0, pad))),
        alog_b=_lane_bcast(a_log.reshape(1, N_HEADS)), dskip_b=_lane_bcast(d_skip.reshape(1, N_HEADS)))


def _layer_fwd(h0, memn, p, g, li, comm=None):
    s = h0.shape[0]
    nc = s // CHUNK
    n = f"l{li}_"

    def hosted(what, *args, **kw):
        side = comm.ag_side(what, li) if comm is not None else None
        if side is None:
            return _mm(*args, **kw)
        out, got = _mm(*args, side=side, **kw)
        comm.after_ag(what, li, p, got)
        return out

    if comm is not None:
        comm.before_layer(li, p)
    hn = _norm_fwd(h0, g["norm_mix"], name=n + "norm_mix")
    proj = hosted("proj", hn, p["w_main"], name=n + "proj")
    dt_raw = _mm(hn, p["w_dt"], name=n + "proj_dt")
    xbc_c = _conv_silu_fwd(proj, p["conv_w"], p["conv_b"], name=n + "conv")
    dt_b, cum_b, cumt = _dt_fwd(dt_raw, p["dt_bias"], p["a_log"], name=n + "dt")
    cumt4 = cumt[:, :N_HEADS, :].reshape(nc, N_HEADS, 1, CHUNK)
    y, hprev = _ssd_fwd(xbc_c, dt_b, cum_b, cumt4, p["dskip_b"], name=n + "ssd")
    y_ssd = _gated_norm_fwd(y, proj, g["ssd_norm"], name=n + "gnorm")
    y_sc = _sc_fwd(proj, p["sc_conv_w"], g["sc_norm"], name=n + "sc")
    cat = jnp.concatenate([y_ssd, y_sc], axis=1)
    h1 = hosted("out", cat, p["w_out"], res=h0, bm=512, j_outer=True, name=n + "out")
    hx = _norm_fwd(h1, g["norm_xa"], name=n + "norm_xa")
    q = _mm(hx, p["w_q"], out_dtype=BF16, name=n + "q")
    k = _mm(memn, p["w_k"], out_dtype=BF16, name=n + "k")
    v = _mm(memn, p["w_v"], out_dtype=BF16, name=n + "v")
    o = _attn_fwd(q, k, v, name=n + "attn")
    h2 = _mm(o, p["w_o"], res=h1, name=n + "o")
    hf = _norm_fwd(h2, g["norm_ffn"], name=n + "norm_ffn")
    gu = hosted("gu", hf, p["w_gu"], out_dtype=BF16, name=n + "gu")
    act = _swiglu_fwd(gu, name=n + "swiglu")
    h3 = hosted("down", act, p["w_down"], res=h2, bm=512, j_outer=True, name=n + "down")
    saved = dict(h0=h0, hn=hn, proj=proj, dt_raw=dt_raw, xbc_c=xbc_c, dt_b=dt_b, cum_b=cum_b, cumt4=cumt4, hprev=hprev,
                 y=y, cat=cat, h1=h1, hx=hx, q=q, k=k, v=v, o=o, h2=h2, hf=hf, gu=gu, act=act)
    return h3, saved


def _layer_bwd(dh, dhb, dmemn, memn, p, g, sv, li, comm=None):
    n = f"l{li}b_"
    gr = {}

    def hosted(names, *args, **kw):
        side = comm.grad_side(li, names, gr) if comm is not None else None
        if side is None:
            return _mm(*args, **kw)
        out, got = _mm(*args, side=side, **kw)
        comm.after_grads(li, names, got)
        return out

    wide = dict(bm=512, bn=512)
    dact = _mm(dhb, p["w_down"], tb=True, bn=512, out_dtype=BF16, name=n + "dact")
    gr["w_down"] = hosted(("carry",), sv["act"], dhb, ta=True, out_dtype=BF16, name=n + "dw_down", **wide)
    dgu = _swiglu_bwd(sv["gu"], dact, name=n + "swiglu")
    dw_gu = hosted(("w_down",), sv["hf"], dgu, ta=True, out_dtype=BF16, name=n + "dw_gu", **wide)
    gr["w_gate"], gr["w_up"] = dw_gu[:, :D_FF], dw_gu[:, D_FF:]
    dhf = hosted(("w_gate",), dgu, p["w_gu"], tb=True, j_outer=True, out_dtype=BF16, name=n + "dhf", **wide)
    dh, dhb, gr["norm_ffn"] = _norm_bwd(sv["h2"], g["norm_ffn"], dhf, dh, name=n + "norm_ffn")
    do = _mm(dhb, p["w_o"], tb=True, out_dtype=BF16, name=n + "do")
    gr["w_o"] = _mm(sv["o"], dhb, ta=True, out_dtype=BF16, name=n + "dw_o", **wide)
    dq, dk, dv = _attn_bwd(sv["q"], sv["k"], sv["v"], do, name=n + "attn")
    gr["w_q"] = _mm(sv["hx"], dq, ta=True, out_dtype=BF16, name=n + "dw_q", **wide)
    gr["w_k"] = _mm(memn, dk, ta=True, out_dtype=BF16, name=n + "dw_k")
    gr["w_v"] = _mm(memn, dv, ta=True, out_dtype=BF16, name=n + "dw_v")
    dmemn = _mm(dk, p["w_k"], tb=True, res=dmemn, name=n + "dmem_k")
    dmemn = _mm(dv, p["w_v"], tb=True, res=dmemn, name=n + "dmem_v")
    dhx = _mm(dq, p["w_q"], tb=True, out_dtype=BF16, name=n + "dhx")
    dh, dhb, gr["norm_xa"] = _norm_bwd(sv["h1"], g["norm_xa"], dhx, dh, name=n + "norm_xa")
    dcat = hosted(("w_q", "w_k", "w_v", "w_o"), dhb, p["w_out"], tb=True, out_dtype=BF16, name=n + "dcat")
    gr["w_out"] = _mm(sv["cat"], dhb, ta=True, out_dtype=BF16, name=n + "dw_out", **wide)
    du, dgb, dgc, dsc_w, gr["sc_norm"] = _sc_bwd(sv["proj"], dcat, p["sc_conv_w"], g["sc_norm"], name=n + "sc")
    gr["sc_conv_w"] = dsc_w.reshape(3, D)
    dy, dz, gr["ssd_norm"] = _gated_norm_bwd(sv["y"], sv["proj"], dcat, g["ssd_norm"], name=n + "gnorm")
    dxs, d_b, d_c, ddt_b, dalog, ddsk = _ssd_bwd(sv["xbc_c"], sv["dt_b"], sv["cum_b"], sv["cumt4"], sv["hprev"], dy,
                                                 p["alog_b"], p["dskip_b"], name=n + "ssd")
    gr["a_log"] = dalog.reshape(N_HEADS, HEAD_DIM)[:, 0]
    gr["d_skip"] = ddsk.reshape(N_HEADS, HEAD_DIM)[:, 0]
    dxbc, dconv_w, dconv_b = _conv_silu_bwd(sv["proj"], dxs, d_b, d_c, p["conv_w"], p["conv_b"], name=n + "conv")
    gr["ssd_conv_w"] = dconv_w.reshape(4, D_XBC)
    gr["ssd_conv_b"] = dconv_b.reshape(D_XBC)
    ddt_raw, ddt_bias = _dt_bwd(sv["dt_raw"], p["dt_bias"], ddt_b, name=n + "dt")
    gr["dt_bias"] = ddt_bias[0, :N_HEADS]
    dproj = jnp.concatenate([dz, dxbc, du, dgb, dgc], axis=1)
    dw_main = hosted(("w_up", "w_out"), sv["hn"], dproj, ta=True, out_dtype=BF16, name=n + "dw_main", **wide)
    dw_dt = _mm(sv["hn"], ddt_raw, ta=True, out_dtype=BF16, name=n + "dw_dt", **wide)
    gr["w_in"] = [_w_in_pieces(dw_main[rows], dw_dt[rows]) for rows in (slice(0, D // 2), slice(D // 2, D))]
    dhn = hosted(("w_in",), dproj, p["w_main"], tb=True, j_outer=True, name=n + "dhn", **wide)
    dhn = _mm(ddt_raw, p["w_dt"], tb=True, res=dhn, name=n + "dhn_dt")
    dh, dhb, gr["norm_mix"] = _norm_bwd(sv["h0"], g["norm_mix"], dhn, dh, name=n + "norm_mix")
    return dh, dhb, dmemn, gr


def _local_step(x, mem, tgt, layers, gains, mem_norm, norm_final, comm=None):
    depth = len(layers)
    memn_f = _norm_fwd(mem, mem_norm, out_dtype=F32, name="mem_norm")
    memn = memn_f.astype(BF16)
    h = x
    saved = []
    for li in range(depth):
        h, sv = _layer_fwd(h, memn, layers[li], gains[li], li, comm)
        saved.append(sv)
    loss, dh, dhb, d_final = _loss_head(h, norm_final, tgt, name="loss_head")
    dmemn = jnp.zeros(mem.shape, F32)
    grads = [None] * depth
    for li in reversed(range(depth)):
        dh, dhb, dmemn, grads[li] = _layer_bwd(dh, dhb, dmemn, memn, layers[li], gains[li], saved[li], li, comm)
    _, _, d_mem_norm = _norm_bwd(mem, mem_norm, dmemn, None, name="mem_norm_b")
    return loss, dh, grads, d_mem_norm, d_final


ADAMW_ROWS = (64, 32, 16, 8)


def _adamw(parts, w, m, v, *, name):
    n_parts = len(parts)
    r, c_dim = parts[0].shape[1:]
    per_layer = n_parts // w.shape[0]
    assert w.shape == (n_parts // per_layer, per_layer * r, c_dim), (name, w.shape, parts[0].shape)
    tr = next(t for t in ADAMW_ROWS if r % t == 0)
    nb = r // tr

    def body(*refs):
        p_refs = refs[:n_parts]
        w_ref, m_ref, v_ref, g_ref, d_ref, nm_ref, nv_ref = refs[n_parts:]
        for l in range(n_parts):
            @pl.when(pl.program_id(0) == l)
            def _(l=l):
                g = p_refs[l][0].astype(F32)
                for s in range(1, N_DEV):
                    g = g + p_refs[l][s].astype(F32)
                g_ref[...] = g

        g = g_ref[...]
        m2 = ADAM_B1 * m_ref[...] + (1.0 - ADAM_B1) * g
        v2 = ADAM_B2 * v_ref[...] + (1.0 - ADAM_B2) * (g * g)
        m_hat = m2 / (1.0 - ADAM_B1 ** ADAM_STEP)
        v_hat = v2 / (1.0 - ADAM_B2 ** ADAM_STEP)
        d_ref[...] = -ADAM_LR * (m_hat / (jnp.sqrt(v_hat) + ADAM_EPS) + ADAM_WD * w_ref[...])
        nm_ref[...] = m2
        nv_ref[...] = v2

    def part_spec(l):
        return pl.BlockSpec((N_DEV, tr, c_dim),
                            lambda lay, i: (0, jnp.where(lay == l, i, jnp.where(lay < l, 0, nb - 1)), 0))

    row = pl.BlockSpec((None, tr, c_dim), lambda lay, i: (lay // per_layer, (lay % per_layer) * nb + i, 0))
    out = _sds(w.shape, F32)
    return pl.pallas_call(
        body, name=name, grid=(n_parts, nb), in_specs=[part_spec(l) for l in range(n_parts)] + [row, row, row],
        out_specs=(row, row, row, row), out_shape=(out, out, out, out), compiler_params=_cp("arbitrary", "arbitrary"),
    )(*parts, w, m, v)


BIG = ("w_in", "w_out", "w_q", "w_k", "w_v", "w_o", "w_gate", "w_up", "w_down")
COL_SHARDED = ("w_in", "w_o", "w_gate", "w_up")
PROJ_HOSTED = ("w_out", "w_q", "w_k", "w_v", "w_o", "w_gate")
SMALL_REPL = ("norm_mix", "ssd_conv_b", "dt_bias", "a_log", "d_skip", "ssd_norm", "sc_norm", "mem_norm", "norm_xa",
              "norm_ffn", "norm_final")
SMALL_SHARDED = ("ssd_conv_w", "sc_conv_w")
WEIGHTS = ("norm_mix", "w_in", "ssd_conv_w", "ssd_conv_b", "dt_bias", "a_log", "d_skip", "ssd_norm", "sc_conv_w",
           "sc_norm", "w_out", "mem_norm", "norm_xa", "w_q", "w_k", "w_v", "w_o", "norm_ffn", "w_gate", "w_up",
           "w_down", "norm_final")


def _assemble(got, col_sharded):
    l, _, r, c_dim = got.shape
    if col_sharded:
        return jnp.transpose(got, (0, 2, 1, 3)).reshape(l, r, N_DEV * c_dim)
    return got.reshape(l, N_DEV * r, c_dim)


def _to_pieces(full, col_sharded):
    rr, cc = full.shape
    if col_sharded:
        return jnp.transpose(full.reshape(rr, N_DEV, cc // N_DEV), (1, 0, 2))
    return full.reshape(N_DEV, rr // N_DEV, cc)


class _Comm:
    def __init__(self, w, first_w_in):
        self.w = w
        self.depth = w["w_in"].shape[0]
        self.next_w_in = first_w_in
        self.next_w_down = None
        self.gate = None
        self.carry = None
        self.got = {nm: [None] * self.depth for nm in BIG}
        self.got["w_in"] = [None] * (2 * self.depth)

    def _wanted(self, what, li):
        more = li + 1 < self.depth
        if what == "proj":
            return [(nm, li) for nm in PROJ_HOSTED]
        if what == "out":
            return [("w_up", li)]
        if what == "gu":
            return ([("w_in", li + 1)] if more else []) + ([("w_down", 0)] if li == 0 else [])
        return [("w_down", li + 1)] if more else []

    def before_layer(self, li, p):
        p["w_main"], p["w_dt"] = _split_w_in(self.next_w_in)
        if li > 0:
            p["w_down"] = self.next_w_down

    def ag_side(self, what, li):
        wanted = self._wanted(what, li)
        return _ag_side([self.w[nm][l:l + 1].astype(BF16) for nm, l in wanted]) if wanted else None

    def after_ag(self, what, li, p, got):
        for (nm, l), g in zip(self._wanted(what, li), got):
            if nm == "w_in":
                self.next_w_in = g[0]
                continue
            full = _assemble(g, nm in COL_SHARDED)[0]
            if nm == "w_gate":
                self.gate = full
            elif nm == "w_up":
                p["w_gu"] = jnp.concatenate([self.gate, full], axis=1)
            elif nm == "w_down" and l > li:
                self.next_w_down = full
            else:
                p[nm] = full

    def _grad_jobs(self, li, names):
        jobs = []
        for nm in names:
            if nm == "w_in":
                jobs += [("w_in", 2 * li)] + ([("w_in", 1)] if li == 0 else [])
            elif nm == "carry":
                jobs += [("w_in", 2 * li + 3)] if li + 1 < self.depth else []
            else:
                jobs.append((nm, li))
        return jobs

    def grad_side(self, li, names, gr):
        pieces = []
        for nm, slot in self._grad_jobs(li, names):
            if nm != "w_in":
                pieces.append(_to_pieces(gr[nm], nm in COL_SHARDED))
            elif slot == 2 * li + 3:
                pieces.append(self.carry)
            else:
                pieces.append(gr["w_in"][slot - 2 * li])
        if "w_in" in names and li > 0:
            self.carry = gr["w_in"][1]
        return _rs_side(pieces) if pieces else None

    def after_grads(self, li, names, got):
        for (nm, slot), g in zip(self._grad_jobs(li, names), got):
            self.got[nm][slot] = g


def _pack(arrs, names):
    flat = jnp.concatenate([arrs[nm].reshape(-1).astype(F32) for nm in names])
    rows = -(-flat.shape[0] // (TB * LANES)) * TB
    return jnp.pad(flat, (0, rows * LANES - flat.shape[0])).reshape(rows, LANES)


def _unpack(packed, shapes, names):
    flat = packed.reshape(-1)
    out, off = {}, 0
    for nm in names:
        size = 1
        for dim in shapes[nm]:
            size *= dim
        out[nm] = flat[off:off + size].reshape(shapes[nm])
        off += size
    return out


def kernel(x, mem, norm_mix, w_in, ssd_conv_w, ssd_conv_b, dt_bias, a_log, d_skip, ssd_norm, sc_conv_w, sc_norm, w_out, mem_norm, norm_xa, w_q, w_k, w_v, w_o, norm_ffn, w_gate, w_up, w_down, norm_final, loss_target, m_norm_mix, m_w_in, m_ssd_conv_w, m_ssd_conv_b, m_dt_bias, m_a_log, m_d_skip, m_ssd_norm, m_sc_conv_w, m_sc_norm, m_w_out, m_mem_norm, m_norm_xa, m_w_q, m_w_k, m_w_v, m_w_o, m_norm_ffn, m_w_gate, m_w_up, m_w_down, m_norm_final, v_norm_mix, v_w_in, v_ssd_conv_w, v_ssd_conv_b, v_dt_bias, v_a_log, v_d_skip, v_ssd_norm, v_sc_conv_w, v_sc_norm, v_w_out, v_mem_norm, v_norm_xa, v_w_q, v_w_k, v_w_v, v_w_o, v_norm_ffn, v_w_gate, v_w_up, v_w_down, v_norm_final):
    w = dict(norm_mix=norm_mix, w_in=w_in, ssd_conv_w=ssd_conv_w, ssd_conv_b=ssd_conv_b, dt_bias=dt_bias, a_log=a_log,
             d_skip=d_skip, ssd_norm=ssd_norm, sc_conv_w=sc_conv_w, sc_norm=sc_norm, w_out=w_out, mem_norm=mem_norm,
             norm_xa=norm_xa, w_q=w_q, w_k=w_k, w_v=w_v, w_o=w_o, norm_ffn=norm_ffn, w_gate=w_gate, w_up=w_up,
             w_down=w_down, norm_final=norm_final)
    mom = dict(norm_mix=m_norm_mix, w_in=m_w_in, ssd_conv_w=m_ssd_conv_w, ssd_conv_b=m_ssd_conv_b, dt_bias=m_dt_bias,
               a_log=m_a_log, d_skip=m_d_skip, ssd_norm=m_ssd_norm, sc_conv_w=m_sc_conv_w, sc_norm=m_sc_norm,
               w_out=m_w_out, mem_norm=m_mem_norm, norm_xa=m_norm_xa, w_q=m_w_q, w_k=m_w_k, w_v=m_w_v, w_o=m_w_o,
               norm_ffn=m_norm_ffn, w_gate=m_w_gate, w_up=m_w_up, w_down=m_w_down, norm_final=m_norm_final)
    var = dict(norm_mix=v_norm_mix, w_in=v_w_in, ssd_conv_w=v_ssd_conv_w, ssd_conv_b=v_ssd_conv_b, dt_bias=v_dt_bias,
               a_log=v_a_log, d_skip=v_d_skip, ssd_norm=v_ssd_norm, sc_conv_w=v_sc_conv_w, sc_norm=v_sc_norm,
               w_out=v_w_out, mem_norm=v_mem_norm, norm_xa=v_norm_xa, w_q=v_w_q, w_k=v_w_k, w_v=v_w_v, w_o=v_w_o,
               norm_ffn=v_norm_ffn, w_gate=v_w_gate, w_up=v_w_up, w_down=v_w_down, norm_final=v_norm_final)
    depth = w_in.shape[0]
    my = 4 * lax.axis_index("x") + 2 * lax.axis_index("y") + lax.axis_index("c")

    got = _run_side(_ag_side([w_in[0:1].astype(BF16), ssd_conv_w, sc_conv_w]), name="ag_first")
    comm = _Comm(w, got[0][0])
    conv_full = {"ssd_conv_w": _assemble(got[1], True), "sc_conv_w": _assemble(got[2], True)}
    layers, gains = [], []
    for li in range(depth):
        p = _prep_small(conv_full["ssd_conv_w"][li], ssd_conv_b[li], dt_bias[li], a_log[li], d_skip[li])
        p["sc_conv_w"] = conv_full["sc_conv_w"][li]
        layers.append(p)
        gains.append({nm: w[nm][li].reshape(1, D) for nm in ("norm_mix", "ssd_norm", "sc_norm", "norm_xa", "norm_ffn")})

    loss_v, grad_x, grads, d_mem_norm, d_final = _local_step(
        x[0], mem[0], loss_target[0], layers, gains, mem_norm.reshape(1, D), norm_final.reshape(1, D), comm)
    loss = lax.psum(loss_v[0, 0], ("x", "y", "c"))

    outs = {}
    for nm in BIG:
        outs[nm] = _adamw(comm.got[nm], w[nm], mom[nm], var[nm], name="adamw_" + nm)

    small = SMALL_REPL + SMALL_SHARDED
    gsmall = dict(mem_norm=d_mem_norm.reshape(D), norm_final=d_final.reshape(D))
    for nm in ("norm_mix", "ssd_norm", "sc_norm", "norm_xa", "norm_ffn"):
        gsmall[nm] = jnp.stack([gr[nm].reshape(D) for gr in grads])
    for nm in ("ssd_conv_b", "dt_bias", "a_log", "d_skip", "ssd_conv_w", "sc_conv_w"):
        gsmall[nm] = jnp.stack([gr[nm] for gr in grads])
    packed_g = _pack(gsmall, small)
    all_g = _run_side(_ag_side([packed_g[None]]), name="ag_small_grads")[0][0]

    def put_shard(arrs):
        loc = {nm: arrs[nm] for nm in SMALL_REPL}
        for nm in SMALL_SHARDED:
            cs = arrs[nm].shape[-1]
            loc[nm] = lax.dynamic_update_slice_in_dim(jnp.zeros(gsmall[nm].shape, F32), arrs[nm], my * cs, axis=2)
        return _pack(loc, small)

    res = _adamw([all_g], put_shard(w)[None], put_shard(mom)[None], put_shard(var)[None], name="adamw_small")
    shapes = {nm: gsmall[nm].shape for nm in small}
    for idx in range(4):
        un = _unpack(res[idx], shapes, small)
        for nm in SMALL_REPL:
            outs.setdefault(nm, [None] * 4)[idx] = un[nm]
        for nm in SMALL_SHARDED:
            cs = w[nm].shape[-1]
            outs.setdefault(nm, [None] * 4)[idx] = lax.dynamic_slice_in_dim(un[nm], my * cs, cs, axis=2)

    return (loss, grad_x[None], *[outs[nm][0] for nm in WEIGHTS], *[outs[nm][1] for nm in WEIGHTS],
            *[outs[nm][2] for nm in WEIGHTS], *[outs[nm][3] for nm in WEIGHTS])
```
